```python
import jax, jax.numpy as jnp
from jax import lax
import numpy as np

D_MODEL = 4096
BATCH = 4
SEQ = 4096
DEPTH = 2

CTX_LEN = 256
GRID_W = 64
EPS = 1e-6
N_MOD = 6

GLA_HEADS = 4
GLA_DV = (D_MODEL // 2) // GLA_HEADS
GLA_DK = GLA_DV // 2
GLA_KEY_W = GLA_HEADS * GLA_DK
GLA_VAL_W = GLA_HEADS * GLA_DV
GLA_GATE_RANK = 16
GLA_TAU = 16.0
GLA_CHUNK = 64

SGU_GROUPS = 8
SGU_W = D_MODEL // 2
SGU_CHUNK = 128

EVEN_SIZES = (GLA_KEY_W, GLA_KEY_W, GLA_VAL_W, GLA_VAL_W, GLA_GATE_RANK, GLA_GATE_RANK, SGU_W, SGU_W)
EVEN_IN = sum(EVEN_SIZES)
EVEN_MIX_W = GLA_VAL_W + SGU_W

HEAD_DIM = 128
ATT_HEADS = D_MODEL // HEAD_DIM
ATT_KV_HEADS = ATT_HEADS // 4
ATT_GROUP = ATT_HEADS // ATT_KV_HEADS
ODD_Q_W = ATT_HEADS * HEAD_DIM
ODD_KV_W = ATT_KV_HEADS * HEAD_DIM
ODD_IN = ODD_Q_W + 2 * ODD_KV_W
WINDOW = 128
ATT_BLOCK = 128
ROPE_THETA = 10000.0

FFN_HIDDEN = -(-8 * D_MODEL // (3 * 256)) * 256

kernel_name = "hybrid_gla_sgu_swa_prefix_dit"


def rms_norm(x, gain=None):
    xf = x.astype(jnp.float32)
    y = xf * lax.rsqrt(jnp.mean(xf * xf, axis=-1, keepdims=True) + EPS)
    if gain is not None:
        y = y * gain.astype(jnp.float32)
    return y.astype(x.dtype)


def modulate(x, shift, scale):
    return rms_norm(x) * (1.0 + scale) + shift


def swiglu(h, w_gate, w_up, w_down):
    return (jax.nn.silu(h @ w_gate) * (h @ w_up)) @ w_down


def gla_heads(q, k, v, g_f, g_b, w2_f, b2_f, w2_b, b2_b):
    B, L = q.shape[:2]

    def heads(t, d):
        return jnp.swapaxes(t.reshape(B, L, GLA_HEADS, d), 1, 2)

    def log_decay(g, w2, b2):
        return heads(jax.nn.log_sigmoid((g @ w2 + b2).astype(jnp.float32)) / GLA_TAU, GLA_DK)

    return (heads(q, GLA_DK) * GLA_DK ** -0.5, heads(k, GLA_DK), heads(v, GLA_DV),
            log_decay(g_f, w2_f, b2_f), log_decay(g_b, w2_b, b2_b))


def gla_chunk_scan(q, k, v, log_a, s0, strict):
    B, H, L, dk = q.shape
    dv = v.shape[-1]
    n = L // GLA_CHUNK

    def to_chunks(t):
        return jnp.moveaxis(t.reshape(B, H, n, GLA_CHUNK, t.shape[-1]), 2, 0)

    pos = jnp.arange(GLA_CHUNK)
    mask = (pos[:, None] > pos[None, :]) if strict else (pos[:, None] >= pos[None, :])

    def step(s, inp):
        qi, ki, vi, gi = [t.astype(jnp.float32) for t in inp]
        b = jnp.cumsum(gi, axis=-2)
        inter = jnp.einsum('bhtk,bhkv->bhtv', qi * jnp.exp(b), s)
        diff = b[..., :, None, :] - b[..., None, :, :]
        decay = jnp.exp(jnp.where(mask[:, :, None], diff, -jnp.inf))
        att = jnp.einsum('bhtik,bhik->bhti', qi[..., :, None, :] * decay, ki)
        intra = jnp.einsum('bhti,bhiv->bhtv', att, vi)
        b_last = b[..., -1, :]
        s_new = jnp.exp(b_last)[..., None] * s + jnp.einsum(
            'bhik,bhiv->bhkv', ki * jnp.exp(b_last[..., None, :] - b), vi)
        return s_new, inter + intra

    s_fin, o = lax.scan(step, s0, (to_chunks(q), to_chunks(k), to_chunks(v), to_chunks(log_a)))
    o = jnp.moveaxis(o, 0, 2).reshape(B, H, L, dv)
    return o.astype(v.dtype), s_fin


def gla_final_state(k, v, log_a):
    b = jnp.cumsum(log_a, axis=2)
    w = jnp.exp(b[:, :, -1:, :] - b)
    return jnp.einsum('bhlk,bhlv->bhkv', k.astype(jnp.float32) * w, v.astype(jnp.float32))


def gla_out(o, r, gain):
    B, H, L, dv = o.shape
    o = rms_norm(jnp.swapaxes(o, 1, 2), gain).reshape(B, L, H * dv)
    return o * jax.nn.silu(r)


def chunk_sgu(u, v, v_gain, w_s, b_s):
    B, L, _ = u.shape
    n = L // SGU_CHUNK
    gw = SGU_W // SGU_GROUPS
    u = jax.nn.gelu(u)
    vg = rms_norm(jax.nn.gelu(v).reshape(B, n, SGU_CHUNK, SGU_GROUPS, gw), v_gain.reshape(SGU_GROUPS, gw))
    mixed = jnp.einsum('gts,bnsgc->bntgc', w_s, vg) + jnp.swapaxes(b_s, 0, 1)[:, :, None]
    return u * mixed.reshape(B, L, SGU_W)


def split_even(proj):
    offs = np.cumsum(EVEN_SIZES)[:-1].tolist()
    return jnp.split(proj, offs, axis=-1)


def even_mixer(h_l, h_c, w_in, w2_f, b2_f, w2_b, b2_b, gla_gain, sgu_gain, w_s, b_s, w_out, need_ctx_out):
    q_l, k_l, v_l, r_l, gf_l, gb_l, u_l, s_l = split_even(h_l @ w_in)
    q_c, k_c, v_c, r_c, gf_c, gb_c, u_c, s_c = split_even(h_c @ w_in)
    ql, kl, vl, afl, abl = gla_heads(q_l, k_l, v_l, gf_l, gb_l, w2_f, b2_f, w2_b, b2_b)
    qc, kc, vc, afc, abc = gla_heads(q_c, k_c, v_c, gf_c, gb_c, w2_f, b2_f, w2_b, b2_b)
    rev = lambda t: jnp.flip(t, axis=2)
    B = h_l.shape[0]
    if need_ctx_out:
        s0 = jnp.zeros((B, GLA_HEADS, GLA_DK, GLA_DV), jnp.float32)
        oc_f, sc_f = gla_chunk_scan(qc, kc, vc, afc, s0, False)
        oc_b, sc_b = gla_chunk_scan(rev(qc), rev(kc), rev(vc), rev(abc), s0, True)
        o_c = oc_f + rev(oc_b)
    else:
        sc_f = gla_final_state(kc, vc, afc)
        sc_b = gla_final_state(rev(kc), rev(vc), rev(abc))
    ol_f, _ = gla_chunk_scan(ql, kl, vl, afl, sc_f, False)
    ol_b, _ = gla_chunk_scan(rev(ql), rev(kl), rev(vl), rev(abl), sc_b, True)
    o_l = ol_f + rev(ol_b)
    y_l = jnp.concatenate([gla_out(o_l, r_l, gla_gain), chunk_sgu(u_l, s_l, sgu_gain, w_s, b_s)], axis=-1) @ w_out
    if need_ctx_out:
        y_c = jnp.concatenate([gla_out(o_c, r_c, gla_gain), chunk_sgu(u_c, s_c, sgu_gain, w_s, b_s)], axis=-1) @ w_out
        return y_l, y_c
    return y_l, None


def axial_rope(L):
    rows = L // GRID_W
    row = jnp.repeat(jnp.arange(rows, dtype=jnp.float32), GRID_W)
    col = jnp.tile(jnp.arange(GRID_W, dtype=jnp.float32), rows)
    n_freq = HEAD_DIM // 4
    inv_freq = ROPE_THETA ** (-jnp.arange(n_freq, dtype=jnp.float32) / n_freq)
    ang = jnp.concatenate([row[:, None] * inv_freq, col[:, None] * inv_freq], axis=-1)
    return jnp.cos(ang), jnp.sin(ang)


def apply_rope(x, cos, sin):
    x1, x2 = jnp.split(x.astype(jnp.float32), 2, axis=-1)
    cs, sn = cos[:, None], sin[:, None]
    return jnp.concatenate([x1 * cs - x2 * sn, x1 * sn + x2 * cs], axis=-1).astype(x.dtype)


def sink_softmax(scores, sink):
    m = sink
    for s in scores:
        m = jnp.maximum(m, jnp.max(s, axis=-1, keepdims=True))
    ps = [jnp.exp(s - m) for s in scores]
    den = jnp.exp(sink - m)
    for p in ps:
        den = den + jnp.sum(p, axis=-1, keepdims=True)
    return [p / den for p in ps]


def window_attention(q, k, v, k_c, v_c, sink_kg):
    B, L = q.shape[:2]
    nb = L // ATT_BLOCK
    scale = HEAD_DIM ** -0.5
    qb = q.reshape(B, nb, ATT_BLOCK, ATT_KV_HEADS, ATT_GROUP, HEAD_DIM)

    def band(t):
        tp = jnp.pad(t, ((0, 0), (ATT_BLOCK, ATT_BLOCK), (0, 0), (0, 0)))
        tb = tp.reshape(B, nb + 2, ATT_BLOCK, ATT_KV_HEADS, HEAD_DIM)
        return jnp.concatenate([tb[:, :-2], tb[:, 1:-1], tb[:, 2:]], axis=2)

    kb, vb = band(k), band(v)
    qpos = jnp.arange(ATT_BLOCK)[:, None]
    kpos = jnp.arange(3 * ATT_BLOCK)[None, :] - ATT_BLOCK
    abs_k = jnp.arange(nb)[:, None, None] * ATT_BLOCK + kpos[None]
    valid = (jnp.abs(kpos - qpos) <= WINDOW)[None] & (abs_k >= 0) & (abs_k < L)
    s_loc = jnp.einsum('bnqkgd,bnskd->bnqkgs', qb, kb, preferred_element_type=jnp.float32) * scale
    s_loc = jnp.where(valid[None, :, :, None, None, :], s_loc, -jnp.inf)
    s_ctx = jnp.einsum('bnqkgd,bskd->bnqkgs', qb, k_c, preferred_element_type=jnp.float32) * scale
    p_loc, p_ctx = sink_softmax([s_loc, s_ctx], sink_kg)
    o = (jnp.einsum('bnqkgs,bnskd->bnqkgd', p_loc.astype(v.dtype), vb)
         + jnp.einsum('bnqkgs,bskd->bnqkgd', p_ctx.astype(v.dtype), v_c))
    return o.reshape(B, L, ODD_Q_W)


def context_attention(q, k, v, sink_kg):
    B, Lc = q.shape[:2]
    qg = q.reshape(B, Lc, ATT_KV_HEADS, ATT_GROUP, HEAD_DIM)
    s = jnp.einsum('bqkgd,bskd->bqkgs', qg, k, preferred_element_type=jnp.float32) * HEAD_DIM ** -0.5
    (p,) = sink_softmax([s], sink_kg)
    o = jnp.einsum('bqkgs,bskd->bqkgd', p.astype(v.dtype), v)
    return o.reshape(B, Lc, ODD_Q_W)


def odd_mixer(h_l, h_c, w_in, q_gain, k_gain, sink, w_out, need_ctx_out):
    def qkv(h):
        B, L = h.shape[:2]
        q, k, v = jnp.split(h @ w_in, [ODD_Q_W, ODD_Q_W + ODD_KV_W], axis=-1)
        q = rms_norm(q.reshape(B, L, ATT_HEADS, HEAD_DIM), q_gain)
        k = rms_norm(k.reshape(B, L, ATT_KV_HEADS, HEAD_DIM), k_gain)
        return q, k, v.reshape(B, L, ATT_KV_HEADS, HEAD_DIM)

    q_l, k_l, v_l = qkv(h_l)
    cos, sin = axial_rope(h_l.shape[1])
    q_l, k_l = apply_rope(q_l, cos, sin), apply_rope(k_l, cos, sin)
    q_c, k_c, v_c = qkv(h_c)
    sink_kg = sink.astype(jnp.float32).reshape(ATT_KV_HEADS, ATT_GROUP, 1)
    y_l = window_attention(q_l, k_l, v_l, k_c, v_c, sink_kg) @ w_out
    if need_ctx_out:
        return y_l, context_attention(q_c, k_c, v_c, sink_kg) @ w_out
    return y_l, None


def setup_inputs(seed: int = 0) -> dict:
    key = jax.random.key(seed)
    ks = iter(jax.random.split(key, 32))
    nrm = lambda shape, scale: jax.random.normal(next(ks), shape, jnp.float32) * scale
    ne, no = (DEPTH + 1) // 2, DEPTH // 2
    return {
        "x": nrm((BATCH, SEQ, D_MODEL), 1.0),
        "c": nrm((BATCH, D_MODEL), 1.0),
        "ctx": nrm((BATCH, CTX_LEN, D_MODEL), 1.0),
        "c_ctx": nrm((D_MODEL,), 1.0),
        "ada_w": nrm((DEPTH, D_MODEL, N_MOD * D_MODEL), 0.5 * D_MODEL ** -0.5),
        "ada_b": nrm((DEPTH, N_MOD * D_MODEL), 0.02),
        "ffn_w_gate": nrm((DEPTH, D_MODEL, FFN_HIDDEN), D_MODEL ** -0.5),
        "ffn_w_up": nrm((DEPTH, D_MODEL, FFN_HIDDEN), D_MODEL ** -0.5),
        "ffn_w_down": nrm((DEPTH, FFN_HIDDEN, D_MODEL), FFN_HIDDEN ** -0.5),
        "even_w_in": nrm((ne, D_MODEL, EVEN_IN), D_MODEL ** -0.5),
        "even_gate_w2_fwd": nrm((ne, GLA_GATE_RANK, GLA_KEY_W), GLA_GATE_RANK ** -0.5),
        "even_gate_b_fwd": nrm((ne, GLA_KEY_W), 0.1),
        "even_gate_w2_bwd": nrm((ne, GLA_GATE_RANK, GLA_KEY_W), GLA_GATE_RANK ** -0.5),
        "even_gate_b_bwd": nrm((ne, GLA_KEY_W), 0.1),
        "even_gla_norm_gain": 1.0 + nrm((ne, GLA_DV), 0.02),
        "even_sgu_norm_gain": 1.0 + nrm((ne, SGU_W), 0.02),
        "even_sgu_w_s": nrm((ne, SGU_GROUPS, SGU_CHUNK, SGU_CHUNK), SGU_CHUNK ** -0.5),
        "even_sgu_b_s": 1.0 + nrm((ne, SGU_GROUPS, SGU_CHUNK), 0.02),
        "even_w_out": nrm((ne, EVEN_MIX_W, D_MODEL), EVEN_MIX_W ** -0.5),
        "odd_w_in": nrm((no, D_MODEL, ODD_IN), D_MODEL ** -0.5),
        "odd_q_norm_gain": 1.0 + nrm((no, HEAD_DIM), 0.02),
        "odd_k_norm_gain": 1.0 + nrm((no, HEAD_DIM), 0.02),
        "odd_sink": nrm((no, ATT_HEADS), 1.0),
        "odd_w_out": nrm((no, ODD_Q_W, D_MODEL), ODD_Q_W ** -0.5),
    }


def reference(x, c, ctx, c_ctx, ada_w, ada_b, ffn_w_gate, ffn_w_up, ffn_w_down,
              even_w_in, even_gate_w2_fwd, even_gate_b_fwd, even_gate_w2_bwd, even_gate_b_bwd,
              even_gla_norm_gain, even_sgu_norm_gain, even_sgu_w_s, even_sgu_b_s, even_w_out,
              odd_w_in, odd_q_norm_gain, odd_k_norm_gain, odd_sink, odd_w_out):
    for i in range(DEPTH):
        need_ctx = i < DEPTH - 1
        j = i // 2
        sh1, sc1, g1, sh2, sc2, g2 = [m[:, None] for m in jnp.split(jax.nn.silu(c) @ ada_w[i] + ada_b[i], N_MOD, axis=-1)]
        csh1, csc1, cg1, csh2, csc2, cg2 = jnp.split(jax.nn.silu(c_ctx) @ ada_w[i] + ada_b[i], N_MOD, axis=-1)
        h_l = modulate(x, sh1, sc1)
        h_c = modulate(ctx, csh1, csc1)
        if i % 2 == 0:
            y_l, y_c = even_mixer(h_l, h_c, even_w_in[j], even_gate_w2_fwd[j], even_gate_b_fwd[j],
                                  even_gate_w2_bwd[j], even_gate_b_bwd[j], even_gla_norm_gain[j],
                                  even_sgu_norm_gain[j], even_sgu_w_s[j], even_sgu_b_s[j], even_w_out[j], need_ctx)
        else:
            y_l, y_c = odd_mixer(h_l, h_c, odd_w_in[j], odd_q_norm_gain[j], odd_k_norm_gain[j],
                                 odd_sink[j], odd_w_out[j], need_ctx)
        x = x + g1 * y_l
        x = x + g2 * swiglu(modulate(x, sh2, sc2), ffn_w_gate[i], ffn_w_up[i], ffn_w_down[i])
        if need_ctx:
            ctx = ctx + cg1 * y_c
            ctx = ctx + cg2 * swiglu(modulate(ctx, csh2, csc2), ffn_w_gate[i], ffn_w_up[i], ffn_w_down[i])
    return x
```

```python
import functools

import numpy as np
import jax
import jax.numpy as jnp
from jax import lax
from jax.experimental import pallas as pl
from jax.experimental.pallas import tpu as pltpu

F32 = jnp.float32
BF16 = jnp.bfloat16

EPS = 1e-6
N_MOD = 6
GRID_W = 64
ROPE_THETA = 10000.0

GLA_HEADS = 4
GLA_TAU = 16.0
GLA_GATE_RANK = 16
GLA_BLOCK = 64
SGU_GROUPS = 8
SGU_CHUNK = 128
HEAD_DIM = 128
ATT_GROUP = 4
WINDOW = 128
ATT_BLOCK = 128

LANE = 128
VMEM_LIMIT = 56 * 1024 * 1024

TM = 1024
TN = 512
FFN_TH = 256
ROWS_EW = 256


def _cparams(*sem):
    return pltpu.CompilerParams(dimension_semantics=sem, vmem_limit_bytes=VMEM_LIMIT)


def _dot(a, b):
    return jnp.dot(a, b, preferred_element_type=F32)


def _dot_nt(a, b):
    return lax.dot_general(a, b, (((1,), (1,)), ((), ())), preferred_element_type=F32)


def _dot_tn(a, b):
    return lax.dot_general(a, b, (((0,), (0,)), ((), ())), preferred_element_type=F32)


def _silu(x):
    return x * jax.nn.sigmoid(x)


def _gelu_tanh(x):
    c = np.float32(np.sqrt(2.0 / np.pi))
    return x * (0.5 * (1.0 + jnp.tanh(c * (x + 0.044715 * (x * x * x)))))


def _rms(x):
    return x * lax.rsqrt(jnp.mean(x * x, axis=-1, keepdims=True) + EPS)


def _ada_kernel(c_ref, w_ref, b_ref, o_ref):
    a = _silu(c_ref[...]).astype(BF16)
    o_ref[...] = _dot(a, w_ref[...].astype(BF16)) + b_ref[...]


def _ada(cc, ada_w, ada_b):
    depth, d, n = ada_w.shape
    rows = cc.shape[0]
    return pl.pallas_call(
        _ada_kernel,
        grid=(depth, n // TN),
        in_specs=[pl.BlockSpec((rows, d), lambda l, j: (0, 0)),
                  pl.BlockSpec((None, d, TN), lambda l, j: (l, 0, j)),
                  pl.BlockSpec((None, 1, TN), lambda l, j: (l, 0, j))],
        out_specs=pl.BlockSpec((None, rows, TN), lambda l, j: (l, 0, j)),
        out_shape=jax.ShapeDtypeStruct((depth, rows, n), F32),
        compiler_params=_cparams("parallel", "parallel"),
        name="ada",
    )(cc, ada_w, ada_b.reshape(depth, 1, n))


def _modulate_kernel(x_ref, sh_ref, sc_ref, h_ref):
    h_ref[...] = (_rms(x_ref[...]) * (1.0 + sc_ref[...]) + sh_ref[...]).astype(h_ref.dtype)


def _resmod_kernel(x_ref, f_ref, g_ref, sh_ref, sc_ref, xo_ref, h_ref):
    x = x_ref[...] + g_ref[...] * f_ref[...]
    xo_ref[...] = x
    h_ref[...] = (_rms(x) * (1.0 + sc_ref[...]) + sh_ref[...]).astype(h_ref.dtype)


def _resadd_kernel(x_ref, f_ref, g_ref, xo_ref):
    xo_ref[...] = x_ref[...] + g_ref[...] * f_ref[...]


def _row_spec(d):
    return pl.BlockSpec((ROWS_EW, d), lambda i: (i, 0))


def _vec_spec(d, rows_per_vec):
    return pl.BlockSpec((None, 1, d), lambda i: ((i * ROWS_EW) // rows_per_vec, 0, 0))


def _modulate(x, shift, scale, rows_per_vec):
    m, d = x.shape
    return pl.pallas_call(
        _modulate_kernel, grid=(m // ROWS_EW,),
        in_specs=[_row_spec(d), _vec_spec(d, rows_per_vec), _vec_spec(d, rows_per_vec)],
        out_specs=_row_spec(d),
        out_shape=jax.ShapeDtypeStruct((m, d), BF16),
        compiler_params=_cparams("parallel"), name="modulate",
    )(x, shift, scale)


def _resmod(x, f, gate, shift, scale, rows_per_vec):
    m, d = x.shape
    vs = _vec_spec(d, rows_per_vec)
    return pl.pallas_call(
        _resmod_kernel, grid=(m // ROWS_EW,),
        in_specs=[_row_spec(d), _row_spec(d), vs, vs, vs],
        out_specs=[_row_spec(d), _row_spec(d)],
        out_shape=[jax.ShapeDtypeStruct((m, d), F32), jax.ShapeDtypeStruct((m, d), BF16)],
        compiler_params=_cparams("parallel"), name="resmod",
    )(x, f, gate, shift, scale)


def _resadd(x, f, gate, rows_per_vec):
    m, d = x.shape
    return pl.pallas_call(
        _resadd_kernel, grid=(m // ROWS_EW,),
        in_specs=[_row_spec(d), _row_spec(d), _vec_spec(d, rows_per_vec)],
        out_specs=_row_spec(d),
        out_shape=jax.ShapeDtypeStruct((m, d), F32),
        compiler_params=_cparams("parallel"), name="resadd",
    )(x, f, gate)


def _proj_kernel(x_ref, w_ref, o_ref):
    o_ref[...] = _dot(x_ref[...], w_ref[...]).astype(o_ref.dtype)


def _proj(x, w, tn=TN):
    m, k = x.shape
    n = w.shape[1]
    tm = min(TM, m)
    return pl.pallas_call(
        _proj_kernel, grid=(m // tm, n // tn),
        in_specs=[pl.BlockSpec((tm, k), lambda i, j: (i, 0)),
                  pl.BlockSpec((k, tn), lambda i, j: (0, j))],
        out_specs=pl.BlockSpec((tm, tn), lambda i, j: (i, j)),
        out_shape=jax.ShapeDtypeStruct((m, n), BF16),
        compiler_params=_cparams("parallel", "parallel"), name="proj",
    )(x, w)


def _qk_proj_kernel(x_ref, w_ref, gain_ref, cos_ref, sin_ref, o_ref, *, rope):
    acc = _dot(x_ref[...], w_ref[...])
    for hh in range(acc.shape[1] // HEAD_DIM):
        sl = slice(hh * HEAD_DIM, (hh + 1) * HEAD_DIM)
        y = _rms(acc[:, sl]) * gain_ref[:, sl]
        if rope:
            y = y * cos_ref[...] + pltpu.roll(y, HEAD_DIM // 2, 1) * sin_ref[...]
        o_ref[:, sl] = y.astype(o_ref.dtype)


def _qk_proj(x, w, gain, cos2, sin2, rope):
    m, k = x.shape
    n = w.shape[1]
    tm = min(TM, m)
    seq_blocks = cos2.shape[0] // tm if rope else 1
    tab = pl.BlockSpec((tm, HEAD_DIM), lambda i, j: (i % seq_blocks, 0))
    return pl.pallas_call(
        functools.partial(_qk_proj_kernel, rope=rope), grid=(m // tm, n // TN),
        in_specs=[pl.BlockSpec((tm, k), lambda i, j: (i, 0)),
                  pl.BlockSpec((k, TN), lambda i, j: (0, j)),
                  pl.BlockSpec((1, TN), lambda i, j: (0, j)),
                  tab, tab],
        out_specs=pl.BlockSpec((tm, TN), lambda i, j: (i, j)),
        out_shape=jax.ShapeDtypeStruct((m, n), BF16),
        compiler_params=_cparams("parallel", "parallel"), name="qk_proj",
    )(x, w, gain, cos2, sin2)


def _out_res_kernel(*refs, n_in):
    xs, ws = refs[:n_in], refs[n_in:2 * n_in]
    res_ref, g_ref, o_ref = refs[2 * n_in:]
    acc = _dot(xs[0][...], ws[0][...])
    for x_ref, w_ref in zip(xs[1:], ws[1:]):
        acc = acc + _dot(x_ref[...], w_ref[...])
    o_ref[...] = res_ref[...] + g_ref[...] * acc


def _out_res(xs, ws, res, gate, rows_per_vec):
    m, n = res.shape
    tm = min(TM, m)
    n_in = len(xs)
    in_specs = ([pl.BlockSpec((tm, x.shape[1]), lambda i, j: (i, 0)) for x in xs]
                + [pl.BlockSpec((w.shape[0], TN), lambda i, j: (0, j)) for w in ws]
                + [pl.BlockSpec((tm, TN), lambda i, j: (i, j)),
                   pl.BlockSpec((None, 1, TN), lambda i, j: ((i * tm) // rows_per_vec, 0, j))])
    return pl.pallas_call(
        functools.partial(_out_res_kernel, n_in=n_in), grid=(m // tm, n // TN),
        in_specs=in_specs,
        out_specs=pl.BlockSpec((tm, TN), lambda i, j: (i, j)),
        out_shape=jax.ShapeDtypeStruct((m, n), F32),
        compiler_params=_cparams("parallel", "parallel"), name="out_res",
    )(*xs, *ws, res, gate)


def _ffn_kernel(h_ref, wg_ref, wu_ref, wd_ref, o_ref):
    j = pl.program_id(1)
    h = h_ref[...]
    g = _dot(h, wg_ref[...])
    u = _dot(h, wu_ref[...])
    a = (_silu(g) * u).astype(BF16)

    @pl.when(j == 0)
    def _():
        o_ref[...] = _dot(a, wd_ref[...])

    @pl.when(j > 0)
    def _():
        o_ref[...] += _dot(a, wd_ref[...])


def _ffn(h, wg, wu, wd):
    m, d = h.shape
    hidden = wg.shape[1]
    tm = min(TM, m)
    return pl.pallas_call(
        _ffn_kernel, grid=(m // tm, hidden // FFN_TH),
        in_specs=[pl.BlockSpec((tm, d), lambda i, j: (i, 0), pipeline_mode=pl.Buffered(1)),
                  pl.BlockSpec((d, FFN_TH), lambda i, j: (0, j)),
                  pl.BlockSpec((d, FFN_TH), lambda i, j: (0, j)),
                  pl.BlockSpec((FFN_TH, d), lambda i, j: (j, 0))],
        out_specs=pl.BlockSpec((tm, d), lambda i, j: (i, 0), pipeline_mode=pl.Buffered(1)),
        out_shape=jax.ShapeDtypeStruct((m, d), F32),
        compiler_params=_cparams("parallel", "arbitrary"), name="ffn",
    )(h, wg, wu, wd)


def _gla_step(q_ref, k_ref, v_ref, g_ref, off, w2_ref, b2_ref, tri, mask, s_ref, last_row, dk):
    rows = pl.ds(off, GLA_BLOCK)
    q = q_ref[rows, :].astype(F32) * (dk ** -0.5)
    k = k_ref[rows, :].astype(F32)
    v = v_ref[rows, :]
    x = _dot(g_ref[rows, :], w2_ref[...]) + b2_ref[...]
    log_a = (jnp.minimum(x, 0.0) - jnp.log(1.0 + jnp.exp(-jnp.abs(x)))) * (1.0 / GLA_TAU)
    hi = log_a.astype(BF16)
    lo = (log_a - hi.astype(F32)).astype(BF16)
    b = _dot(tri, hi) + _dot(tri, lo)
    mid = b[GLA_BLOCK // 2:GLA_BLOCK // 2 + 1, :]
    b_last = b[last_row:last_row + 1, :]
    qe = q * jnp.exp(b - mid)
    ke = k * jnp.exp(mid - b)
    q_in = (qe * jnp.exp(mid)).astype(BF16)
    k_out = (ke * jnp.exp(b_last - mid)).astype(BF16)
    att = _dot_nt(qe.astype(BF16), ke.astype(BF16))
    att = jnp.where(mask, att, 0.0).astype(BF16)
    s = s_ref[...]
    o = _dot_nt(q_in, s.astype(BF16)) + _dot(att, v)
    s_ref[...] = s * jnp.exp(b_last) + _dot_tn(v, k_out)
    return o


def _gla_finish(total, r_ref, off, gain):
    r = r_ref[pl.ds(off, GLA_BLOCK), :].astype(F32)
    return (_rms(total) * gain * _silu(r)).astype(BF16)


def _gla_kernel(qc_ref, kc_ref, vc_ref, rc_ref, gc_ref, ql_ref, kl_ref, vl_ref, rl_ref, gl_ref,
                w2f_ref, b2f_ref, w2b_ref, b2b_ref, gain_ref, oc_ref, ol_ref,
                sf_ref, sb_ref, acc_c_ref, acc_l_ref):
    dk = ql_ref.shape[1]
    sf_ref[...] = jnp.zeros_like(sf_ref)
    sb_ref[...] = jnp.zeros_like(sb_ref)
    row = lax.broadcasted_iota(jnp.int32, (GLA_BLOCK, GLA_BLOCK), 0)
    col = lax.broadcasted_iota(jnp.int32, (GLA_BLOCK, GLA_BLOCK), 1)
    tri_f = (col <= row).astype(BF16)
    tri_b = (col >= row).astype(BF16)
    mask_f = col <= row
    mask_b = col > row
    gain = gain_ref[...]

    def scan(q_ref, k_ref, v_ref, r_ref, g_ref, acc_ref, o_ref):
        n = q_ref.shape[0] // GLA_BLOCK
        fwd = functools.partial(_gla_step, q_ref, k_ref, v_ref, g_ref, w2_ref=w2f_ref, b2_ref=b2f_ref,
                                tri=tri_f, mask=mask_f, s_ref=sf_ref, last_row=GLA_BLOCK - 1, dk=dk)
        bwd = functools.partial(_gla_step, q_ref, k_ref, v_ref, g_ref, w2_ref=w2b_ref, b2_ref=b2b_ref,
                                tri=tri_b, mask=mask_b, s_ref=sb_ref, last_row=0, dk=dk)

        def offsets(it):
            return (pl.multiple_of(it * GLA_BLOCK, GLA_BLOCK),
                    pl.multiple_of((n - 1 - it) * GLA_BLOCK, GLA_BLOCK))

        def first_visit(it, carry):
            off_f, off_b = offsets(it)
            acc_ref[pl.ds(off_f, GLA_BLOCK), :] = fwd(off=off_f)
            acc_ref[pl.ds(off_b, GLA_BLOCK), :] = bwd(off=off_b)
            return carry

        def second_visit(it, carry):
            off_f, off_b = offsets(it)
            tot_f = acc_ref[pl.ds(off_f, GLA_BLOCK), :] + fwd(off=off_f)
            o_ref[pl.ds(off_f, GLA_BLOCK), :] = _gla_finish(tot_f, r_ref, off_f, gain)
            tot_b = acc_ref[pl.ds(off_b, GLA_BLOCK), :] + bwd(off=off_b)
            o_ref[pl.ds(off_b, GLA_BLOCK), :] = _gla_finish(tot_b, r_ref, off_b, gain)
            return carry

        lax.fori_loop(0, n // 2, first_visit, 0)
        lax.fori_loop(n // 2, n, second_visit, 0)

    scan(qc_ref, kc_ref, vc_ref, rc_ref, gc_ref, acc_c_ref, oc_ref)
    scan(ql_ref, kl_ref, vl_ref, rl_ref, gl_ref, acc_l_ref, ol_ref)


def _gla(p_l, p_c, g_l, g_c, w2f, b2f, w2b, b2b, gain, batch, dk, dv):
    seq, ctx_len = p_l.shape[0] // batch, p_c.shape[0] // batch
    assert seq % (2 * GLA_BLOCK) == 0 and ctx_len % (2 * GLA_BLOCK) == 0
    k_blk0 = GLA_HEADS
    v_blk0 = 2 * GLA_HEADS * dk // dv
    r_blk0 = v_blk0 + GLA_HEADS

    def tok_specs(rows):
        return [pl.BlockSpec((rows, dk), lambda b, h: (b, h)),
                pl.BlockSpec((rows, dk), lambda b, h: (b, k_blk0 + h)),
                pl.BlockSpec((rows, dv), lambda b, h: (b, v_blk0 + h)),
                pl.BlockSpec((rows, dv), lambda b, h: (b, r_blk0 + h)),
                pl.BlockSpec((rows, LANE), lambda b, h: (b, 0))]

    w2_spec = pl.BlockSpec((LANE, dk), lambda b, h: (0, h))
    b2_spec = pl.BlockSpec((1, dk), lambda b, h: (0, h))
    return pl.pallas_call(
        _gla_kernel, grid=(batch, GLA_HEADS),
        in_specs=tok_specs(ctx_len) + tok_specs(seq)
        + [w2_spec, b2_spec, w2_spec, b2_spec, pl.BlockSpec((1, dv), lambda b, h: (0, 0))],
        out_specs=[pl.BlockSpec((ctx_len, dv), lambda b, h: (b, h)),
                   pl.BlockSpec((seq, dv), lambda b, h: (b, h))],
        out_shape=[jax.ShapeDtypeStruct((batch * ctx_len, GLA_HEADS * dv), BF16),
                   jax.ShapeDtypeStruct((batch * seq, GLA_HEADS * dv), BF16)],
        scratch_shapes=[pltpu.VMEM((dv, dk), F32), pltpu.VMEM((dv, dk), F32),
                        pltpu.VMEM((ctx_len, dv), F32), pltpu.VMEM((seq, dv), F32)],
        compiler_params=_cparams("parallel", "parallel"), name="gla",
    )(p_c, p_c, p_c, p_c, g_c, p_l, p_l, p_l, p_l, g_l, w2f, b2f, w2b, b2b, gain)


def _sgu_kernel(u_ref, s_ref, ws_ref, bs_ref, gain_ref, o_ref):
    gw = u_ref.shape[1] // SGU_GROUPS
    for ci in range(u_ref.shape[0] // SGU_CHUNK):
        rows = slice(ci * SGU_CHUNK, (ci + 1) * SGU_CHUNK)
        for g in range(SGU_GROUPS):
            cols = slice(g * gw, (g + 1) * gw)
            vg = _rms(_gelu_tanh(s_ref[rows, cols].astype(F32))) * gain_ref[:, cols]
            mixed = _dot(ws_ref[g], vg.astype(BF16)) + bs_ref[:, g:g + 1]
            o_ref[rows, cols] = (_gelu_tanh(u_ref[rows, cols].astype(F32)) * mixed).astype(o_ref.dtype)


def _sgu(p, w_s, b_s_t, gain, width, u_blk, s_blk):
    m = p.shape[0]
    rows = 2 * SGU_CHUNK
    return pl.pallas_call(
        _sgu_kernel, grid=(m // rows,),
        in_specs=[pl.BlockSpec((rows, width), lambda i: (i, u_blk)),
                  pl.BlockSpec((rows, width), lambda i: (i, s_blk)),
                  pl.BlockSpec(w_s.shape, lambda i: (0, 0, 0)),
                  pl.BlockSpec(b_s_t.shape, lambda i: (0, 0)),
                  pl.BlockSpec((1, width), lambda i: (0, 0))],
        out_specs=pl.BlockSpec((rows, width), lambda i: (i, 0)),
        out_shape=jax.ShapeDtypeStruct((m, width), BF16),
        compiler_params=_cparams("parallel"), name="sgu",
    )(p, p, w_s, b_s_t, gain)


def _attn_kernel(sink_ref, q_ref, k_ref, v_ref, kc_ref, vc_ref, o_ref):
    kv, n = pl.program_id(1), pl.program_id(2)
    seq = k_ref.shape[0]
    band = 3 * ATT_BLOCK
    start = pl.multiple_of(jnp.clip((n - 1) * ATT_BLOCK, 0, seq - band), ATT_BLOCK)
    q = q_ref[...]
    qs = jnp.concatenate([q[:, g * HEAD_DIM:(g + 1) * HEAD_DIM] for g in range(ATT_GROUP)], axis=0)
    k_loc = k_ref[pl.ds(start, band), :]
    v_loc = v_ref[pl.ds(start, band), :]
    s_loc = _dot_nt(qs, k_loc)
    s_ctx = _dot_nt(qs, kc_ref[...])
    rows = ATT_GROUP * ATT_BLOCK
    qpos = lax.broadcasted_iota(jnp.int32, (rows, band), 0) & (ATT_BLOCK - 1)
    kpos = lax.broadcasted_iota(jnp.int32, (rows, band), 1) + (start - n * ATT_BLOCK)
    s_loc = jnp.where(jnp.abs(kpos - qpos) <= WINDOW, s_loc, -jnp.inf)
    sink = jnp.concatenate([jnp.full((ATT_BLOCK, 1), sink_ref[kv * ATT_GROUP + g], F32)
                            for g in range(ATT_GROUP)], axis=0)
    m = jnp.maximum(sink, jnp.maximum(jnp.max(s_loc, axis=-1, keepdims=True),
                                      jnp.max(s_ctx, axis=-1, keepdims=True)))
    p_loc = jnp.exp(s_loc - m)
    p_ctx = jnp.exp(s_ctx - m)
    den = jnp.exp(sink - m) + jnp.sum(p_loc, axis=-1, keepdims=True) + jnp.sum(p_ctx, axis=-1, keepdims=True)
    o = (_dot(p_loc.astype(BF16), v_loc) + _dot(p_ctx.astype(BF16), vc_ref[...])) / den
    o_ref[...] = jnp.concatenate([o[g * ATT_BLOCK:(g + 1) * ATT_BLOCK] for g in range(ATT_GROUP)],
                                 axis=1).astype(o_ref.dtype)


def _attention(qk, v, k_c, v_c, sink, batch, n_q_heads):
    m = qk.shape[0]
    seq, ctx_len = m // batch, k_c.shape[0] // batch
    kv_heads = n_q_heads // ATT_GROUP
    nb = seq // ATT_BLOCK
    qw = ATT_GROUP * HEAD_DIM
    return pl.pallas_call(
        _attn_kernel, grid=(batch, kv_heads, nb),
        in_specs=[pl.BlockSpec(memory_space=pltpu.SMEM),
                  pl.BlockSpec((ATT_BLOCK, qw), lambda b, h, n: (b * nb + n, h)),
                  pl.BlockSpec((seq, HEAD_DIM), lambda b, h, n: (b, n_q_heads + h)),
                  pl.BlockSpec((seq, HEAD_DIM), lambda b, h, n: (b, h)),
                  pl.BlockSpec((ctx_len, HEAD_DIM), lambda b, h, n: (b, h)),
                  pl.BlockSpec((ctx_len, HEAD_DIM), lambda b, h, n: (b, h))],
        out_specs=pl.BlockSpec((ATT_BLOCK, qw), lambda b, h, n: (b * nb + n, h)),
        out_shape=jax.ShapeDtypeStruct((m, n_q_heads * HEAD_DIM), BF16),
        compiler_params=_cparams("parallel", "parallel", "arbitrary"), name="attention",
    )(sink, qk, qk, v, k_c, v_c)


def _rope_tables(seq):
    rows = seq // GRID_W
    row = jnp.repeat(jnp.arange(rows, dtype=F32), GRID_W)
    col = jnp.tile(jnp.arange(GRID_W, dtype=F32), rows)
    n_freq = HEAD_DIM // 4
    inv_freq = ROPE_THETA ** (-jnp.arange(n_freq, dtype=F32) / n_freq)
    ang = jnp.concatenate([row[:, None] * inv_freq, col[:, None] * inv_freq], axis=-1)
    cos, sin = jnp.cos(ang), jnp.sin(ang)
    return jnp.concatenate([cos, cos], axis=-1), jnp.concatenate([-sin, sin], axis=-1)


def kernel(x, c, ctx, c_ctx, ada_w, ada_b, ffn_w_gate, ffn_w_up, ffn_w_down, even_w_in, even_gate_w2_fwd, even_gate_b_fwd, even_gate_w2_bwd, even_gate_b_bwd, even_gla_norm_gain, even_sgu_norm_gain, even_sgu_w_s, even_sgu_b_s, even_w_out, odd_w_in, odd_q_norm_gain, odd_k_norm_gain, odd_sink, odd_w_out):
    batch, seq, d = x.shape
    ctx_len = ctx.shape[1]
    depth = ada_w.shape[0]
    assert depth == 2 and batch + 1 <= 8
    xl = x.reshape(batch * seq, d)
    xc = ctx.reshape(batch * ctx_len, d)

    cc = jnp.concatenate([c, c_ctx[None], jnp.zeros((8 - batch - 1, d), F32)], axis=0)
    mods = _ada(cc, ada_w, ada_b)

    def mod_l(layer, which):
        return mods[layer, :batch, which * d:(which + 1) * d].reshape(batch, 1, d)

    def mod_c(layer, which):
        return mods[layer, batch:batch + 1, which * d:(which + 1) * d].reshape(1, 1, d)

    all_ctx = batch * ctx_len

    dv = even_gla_norm_gain.shape[1]
    key_w = even_gate_w2_fwd.shape[2]
    dk = key_w // GLA_HEADS
    val_w = GLA_HEADS * dv
    sgu_w = even_sgu_norm_gain.shape[1]
    w_in = even_w_in[0]
    gate0 = 2 * key_w + 2 * val_w
    gate1 = gate0 + 2 * GLA_GATE_RANK
    w_main = jnp.concatenate([w_in[:, :gate0], w_in[:, gate1:]], axis=1).astype(BF16)
    w_gate = jnp.pad(w_in[:, gate0:gate1], ((0, 0), (0, LANE - 2 * GLA_GATE_RANK))).astype(BF16)
    pad_rows = LANE - GLA_GATE_RANK
    w2f = jnp.pad(even_gate_w2_fwd[0], ((0, pad_rows), (0, 0))).astype(BF16)
    w2b = jnp.pad(even_gate_w2_bwd[0], ((GLA_GATE_RANK, pad_rows - GLA_GATE_RANK), (0, 0))).astype(BF16)
    b2f = even_gate_b_fwd[0].reshape(1, key_w)
    b2b = even_gate_b_bwd[0].reshape(1, key_w)

    h_l = _modulate(xl, mod_l(0, 0), mod_l(0, 1), seq)
    h_c = _modulate(xc, mod_c(0, 0), mod_c(0, 1), all_ctx)
    p_l, p_c = _proj(h_l, w_main), _proj(h_c, w_main)
    g_l, g_c = _proj(h_l, w_gate, tn=LANE), _proj(h_c, w_gate, tn=LANE)
    gla_c, gla_l = _gla(p_l, p_c, g_l, g_c, w2f, b2f, w2b, b2b, even_gla_norm_gain[0].reshape(1, dv),
                        batch, dk, dv)
    w_s = even_sgu_w_s[0].astype(BF16)
    b_s_t = even_sgu_b_s[0].T
    sgu_gain = even_sgu_norm_gain[0].reshape(1, sgu_w)
    u_blk = gate0 // sgu_w
    sgu_l = _sgu(p_l, w_s, b_s_t, sgu_gain, sgu_w, u_blk, u_blk + 1)
    sgu_c = _sgu(p_c, w_s, b_s_t, sgu_gain, sgu_w, u_blk, u_blk + 1)
    w_out = even_w_out[0].astype(BF16)
    w_out_gla, w_out_sgu = w_out[:val_w], w_out[val_w:]
    x_l = _out_res([gla_l, sgu_l], [w_out_gla, w_out_sgu], xl, mod_l(0, 2), seq)
    x_c = _out_res([gla_c, sgu_c], [w_out_gla, w_out_sgu], xc, mod_c(0, 2), all_ctx)

    wg, wu, wd = ffn_w_gate[0].astype(BF16), ffn_w_up[0].astype(BF16), ffn_w_down[0].astype(BF16)
    f_l = _ffn(_modulate(x_l, mod_l(0, 3), mod_l(0, 4), seq), wg, wu, wd)
    f_c = _ffn(_modulate(x_c, mod_c(0, 3), mod_c(0, 4), all_ctx), wg, wu, wd)

    x_l, h_l = _resmod(x_l, f_l, mod_l(0, 5), mod_l(1, 0), mod_l(1, 1), seq)
    _, h_c = _resmod(x_c, f_c, mod_c(0, 5), mod_c(1, 0), mod_c(1, 1), all_ctx)

    n_heads = odd_sink.shape[1]
    q_w = n_heads * HEAD_DIM
    kv_w = q_w // ATT_GROUP
    w_in = odd_w_in[0]
    w_qk = w_in[:, :q_w + kv_w].astype(BF16)
    w_k = w_in[:, q_w:q_w + kv_w].astype(BF16)
    w_v = w_in[:, q_w + kv_w:].astype(BF16)
    k_gain = jnp.tile(odd_k_norm_gain[0], kv_w // HEAD_DIM)
    qk_gain = jnp.concatenate([jnp.tile(odd_q_norm_gain[0] * HEAD_DIM ** -0.5, n_heads), k_gain]).reshape(1, -1)
    cos2, sin2 = _rope_tables(seq)
    qk_l = _qk_proj(h_l, w_qk, qk_gain, cos2, sin2, rope=True)
    v_l = _proj(h_l, w_v)
    k_c = _qk_proj(h_c, w_k, k_gain.reshape(1, -1), cos2, sin2, rope=False)
    v_c = _proj(h_c, w_v)
    att = _attention(qk_l, v_l, k_c, v_c, odd_sink[0], batch, n_heads)
    x_l = _out_res([att], [odd_w_out[0].astype(BF16)], x_l, mod_l(1, 2), seq)

    wg, wu, wd = ffn_w_gate[1].astype(BF16), ffn_w_up[1].astype(BF16), ffn_w_down[1].astype(BF16)
    f_l = _ffn(_modulate(x_l, mod_l(1, 3), mod_l(1, 4), seq), wg, wu, wd)
    return _resadd(x_l, f_l, mod_l(1, 5), seq).reshape(batch, seq, d)
```

```python
import functools

import numpy as np
import jax
import jax.numpy as jnp
from jax import lax
from jax.experimental import pallas as pl
from jax.experimental.pallas import tpu as pltpu

F32 = jnp.float32
BF16 = jnp.bfloat16

EPS = 1e-6
N_MOD = 6
GRID_W = 64
ROPE_THETA = 10000.0

GLA_HEADS = 4
GLA_TAU = 16.0
GLA_GATE_RANK = 16
GLA_BLOCK = 64
SGU_GROUPS = 8
SGU_CHUNK = 128
HEAD_DIM = 128
ATT_GROUP = 4
WINDOW = 128
ATT_BLOCK = 128
ATT_QB = 4

LANE = 128
MXU_W = 256
VMEM_LIMIT = 56 * 1024 * 1024
FFN_VMEM_LIMIT = 60 * 1024 * 1024

TM = 1024
TN = 512
FFN_TH = 256
ROWS_EW = 256


def _cparams(*sem):
    return pltpu.CompilerParams(dimension_semantics=sem, vmem_limit_bytes=VMEM_LIMIT)


def _dot(a, b):
    return jnp.dot(a, b, preferred_element_type=F32)


def _dot_nt(a, b):
    return lax.dot_general(a, b, (((1,), (1,)), ((), ())), preferred_element_type=F32)


def _dot_tn(a, b):
    return lax.dot_general(a, b, (((0,), (0,)), ((), ())), preferred_element_type=F32)


def _silu(x):
    return x * jax.nn.sigmoid(x)


def _gelu_tanh(x):
    c = np.float32(np.sqrt(2.0 / np.pi))
    return x * (0.5 * (1.0 + jnp.tanh(c * (x + 0.044715 * (x * x * x)))))


def _rms(x):
    return x * lax.rsqrt(jnp.mean(x * x, axis=-1, keepdims=True) + EPS)


def _ada_kernel(c_ref, w_ref, b_ref, o_ref):
    a = _silu(c_ref[...]).astype(BF16)
    o_ref[...] = _dot(a, w_ref[...].astype(BF16)) + b_ref[...]


def _ada(cc, ada_w, ada_b):
    depth, d, n = ada_w.shape
    rows = cc.shape[0]
    return pl.pallas_call(
        _ada_kernel,
        grid=(depth, n // TN),
        in_specs=[pl.BlockSpec((rows, d), lambda l, j: (0, 0)),
                  pl.BlockSpec((None, d, TN), lambda l, j: (l, 0, j)),
                  pl.BlockSpec((None, 1, TN), lambda l, j: (l, 0, j))],
        out_specs=pl.BlockSpec((None, rows, TN), lambda l, j: (l, 0, j)),
        out_shape=jax.ShapeDtypeStruct((depth, rows, n), F32),
        compiler_params=_cparams("parallel", "parallel"),
        name="ada",
    )(cc, ada_w, ada_b.reshape(depth, 1, n))


def _modulate_kernel(x_ref, sh_ref, sc_ref, h_ref):
    h_ref[...] = (_rms(x_ref[...]) * (1.0 + sc_ref[...]) + sh_ref[...]).astype(h_ref.dtype)


def _row_spec(d):
    return pl.BlockSpec((ROWS_EW, d), lambda i: (i, 0))


def _vec_spec(d, rows_per_vec):
    return pl.BlockSpec((None, 1, d), lambda i: ((i * ROWS_EW) // rows_per_vec, 0, 0))


def _modulate(x, shift, scale, rows_per_vec):
    m, d = x.shape
    return pl.pallas_call(
        _modulate_kernel, grid=(m // ROWS_EW,),
        in_specs=[_row_spec(d), _vec_spec(d, rows_per_vec), _vec_spec(d, rows_per_vec)],
        out_specs=_row_spec(d),
        out_shape=jax.ShapeDtypeStruct((m, d), BF16),
        compiler_params=_cparams("parallel"), name="modulate",
    )(x, shift, scale)


def _proj_kernel(x_ref, w_ref, o_ref):
    o_ref[...] = _dot(x_ref[...], w_ref[...]).astype(o_ref.dtype)


def _proj(x, w, tn=TN):
    m, k = x.shape
    n = w.shape[1]
    tm = min(TM, m)
    return pl.pallas_call(
        _proj_kernel, grid=(m // tm, n // tn),
        in_specs=[pl.BlockSpec((tm, k), lambda i, j: (i, 0)),
                  pl.BlockSpec((k, tn), lambda i, j: (0, j))],
        out_specs=pl.BlockSpec((tm, tn), lambda i, j: (i, j)),
        out_shape=jax.ShapeDtypeStruct((m, n), BF16),
        compiler_params=_cparams("parallel", "parallel"), name="proj",
    )(x, w)


def _qk_proj_kernel(x_ref, w_ref, gain_ref, cos_ref, sin_ref, ones_ref, o_ref, *, rope):
    half = x_ref.shape[0] // 2
    for rr in range(2):
        rows = slice(rr * half, (rr + 1) * half)
        x = x_ref[rows, :]
        for cc in range(w_ref.shape[1] // TN):
            acc = _dot(x, w_ref[:, cc * TN:(cc + 1) * TN])
            for tt in range(TN // MXU_W):
                c0 = cc * TN + tt * MXU_W
                a = acc[:, tt * MXU_W:(tt + 1) * MXU_W]
                ssq = _dot((a * a).astype(BF16), ones_ref[...])
                y = a * lax.rsqrt(ssq * (1.0 / HEAD_DIM) + EPS) * gain_ref[:, c0:c0 + MXU_W]
                for hh in range(MXU_W // HEAD_DIM):
                    yh = y[:, hh * HEAD_DIM:(hh + 1) * HEAD_DIM]
                    if rope:
                        yh = (yh * cos_ref[rows, :]
                              + pltpu.roll(yh, HEAD_DIM // 2, 1) * sin_ref[rows, :])
                    o_ref[rows, c0 + hh * HEAD_DIM:c0 + (hh + 1) * HEAD_DIM] = yh.astype(o_ref.dtype)


def _qk_proj(x, w, gain, cos2, sin2, rope):
    m, k = x.shape
    n = w.shape[1]
    tm = min(TM, m)
    tn = 2 * TN
    seq_blocks = cos2.shape[0] // tm if rope else 1
    tab = pl.BlockSpec((tm, HEAD_DIM), lambda i, j: (i % seq_blocks, 0))
    head_of = np.arange(MXU_W) // HEAD_DIM
    ones = jnp.asarray(head_of[:, None] == head_of[None, :], BF16)
    return pl.pallas_call(
        functools.partial(_qk_proj_kernel, rope=rope), grid=(m // tm, n // tn),
        in_specs=[pl.BlockSpec((tm, k), lambda i, j: (i, 0)),
                  pl.BlockSpec((k, tn), lambda i, j: (0, j)),
                  pl.BlockSpec((1, tn), lambda i, j: (0, j)),
                  tab, tab,
                  pl.BlockSpec((MXU_W, MXU_W), lambda i, j: (0, 0))],
        out_specs=pl.BlockSpec((tm, tn), lambda i, j: (i, j)),
        out_shape=jax.ShapeDtypeStruct((m, n), BF16),
        compiler_params=_cparams("parallel", "parallel"), name="qk_proj",
    )(x, w, gain, cos2, sin2, ones)


def _out_res_kernel(*refs, n_in):
    xs, ws = refs[:n_in], refs[n_in:2 * n_in]
    res_ref, g_ref, o_ref = refs[2 * n_in:]
    acc = _dot(xs[0][...], ws[0][...])
    for x_ref, w_ref in zip(xs[1:], ws[1:]):
        acc = acc + _dot(x_ref[...], w_ref[...])
    o_ref[...] = res_ref[...] + g_ref[...] * acc


def _out_res(xs, ws, res, gate, rows_per_vec):
    m, n = res.shape
    tm = min(TM, m)
    n_in = len(xs)
    in_specs = ([pl.BlockSpec((tm, x.shape[1]), lambda i, j: (i, 0)) for x in xs]
                + [pl.BlockSpec((w.shape[0], TN), lambda i, j: (0, j)) for w in ws]
                + [pl.BlockSpec((tm, TN), lambda i, j: (i, j)),
                   pl.BlockSpec((None, 1, TN), lambda i, j: ((i * tm) // rows_per_vec, 0, j))])
    return pl.pallas_call(
        functools.partial(_out_res_kernel, n_in=n_in), grid=(m // tm, n // TN),
        in_specs=in_specs,
        out_specs=pl.BlockSpec((tm, TN), lambda i, j: (i, j)),
        out_shape=jax.ShapeDtypeStruct((m, n), F32),
        compiler_params=_cparams("parallel", "parallel"), name="out_res",
    )(*xs, *ws, res, gate)


def _ffn_kernel(x_ref, sh_ref, sc_ref, gate_ref, wg_ref, wu_ref, wd_ref, o_ref, h_ref):
    j = pl.program_id(1)
    n_chunks = x_ref.shape[0] // ROWS_EW

    def chunk_rows(ci):
        return pl.ds(pl.multiple_of(ci * ROWS_EW, ROWS_EW), ROWS_EW)

    @pl.when(j == 0)
    def _():
        def body(ci, carry):
            rows = chunk_rows(ci)
            h_ref[rows, :] = (_rms(x_ref[rows, :]) * (1.0 + sc_ref[...]) + sh_ref[...]).astype(BF16)
            return carry
        lax.fori_loop(0, n_chunks, body, 0)

    h = h_ref[...]
    g = _dot(h, wg_ref[...])
    u = _dot(h, wu_ref[...])
    a = (_silu(g) * u).astype(BF16)

    @pl.when(j == 0)
    def _():
        o_ref[...] = _dot(a, wd_ref[...])

    @pl.when(j > 0)
    def _():
        o_ref[...] += _dot(a, wd_ref[...])

    @pl.when(j == pl.num_programs(1) - 1)
    def _():
        def body(ci, carry):
            rows = chunk_rows(ci)
            o_ref[rows, :] = x_ref[rows, :] + gate_ref[...] * o_ref[rows, :]
            return carry
        lax.fori_loop(0, n_chunks, body, 0)


def _ffn(x, shift, scale, gate, wg, wu, wd, rows_per_vec):
    m, d = x.shape
    hidden = wg.shape[1]
    tm = min(TM, m)
    vec = pl.BlockSpec((None, 1, d), lambda i, j: ((i * tm) // rows_per_vec, 0, 0))
    return pl.pallas_call(
        _ffn_kernel, grid=(m // tm, hidden // FFN_TH),
        in_specs=[pl.BlockSpec((tm, d), lambda i, j: (i, 0), pipeline_mode=pl.Buffered(1)),
                  vec, vec, vec,
                  pl.BlockSpec((d, FFN_TH), lambda i, j: (0, j)),
                  pl.BlockSpec((d, FFN_TH), lambda i, j: (0, j)),
                  pl.BlockSpec((FFN_TH, d), lambda i, j: (j, 0))],
        out_specs=pl.BlockSpec((tm, d), lambda i, j: (i, 0), pipeline_mode=pl.Buffered(1)),
        out_shape=jax.ShapeDtypeStruct((m, d), F32),
        scratch_shapes=[pltpu.VMEM((tm, d), BF16)],
        compiler_params=pltpu.CompilerParams(dimension_semantics=("parallel", "arbitrary"),
                                             vmem_limit_bytes=FFN_VMEM_LIMIT), name="ffn",
    )(x, shift, scale, gate, wg, wu, wd)


def _gla_step(q_ref, k_ref, v_ref, g_ref, off, w2_ref, b2_ref, tri, mask, s_ref, last_row, dk):
    rows = pl.ds(off, GLA_BLOCK)
    q = q_ref[rows, :].astype(F32) * (dk ** -0.5)
    k = k_ref[rows, :].astype(F32)
    v = v_ref[rows, :]
    x = _dot(g_ref[rows, :], w2_ref[...]) + b2_ref[...]
    log_a = (jnp.minimum(x, 0.0) - jnp.log(1.0 + jnp.exp(-jnp.abs(x)))) * (1.0 / GLA_TAU)
    hi = log_a.astype(BF16)
    lo = (log_a - hi.astype(F32)).astype(BF16)
    b = _dot(tri, hi) + _dot(tri, lo)
    mid = b[GLA_BLOCK // 2:GLA_BLOCK // 2 + 1, :]
    b_last = b[last_row:last_row + 1, :]
    qe = q * jnp.exp(b - mid)
    ke = k * jnp.exp(mid - b)
    q_in = (qe * jnp.exp(mid)).astype(BF16)
    k_out = (ke * jnp.exp(b_last - mid)).astype(BF16)
    att = _dot_nt(qe.astype(BF16), ke.astype(BF16))
    att = jnp.where(mask, att, 0.0).astype(BF16)
    s = s_ref[...]
    o = _dot_nt(q_in, s.astype(BF16)) + _dot(att, v)
    s_ref[...] = s * jnp.exp(b_last) + _dot_tn(v, k_out)
    return o


def _gla_finish(total, r_ref, off, gain):
    r = r_ref[pl.ds(off, GLA_BLOCK), :].astype(F32)
    return (_rms(total) * gain * _silu(r)).astype(BF16)


def _gla_kernel(qc_ref, kc_ref, vc_ref, rc_ref, gc_ref, ql_ref, kl_ref, vl_ref, rl_ref, gl_ref,
                w2f_ref, b2f_ref, w2b_ref, b2b_ref, gain_ref, oc_ref, ol_ref,
                sf_ref, sb_ref, acc_c_ref, acc_l_ref):
    dk = ql_ref.shape[1]
    sf_ref[...] = jnp.zeros_like(sf_ref)
    sb_ref[...] = jnp.zeros_like(sb_ref)
    row = lax.broadcasted_iota(jnp.int32, (GLA_BLOCK, GLA_BLOCK), 0)
    col = lax.broadcasted_iota(jnp.int32, (GLA_BLOCK, GLA_BLOCK), 1)
    tri_f = (col <= row).astype(BF16)
    tri_b = (col >= row).astype(BF16)
    mask_f = col <= row
    mask_b = col > row
    gain = gain_ref[...]

    def scan(q_ref, k_ref, v_ref, r_ref, g_ref, acc_ref, o_ref):
        n = q_ref.shape[0] // GLA_BLOCK
        fwd = functools.partial(_gla_step, q_ref, k_ref, v_ref, g_ref, w2_ref=w2f_ref, b2_ref=b2f_ref,
                                tri=tri_f, mask=mask_f, s_ref=sf_ref, last_row=GLA_BLOCK - 1, dk=dk)
        bwd = functools.partial(_gla_step, q_ref, k_ref, v_ref, g_ref, w2_ref=w2b_ref, b2_ref=b2b_ref,
                                tri=tri_b, mask=mask_b, s_ref=sb_ref, last_row=0, dk=dk)

        def offsets(it):
            return (pl.multiple_of(it * GLA_BLOCK, GLA_BLOCK),
                    pl.multiple_of((n - 1 - it) * GLA_BLOCK, GLA_BLOCK))

        def first_visit(it, carry):
            off_f, off_b = offsets(it)
            acc_ref[pl.ds(off_f, GLA_BLOCK), :] = fwd(off=off_f)
            acc_ref[pl.ds(off_b, GLA_BLOCK), :] = bwd(off=off_b)
            return carry

        def second_visit(it, carry):
            off_f, off_b = offsets(it)
            tot_f = acc_ref[pl.ds(off_f, GLA_BLOCK), :] + fwd(off=off_f)
            o_ref[pl.ds(off_f, GLA_BLOCK), :] = _gla_finish(tot_f, r_ref, off_f, gain)
            tot_b = acc_ref[pl.ds(off_b, GLA_BLOCK), :] + bwd(off=off_b)
            o_ref[pl.ds(off_b, GLA_BLOCK), :] = _gla_finish(tot_b, r_ref, off_b, gain)
            return carry

        lax.fori_loop(0, n // 2, first_visit, 0)
        lax.fori_loop(n // 2, n, second_visit, 0)

    scan(qc_ref, kc_ref, vc_ref, rc_ref, gc_ref, acc_c_ref, oc_ref)
    scan(ql_ref, kl_ref, vl_ref, rl_ref, gl_ref, acc_l_ref, ol_ref)


def _gla(p_l, p_c, g_l, g_c, w2f, b2f, w2b, b2b, gain, batch, dk, dv):
    seq, ctx_len = p_l.shape[0] // batch, p_c.shape[0] // batch
    assert seq % (2 * GLA_BLOCK) == 0 and ctx_len % (2 * GLA_BLOCK) == 0
    k_blk0 = GLA_HEADS
    v_blk0 = 2 * GLA_HEADS * dk // dv
    r_blk0 = v_blk0 + GLA_HEADS

    def tok_specs(rows):
        return [pl.BlockSpec((rows, dk), lambda b, h: (b, h)),
                pl.BlockSpec((rows, dk), lambda b, h: (b, k_blk0 + h)),
                pl.BlockSpec((rows, dv), lambda b, h: (b, v_blk0 + h)),
                pl.BlockSpec((rows, dv), lambda b, h: (b, r_blk0 + h)),
                pl.BlockSpec((rows, LANE), lambda b, h: (b, 0))]

    w2_spec = pl.BlockSpec((LANE, dk), lambda b, h: (0, h))
    b2_spec = pl.BlockSpec((1, dk), lambda b, h: (0, h))
    return pl.pallas_call(
        _gla_kernel, grid=(batch, GLA_HEADS),
        in_specs=tok_specs(ctx_len) + tok_specs(seq)
        + [w2_spec, b2_spec, w2_spec, b2_spec, pl.BlockSpec((1, dv), lambda b, h: (0, 0))],
        out_specs=[pl.BlockSpec((ctx_len, dv), lambda b, h: (b, h)),
                   pl.BlockSpec((seq, dv), lambda b, h: (b, h))],
        out_shape=[jax.ShapeDtypeStruct((batch * ctx_len, GLA_HEADS * dv), BF16),
                   jax.ShapeDtypeStruct((batch * seq, GLA_HEADS * dv), BF16)],
        scratch_shapes=[pltpu.VMEM((dv, dk), F32), pltpu.VMEM((dv, dk), F32),
                        pltpu.VMEM((ctx_len, dv), F32), pltpu.VMEM((seq, dv), F32)],
        compiler_params=_cparams("parallel", "parallel"), name="gla",
    )(p_c, p_c, p_c, p_c, g_c, p_l, p_l, p_l, p_l, g_l, w2f, b2f, w2b, b2b, gain)


def _sgu_kernel(u_ref, s_ref, ws_ref, bs_ref, gain_ref, o_ref):
    gw = u_ref.shape[1] // SGU_GROUPS
    for ci in range(u_ref.shape[0] // SGU_CHUNK):
        rows = slice(ci * SGU_CHUNK, (ci + 1) * SGU_CHUNK)
        for g in range(SGU_GROUPS):
            cols = slice(g * gw, (g + 1) * gw)
            vg = _rms(_gelu_tanh(s_ref[rows, cols].astype(F32))) * gain_ref[:, cols]
            mixed = _dot(ws_ref[g], vg.astype(BF16)) + bs_ref[:, g:g + 1]
            o_ref[rows, cols] = (_gelu_tanh(u_ref[rows, cols].astype(F32)) * mixed).astype(o_ref.dtype)


def _sgu(p, w_s, b_s_t, gain, width, u_blk, s_blk):
    m = p.shape[0]
    rows = 2 * SGU_CHUNK
    return pl.pallas_call(
        _sgu_kernel, grid=(m // rows,),
        in_specs=[pl.BlockSpec((rows, width), lambda i: (i, u_blk)),
                  pl.BlockSpec((rows, width), lambda i: (i, s_blk)),
                  pl.BlockSpec(w_s.shape, lambda i: (0, 0, 0)),
                  pl.BlockSpec(b_s_t.shape, lambda i: (0, 0)),
                  pl.BlockSpec((1, width), lambda i: (0, 0))],
        out_specs=pl.BlockSpec((rows, width), lambda i: (i, 0)),
        out_shape=jax.ShapeDtypeStruct((m, width), BF16),
        compiler_params=_cparams("parallel"), name="sgu",
    )(p, p, w_s, b_s_t, gain)


def _attn_kernel(sink_ref, bias_ref, q_ref, k_ref, v_ref, kc_ref, vc_ref, o_ref):
    kv, ng = pl.program_id(1), pl.program_id(2)
    seq = k_ref.shape[0]
    nb = seq // ATT_BLOCK
    band = 3 * ATT_BLOCK
    sink = jnp.concatenate([jnp.full((ATT_BLOCK, 1), sink_ref[kv * ATT_GROUP + g], F32)
                            for g in range(ATT_GROUP)], axis=0)
    k_ctx, v_ctx = kc_ref[...], vc_ref[...]
    for jj in range(ATT_QB):
        n = ng * ATT_QB + jj
        start = pl.multiple_of(jnp.clip((n - 1) * ATT_BLOCK, 0, seq - band), ATT_BLOCK)
        if jj == 0:
            bias = bias_ref[jnp.where(n == 0, 1, 0)]
        elif jj == ATT_QB - 1:
            bias = bias_ref[jnp.where(n == nb - 1, 2, 0)]
        else:
            bias = bias_ref[0]
        q = q_ref[jj * ATT_BLOCK:(jj + 1) * ATT_BLOCK, :]
        qs = jnp.concatenate([q[:, g * HEAD_DIM:(g + 1) * HEAD_DIM] for g in range(ATT_GROUP)], axis=0)
        keys = jnp.concatenate([k_ref[pl.ds(start, band), :], k_ctx], axis=0)
        vals = jnp.concatenate([v_ref[pl.ds(start, band), :], v_ctx], axis=0)
        s = _dot_nt(qs, keys)
        s = jnp.concatenate([s[:, :band] + bias, s[:, band:]], axis=1)
        m = jnp.maximum(sink, jnp.max(s, axis=-1, keepdims=True))
        p = jnp.exp(s - m)
        den = jnp.exp(sink - m) + jnp.sum(p, axis=-1, keepdims=True)
        o = _dot(p.astype(BF16), vals) / den
        o_ref[jj * ATT_BLOCK:(jj + 1) * ATT_BLOCK, :] = jnp.concatenate(
            [o[g * ATT_BLOCK:(g + 1) * ATT_BLOCK] for g in range(ATT_GROUP)], axis=1).astype(o_ref.dtype)


def _band_bias():
    t = (np.arange(ATT_GROUP * ATT_BLOCK) % ATT_BLOCK)[:, None]
    i = np.arange(3 * ATT_BLOCK)[None, :]
    shifts = (-ATT_BLOCK, 0, -2 * ATT_BLOCK)
    return jnp.asarray(np.stack([np.where(np.abs(i + sh - t) <= WINDOW, 0.0, -np.inf) for sh in shifts]), F32)


def _attention(qk, v, k_c, v_c, sink, batch, n_q_heads):
    m = qk.shape[0]
    seq, ctx_len = m // batch, k_c.shape[0] // batch
    kv_heads = n_q_heads // ATT_GROUP
    ng = seq // (ATT_BLOCK * ATT_QB)
    assert ng >= 2
    qw = ATT_GROUP * HEAD_DIM
    bias = _band_bias()
    return pl.pallas_call(
        _attn_kernel, grid=(batch, kv_heads, ng),
        in_specs=[pl.BlockSpec(memory_space=pltpu.SMEM),
                  pl.BlockSpec(bias.shape, lambda b, h, n: (0, 0, 0)),
                  pl.BlockSpec((ATT_QB * ATT_BLOCK, qw), lambda b, h, n: (b * ng + n, h)),
                  pl.BlockSpec((seq, HEAD_DIM), lambda b, h, n: (b, n_q_heads + h)),
                  pl.BlockSpec((seq, HEAD_DIM), lambda b, h, n: (b, h)),
                  pl.BlockSpec((ctx_len, HEAD_DIM), lambda b, h, n: (b, h)),
                  pl.BlockSpec((ctx_len, HEAD_DIM), lambda b, h, n: (b, h))],
        out_specs=pl.BlockSpec((ATT_QB * ATT_BLOCK, qw), lambda b, h, n: (b * ng + n, h)),
        out_shape=jax.ShapeDtypeStruct((m, n_q_heads * HEAD_DIM), BF16),
        compiler_params=_cparams("parallel", "parallel", "arbitrary"), name="attention",
    )(sink, bias, qk, qk, v, k_c, v_c)


def _rope_tables(seq):
    rows = seq // GRID_W
    row = jnp.repeat(jnp.arange(rows, dtype=F32), GRID_W)
    col = jnp.tile(jnp.arange(GRID_W, dtype=F32), rows)
    n_freq = HEAD_DIM // 4
    inv_freq = ROPE_THETA ** (-jnp.arange(n_freq, dtype=F32) / n_freq)
    ang = jnp.concatenate([row[:, None] * inv_freq, col[:, None] * inv_freq], axis=-1)
    cos, sin = jnp.cos(ang), jnp.sin(ang)
    return jnp.concatenate([cos, cos], axis=-1), jnp.concatenate([-sin, sin], axis=-1)


def kernel(x, c, ctx, c_ctx, ada_w, ada_b, ffn_w_gate, ffn_w_up, ffn_w_down, even_w_in, even_gate_w2_fwd, even_gate_b_fwd, even_gate_w2_bwd, even_gate_b_bwd, even_gla_norm_gain, even_sgu_norm_gain, even_sgu_w_s, even_sgu_b_s, even_w_out, odd_w_in, odd_q_norm_gain, odd_k_norm_gain, odd_sink, odd_w_out):
    batch, seq, d = x.shape
    ctx_len = ctx.shape[1]
    depth = ada_w.shape[0]
    assert depth == 2 and batch + 1 <= 8
    xl = x.reshape(batch * seq, d)
    xc = ctx.reshape(batch * ctx_len, d)

    cc = jnp.concatenate([c, c_ctx[None], jnp.zeros((8 - batch - 1, d), F32)], axis=0)
    mods = _ada(cc, ada_w, ada_b)

    def mod_l(layer, which):
        return mods[layer, :batch, which * d:(which + 1) * d].reshape(batch, 1, d)

    def mod_c(layer, which):
        return mods[layer, batch:batch + 1, which * d:(which + 1) * d].reshape(1, 1, d)

    all_ctx = batch * ctx_len

    dv = even_gla_norm_gain.shape[1]
    key_w = even_gate_w2_fwd.shape[2]
    dk = key_w // GLA_HEADS
    val_w = GLA_HEADS * dv
    sgu_w = even_sgu_norm_gain.shape[1]
    w_in = even_w_in[0]
    gate0 = 2 * key_w + 2 * val_w
    gate1 = gate0 + 2 * GLA_GATE_RANK
    w_main = jnp.concatenate([w_in[:, :gate0], w_in[:, gate1:]], axis=1).astype(BF16)
    w_gate = jnp.pad(w_in[:, gate0:gate1], ((0, 0), (0, LANE - 2 * GLA_GATE_RANK))).astype(BF16)
    pad_rows = LANE - GLA_GATE_RANK
    w2f = jnp.pad(even_gate_w2_fwd[0], ((0, pad_rows), (0, 0))).astype(BF16)
    w2b = jnp.pad(even_gate_w2_bwd[0], ((GLA_GATE_RANK, pad_rows - GLA_GATE_RANK), (0, 0))).astype(BF16)
    b2f = even_gate_b_fwd[0].reshape(1, key_w)
    b2b = even_gate_b_bwd[0].reshape(1, key_w)

    h_l = _modulate(xl, mod_l(0, 0), mod_l(0, 1), seq)
    h_c = _modulate(xc, mod_c(0, 0), mod_c(0, 1), all_ctx)
    p_l, p_c = _proj(h_l, w_main), _proj(h_c, w_main)
    g_l, g_c = _proj(h_l, w_gate, tn=LANE), _proj(h_c, w_gate, tn=LANE)
    gla_c, gla_l = _gla(p_l, p_c, g_l, g_c, w2f, b2f, w2b, b2b, even_gla_norm_gain[0].reshape(1, dv),
                        batch, dk, dv)
    w_s = even_sgu_w_s[0].astype(BF16)
    b_s_t = even_sgu_b_s[0].T
    sgu_gain = even_sgu_norm_gain[0].reshape(1, sgu_w)
    u_blk = gate0 // sgu_w
    sgu_l = _sgu(p_l, w_s, b_s_t, sgu_gain, sgu_w, u_blk, u_blk + 1)
    sgu_c = _sgu(p_c, w_s, b_s_t, sgu_gain, sgu_w, u_blk, u_blk + 1)
    w_out = even_w_out[0].astype(BF16)
    w_out_gla, w_out_sgu = w_out[:val_w], w_out[val_w:]
    x_l = _out_res([gla_l, sgu_l], [w_out_gla, w_out_sgu], xl, mod_l(0, 2), seq)
    x_c = _out_res([gla_c, sgu_c], [w_out_gla, w_out_sgu], xc, mod_c(0, 2), all_ctx)

    wg, wu, wd = ffn_w_gate[0].astype(BF16), ffn_w_up[0].astype(BF16), ffn_w_down[0].astype(BF16)
    x_l = _ffn(x_l, mod_l(0, 3), mod_l(0, 4), mod_l(0, 5), wg, wu, wd, seq)
    x_c = _ffn(x_c, mod_c(0, 3), mod_c(0, 4), mod_c(0, 5), wg, wu, wd, all_ctx)

    h_l = _modulate(x_l, mod_l(1, 0), mod_l(1, 1), seq)
    h_c = _modulate(x_c, mod_c(1, 0), mod_c(1, 1), all_ctx)

    n_heads = odd_sink.shape[1]
    q_w = n_heads * HEAD_DIM
    kv_w = q_w // ATT_GROUP
    w_in = odd_w_in[0]
    w_qk = w_in[:, :q_w + kv_w].astype(BF16)
    w_k = w_in[:, q_w:q_w + kv_w].astype(BF16)
    w_v = w_in[:, q_w + kv_w:].astype(BF16)
    k_gain = jnp.tile(odd_k_norm_gain[0], kv_w // HEAD_DIM)
    qk_gain = jnp.concatenate([jnp.tile(odd_q_norm_gain[0] * HEAD_DIM ** -0.5, n_heads), k_gain]).reshape(1, -1)
    cos2, sin2 = _rope_tables(seq)
    qk_l = _qk_proj(h_l, w_qk, qk_gain, cos2, sin2, rope=True)
    v_l = _proj(h_l, w_v)
    k_c = _qk_proj(h_c, w_k, k_gain.reshape(1, -1), cos2, sin2, rope=False)
    v_c = _proj(h_c, w_v)
    att = _attention(qk_l, v_l, k_c, v_c, odd_sink[0], batch, n_heads)
    x_l = _out_res([att], [odd_w_out[0].astype(BF16)], x_l, mod_l(1, 2), seq)

    wg, wu, wd = ffn_w_gate[1].astype(BF16), ffn_w_up[1].astype(BF16), ffn_w_down[1].astype(BF16)
    return _ffn(x_l, mod_l(1, 3), mod_l(1, 4), mod_l(1, 5), wg, wu, wd, seq).reshape(batch, seq, d)
```

```python
import functools

import numpy as np
import jax
import jax.numpy as jnp
from jax import lax
from jax.experimental import pallas as pl
from jax.experimental.pallas import tpu as pltpu

F32 = jnp.float32
BF16 = jnp.bfloat16

EPS = 1e-6
N_MOD = 6
GRID_W = 64
ROPE_THETA = 10000.0

GLA_HEADS = 4
GLA_TAU = 16.0
GLA_GATE_RANK = 16
GLA_BLOCK = 64
GLA_TILE = 256
SGU_GROUPS = 8
SGU_CHUNK = 128
HEAD_DIM = 128
ATT_GROUP = 4
WINDOW = 128
ATT_BLOCK = 128
ATT_QB = 4

LANE = 128
MXU_W = 256
VMEM_LIMIT = 56 * 1024 * 1024
FFN_VMEM_LIMIT = 60 * 1024 * 1024

TM = 1024
TN = 512
FFN_TH = 256
ROWS_EW = 256


def _cparams(*sem):
    return pltpu.CompilerParams(dimension_semantics=sem, vmem_limit_bytes=VMEM_LIMIT)


def _dot(a, b):
    return jnp.dot(a, b, preferred_element_type=F32)


def _dot_nt(a, b):
    return lax.dot_general(a, b, (((1,), (1,)), ((), ())), preferred_element_type=F32)


def _dot_tn(a, b):
    return lax.dot_general(a, b, (((0,), (0,)), ((), ())), preferred_element_type=F32)


def _silu(x):
    return x * jax.nn.sigmoid(x)


def _gelu_tanh(x):
    c = np.float32(np.sqrt(2.0 / np.pi))
    return x * (0.5 * (1.0 + jnp.tanh(c * (x + 0.044715 * (x * x * x)))))


def _rms(x):
    return x * lax.rsqrt(jnp.mean(x * x, axis=-1, keepdims=True) + EPS)


def _ada_kernel(c_ref, w_ref, b_ref, o_ref):
    a = _silu(c_ref[...]).astype(BF16)
    o_ref[...] = _dot(a, w_ref[...].astype(BF16)) + b_ref[...]


def _ada(cc, ada_w, ada_b):
    depth, d, n = ada_w.shape
    rows = cc.shape[0]
    return pl.pallas_call(
        _ada_kernel,
        grid=(depth, n // TN),
        in_specs=[pl.BlockSpec((rows, d), lambda l, j: (0, 0)),
                  pl.BlockSpec((None, d, TN), lambda l, j: (l, 0, j)),
                  pl.BlockSpec((None, 1, TN), lambda l, j: (l, 0, j))],
        out_specs=pl.BlockSpec((None, rows, TN), lambda l, j: (l, 0, j)),
        out_shape=jax.ShapeDtypeStruct((depth, rows, n), F32),
        compiler_params=_cparams("parallel", "parallel"),
        name="ada",
    )(cc, ada_w, ada_b.reshape(depth, 1, n))


def _modulate_kernel(x_ref, sh_ref, sc_ref, h_ref):
    h_ref[...] = (_rms(x_ref[...]) * (1.0 + sc_ref[...]) + sh_ref[...]).astype(h_ref.dtype)


def _row_spec(d):
    return pl.BlockSpec((ROWS_EW, d), lambda i: (i, 0))


def _vec_spec(d, rows_per_vec):
    return pl.BlockSpec((None, 1, d), lambda i: ((i * ROWS_EW) // rows_per_vec, 0, 0))


def _modulate(x, shift, scale, rows_per_vec):
    m, d = x.shape
    return pl.pallas_call(
        _modulate_kernel, grid=(m // ROWS_EW,),
        in_specs=[_row_spec(d), _vec_spec(d, rows_per_vec), _vec_spec(d, rows_per_vec)],
        out_specs=_row_spec(d),
        out_shape=jax.ShapeDtypeStruct((m, d), BF16),
        compiler_params=_cparams("parallel"), name="modulate",
    )(x, shift, scale)


def _cast_kernel(w_ref, o_ref):
    o_ref[...] = w_ref[...].astype(o_ref.dtype)


def _cast_col_tiles(w, layer, col0, n_cols, tile):
    k = w.shape[1]
    assert col0 % tile == 0 and n_cols % tile == 0
    t0 = col0 // tile
    return pl.pallas_call(
        _cast_kernel, grid=(n_cols // tile,),
        in_specs=[pl.BlockSpec((None, k, tile), lambda j: (layer, 0, t0 + j))],
        out_specs=pl.BlockSpec((None, k, tile), lambda j: (j, 0, 0)),
        out_shape=jax.ShapeDtypeStruct((n_cols // tile, k, tile), BF16),
        compiler_params=_cparams("parallel"), name="cast_cols",
    )(w)


def _cast_rows(w, layer, tile):
    k, n = w.shape[1:]
    return pl.pallas_call(
        _cast_kernel, grid=(k // tile,),
        in_specs=[pl.BlockSpec((None, tile, n), lambda j: (layer, j, 0))],
        out_specs=pl.BlockSpec((tile, n), lambda j: (j, 0)),
        out_shape=jax.ShapeDtypeStruct((k, n), BF16),
        compiler_params=_cparams("parallel"), name="cast_rows",
    )(w)


def _col_tiles(w, tile):
    k, n = w.shape
    return jnp.transpose(w.reshape(k, n // tile, tile), (1, 0, 2))


def _proj_kernel(x_ref, w_ref, o_ref):
    o_ref[...] = _dot(x_ref[...], w_ref[...]).astype(o_ref.dtype)


def _proj(x, w3):
    m, k = x.shape
    nt, _, tn = w3.shape
    tm = min(TM, m)
    return pl.pallas_call(
        _proj_kernel, grid=(m // tm, nt),
        in_specs=[pl.BlockSpec((tm, k), lambda i, j: (i, 0)),
                  pl.BlockSpec((None, k, tn), lambda i, j: (j, 0, 0))],
        out_specs=pl.BlockSpec((tm, tn), lambda i, j: (i, j)),
        out_shape=jax.ShapeDtypeStruct((m, nt * tn), BF16),
        compiler_params=_cparams("parallel", "parallel"), name="proj",
    )(x, w3)


def _qk_proj_kernel(x_ref, w_ref, gain_ref, cos_ref, sin_ref, ones_ref, o_ref, *, rope):
    half = x_ref.shape[0] // 2
    for rr in range(2):
        rows = slice(rr * half, (rr + 1) * half)
        x = x_ref[rows, :]
        for cc in range(w_ref.shape[1] // TN):
            acc = _dot(x, w_ref[:, cc * TN:(cc + 1) * TN])
            for tt in range(TN // MXU_W):
                c0 = cc * TN + tt * MXU_W
                a = acc[:, tt * MXU_W:(tt + 1) * MXU_W]
                ssq = _dot((a * a).astype(BF16), ones_ref[...])
                y = a * lax.rsqrt(ssq * (1.0 / HEAD_DIM) + EPS) * gain_ref[:, c0:c0 + MXU_W]
                for hh in range(MXU_W // HEAD_DIM):
                    yh = y[:, hh * HEAD_DIM:(hh + 1) * HEAD_DIM]
                    if rope:
                        yh = (yh * cos_ref[rows, :]
                              + pltpu.roll(yh, HEAD_DIM // 2, 1) * sin_ref[rows, :])
                    o_ref[rows, c0 + hh * HEAD_DIM:c0 + (hh + 1) * HEAD_DIM] = yh.astype(o_ref.dtype)


def _qk_proj(x, w3, tile0, n_tiles, gain, cos2, sin2, rope):
    m, k = x.shape
    tn = w3.shape[2]
    tm = min(TM, m)
    seq_blocks = cos2.shape[0] // tm if rope else 1
    tab = pl.BlockSpec((tm, HEAD_DIM), lambda i, j: (i % seq_blocks, 0))
    head_of = np.arange(MXU_W) // HEAD_DIM
    ones = jnp.asarray(head_of[:, None] == head_of[None, :], BF16)
    return pl.pallas_call(
        functools.partial(_qk_proj_kernel, rope=rope), grid=(m // tm, n_tiles),
        in_specs=[pl.BlockSpec((tm, k), lambda i, j: (i, 0)),
                  pl.BlockSpec((None, k, tn), lambda i, j: (tile0 + j, 0, 0)),
                  pl.BlockSpec((1, tn), lambda i, j: (0, j)),
                  tab, tab,
                  pl.BlockSpec((MXU_W, MXU_W), lambda i, j: (0, 0))],
        out_specs=pl.BlockSpec((tm, tn), lambda i, j: (i, j)),
        out_shape=jax.ShapeDtypeStruct((m, n_tiles * tn), BF16),
        compiler_params=_cparams("parallel", "parallel"), name="qk_proj",
    )(x, w3, gain, cos2, sin2, ones)


def _out_res_kernel(*refs, n_in):
    xs, ws = refs[:n_in], refs[n_in:2 * n_in]
    res_ref, g_ref, o_ref = refs[2 * n_in:]
    acc = _dot(xs[0][...], ws[0][...])
    for x_ref, w_ref in zip(xs[1:], ws[1:]):
        acc = acc + _dot(x_ref[...], w_ref[...])
    o_ref[...] = res_ref[...] + g_ref[...] * acc


def _out_res(xs, w3, res, gate, rows_per_vec):
    m, n = res.shape
    nt, k, tn = w3.shape
    tm = min(TM, m)
    n_in = len(xs)
    kx = k // n_in
    assert all(x.shape[1] == kx for x in xs) and nt * tn == n

    def w_spec(part):
        return pl.BlockSpec((None, kx, tn), lambda i, j: (j, part, 0))

    in_specs = ([pl.BlockSpec((tm, kx), lambda i, j: (i, 0)) for _ in xs]
                + [w_spec(part) for part in range(n_in)]
                + [pl.BlockSpec((tm, tn), lambda i, j: (i, j)),
                   pl.BlockSpec((None, 1, tn), lambda i, j: ((i * tm) // rows_per_vec, 0, j))])
    return pl.pallas_call(
        functools.partial(_out_res_kernel, n_in=n_in), grid=(m // tm, nt),
        in_specs=in_specs,
        out_specs=pl.BlockSpec((tm, tn), lambda i, j: (i, j)),
        out_shape=jax.ShapeDtypeStruct((m, n), F32),
        compiler_params=_cparams("parallel", "parallel"), name="out_res",
    )(*xs, *([w3] * n_in), res, gate)


def _ffn_kernel(x_ref, sh_ref, sc_ref, gate_ref, wg_ref, wu_ref, wd_ref, o_ref, h_ref):
    j = pl.program_id(1)
    n_chunks = x_ref.shape[0] // ROWS_EW

    def chunk_rows(ci):
        return pl.ds(pl.multiple_of(ci * ROWS_EW, ROWS_EW), ROWS_EW)

    @pl.when(j == 0)
    def _():
        def body(ci, carry):
            rows = chunk_rows(ci)
            h_ref[rows, :] = (_rms(x_ref[rows, :]) * (1.0 + sc_ref[...]) + sh_ref[...]).astype(BF16)
            return carry
        lax.fori_loop(0, n_chunks, body, 0)

    h = h_ref[...]
    g = _dot(h, wg_ref[...])
    u = _dot(h, wu_ref[...])
    a = (_silu(g) * u).astype(BF16)

    @pl.when(j == 0)
    def _():
        o_ref[...] = _dot(a, wd_ref[...])

    @pl.when(j > 0)
    def _():
        o_ref[...] += _dot(a, wd_ref[...])

    @pl.when(j == pl.num_programs(1) - 1)
    def _():
        def body(ci, carry):
            rows = chunk_rows(ci)
            o_ref[rows, :] = x_ref[rows, :] + gate_ref[...] * o_ref[rows, :]
            return carry
        lax.fori_loop(0, n_chunks, body, 0)


def _ffn(x, shift, scale, gate, wg3, wu3, wd, rows_per_vec):
    m, d = x.shape
    n_tiles = wg3.shape[0]
    tm = min(TM, m)
    vec = pl.BlockSpec((None, 1, d), lambda i, j: ((i * tm) // rows_per_vec, 0, 0))
    return pl.pallas_call(
        _ffn_kernel, grid=(m // tm, n_tiles),
        in_specs=[pl.BlockSpec((tm, d), lambda i, j: (i, 0), pipeline_mode=pl.Buffered(1)),
                  vec, vec, vec,
                  pl.BlockSpec((None, d, FFN_TH), lambda i, j: (j, 0, 0)),
                  pl.BlockSpec((None, d, FFN_TH), lambda i, j: (j, 0, 0)),
                  pl.BlockSpec((FFN_TH, d), lambda i, j: (j, 0))],
        out_specs=pl.BlockSpec((tm, d), lambda i, j: (i, 0), pipeline_mode=pl.Buffered(1)),
        out_shape=jax.ShapeDtypeStruct((m, d), F32),
        scratch_shapes=[pltpu.VMEM((tm, d), BF16)],
        compiler_params=pltpu.CompilerParams(dimension_semantics=("parallel", "arbitrary"),
                                             vmem_limit_bytes=FFN_VMEM_LIMIT), name="ffn",
    )(x, shift, scale, gate, wg3, wu3, wd)


def _gla_scan_tile(q_ref, k_ref, v_ref, g_ref, w2_ref, b2_ref, s_ref, emit, reverse):
    dk = q_ref.shape[1] // GLA_HEADS
    dv = v_ref.shape[1] // GLA_HEADS
    row = lax.broadcasted_iota(jnp.int32, (GLA_BLOCK, GLA_BLOCK), 0)
    col = lax.broadcasted_iota(jnp.int32, (GLA_BLOCK, GLA_BLOCK), 1)
    if reverse:
        tri = (col >= row).astype(BF16)
        mask = col > row
        last_row = 0
    else:
        tri = (col <= row).astype(BF16)
        mask = col <= row
        last_row = GLA_BLOCK - 1
    n_steps = q_ref.shape[0] // GLA_BLOCK
    for step in range(n_steps):
        c = n_steps - 1 - step if reverse else step
        rows = slice(c * GLA_BLOCK, (c + 1) * GLA_BLOCK)
        q = q_ref[rows, :].astype(F32) * (dk ** -0.5)
        k = k_ref[rows, :].astype(F32)
        x = _dot(g_ref[rows, :], w2_ref[...]) + b2_ref[...]
        log_a = (jnp.minimum(x, 0.0) - jnp.log(1.0 + jnp.exp(-jnp.abs(x)))) * (1.0 / GLA_TAU)
        hi = log_a.astype(BF16)
        lo = (log_a - hi.astype(F32)).astype(BF16)
        b = _dot(tri, hi) + _dot(tri, lo)
        mid = b[GLA_BLOCK // 2:GLA_BLOCK // 2 + 1, :]
        b_last = b[last_row:last_row + 1, :]
        qe = q * jnp.exp(b - mid)
        ke = k * jnp.exp(mid - b)
        q_in = (qe * jnp.exp(mid)).astype(BF16)
        k_out = (ke * jnp.exp(b_last - mid)).astype(BF16)
        decay = jnp.exp(b_last)
        qe, ke = qe.astype(BF16), ke.astype(BF16)
        for h in range(GLA_HEADS):
            ks = slice(h * dk, (h + 1) * dk)
            v = v_ref[rows, h * dv:(h + 1) * dv]
            att = jnp.where(mask, _dot_nt(qe[:, ks], ke[:, ks]), 0.0).astype(BF16)
            s = s_ref[h]
            emit(rows, h, _dot_nt(q_in[:, ks], s.astype(BF16)) + _dot(att, v))
            s_ref[h] = s * decay[:, ks] + _dot_tn(v, k_out[:, ks])


def _gla_rev_kernel(qc_ref, kc_ref, vc_ref, gc_ref, ql_ref, kl_ref, vl_ref, gl_ref, w2_ref, b2_ref,
                    oc_ref, ol_ref, s_ref):
    t = pl.program_id(1)
    dv = vc_ref.shape[1] // GLA_HEADS

    def run(q_ref, k_ref, v_ref, g_ref, o_ref):
        def emit(rows, h, o):
            o_ref[rows, h * dv:(h + 1) * dv] = o
        _gla_scan_tile(q_ref, k_ref, v_ref, g_ref, w2_ref, b2_ref, s_ref, emit, reverse=True)

    @pl.when(t == 0)
    def _():
        s_ref[...] = jnp.zeros_like(s_ref)
        run(qc_ref, kc_ref, vc_ref, gc_ref, oc_ref)

    @pl.when(t > 0)
    def _():
        run(ql_ref, kl_ref, vl_ref, gl_ref, ol_ref)


def _gla_fwd_kernel(qc_ref, kc_ref, vc_ref, rc_ref, gc_ref, pc_ref, ql_ref, kl_ref, vl_ref, rl_ref, gl_ref, pl_ref,
                    w2_ref, b2_ref, gain_ref, oc_ref, ol_ref, s_ref):
    t = pl.program_id(1)
    dv = vc_ref.shape[1] // GLA_HEADS

    def run(q_ref, k_ref, v_ref, r_ref, g_ref, p_ref, o_ref):
        def emit(rows, h, o):
            cols = slice(h * dv, (h + 1) * dv)
            total = o + p_ref[rows, cols]
            r = r_ref[rows, cols].astype(F32)
            o_ref[rows, cols] = (_rms(total) * gain_ref[...] * _silu(r)).astype(o_ref.dtype)
        _gla_scan_tile(q_ref, k_ref, v_ref, g_ref, w2_ref, b2_ref, s_ref, emit, reverse=False)

    @pl.when(t == 0)
    def _():
        s_ref[...] = jnp.zeros_like(s_ref)
        run(qc_ref, kc_ref, vc_ref, rc_ref, gc_ref, pc_ref, oc_ref)

    @pl.when(t > 0)
    def _():
        run(ql_ref, kl_ref, vl_ref, rl_ref, gl_ref, pl_ref, ol_ref)


def _gla(p_l, p_c, g_l, g_c, w2f, b2f, w2b, b2b, gain, batch, dk, dv):
    seq, ctx_len = p_l.shape[0] // batch, p_c.shape[0] // batch
    assert ctx_len == GLA_TILE and seq % GLA_TILE == 0
    n_lat = seq // GLA_TILE
    key_w, val_w = GLA_HEADS * dk, GLA_HEADS * dv
    assert val_w == 2 * key_w

    def lat_fwd(b, t):
        return b * n_lat + jnp.maximum(t - 1, 0)

    def lat_rev(b, t):
        return b * n_lat + n_lat - jnp.maximum(t, 1)

    def tok_specs(row_blk, with_r):
        specs = [pl.BlockSpec((GLA_TILE, key_w), lambda b, t: (row_blk(b, t), 0)),
                 pl.BlockSpec((GLA_TILE, key_w), lambda b, t: (row_blk(b, t), 1)),
                 pl.BlockSpec((GLA_TILE, val_w), lambda b, t: (row_blk(b, t), 1))]
        if with_r:
            specs.append(pl.BlockSpec((GLA_TILE, val_w), lambda b, t: (row_blk(b, t), 2)))
        return specs + [pl.BlockSpec((GLA_TILE, LANE), lambda b, t: (row_blk(b, t), 0))]

    def out_spec(row_blk):
        return pl.BlockSpec((GLA_TILE, val_w), lambda b, t: (row_blk(b, t), 0))

    def ctx_blk(b, t):
        return b

    w2_spec = pl.BlockSpec((LANE, key_w), lambda b, t: (0, 0))
    b2_spec = pl.BlockSpec((1, key_w), lambda b, t: (0, 0))
    state = pltpu.VMEM((GLA_HEADS, dv, dk), F32)
    grid = (batch, 1 + n_lat)
    rev_c, rev_l = pl.pallas_call(
        _gla_rev_kernel, grid=grid,
        in_specs=tok_specs(ctx_blk, False) + tok_specs(lat_rev, False) + [w2_spec, b2_spec],
        out_specs=[out_spec(ctx_blk), out_spec(lat_rev)],
        out_shape=[jax.ShapeDtypeStruct((batch * ctx_len, val_w), F32),
                   jax.ShapeDtypeStruct((batch * seq, val_w), F32)],
        scratch_shapes=[state],
        compiler_params=_cparams("parallel", "arbitrary"), name="gla_rev",
    )(p_c, p_c, p_c, g_c, p_l, p_l, p_l, g_l, w2b, b2b)
    return pl.pallas_call(
        _gla_fwd_kernel, grid=grid,
        in_specs=tok_specs(ctx_blk, True) + [out_spec(ctx_blk)] + tok_specs(lat_fwd, True) + [out_spec(lat_fwd)]
        + [w2_spec, b2_spec, pl.BlockSpec((1, dv), lambda b, t: (0, 0))],
        out_specs=[out_spec(ctx_blk), out_spec(lat_fwd)],
        out_shape=[jax.ShapeDtypeStruct((batch * ctx_len, val_w), BF16),
                   jax.ShapeDtypeStruct((batch * seq, val_w), BF16)],
        scratch_shapes=[state],
        compiler_params=_cparams("parallel", "arbitrary"), name="gla_fwd",
    )(p_c, p_c, p_c, p_c, g_c, rev_c, p_l, p_l, p_l, p_l, g_l, rev_l, w2f, b2f, gain)


def _sgu_kernel(u_ref, s_ref, ws_ref, bs_ref, gain_ref, o_ref):
    gw = u_ref.shape[1] // SGU_GROUPS
    for ci in range(u_ref.shape[0] // SGU_CHUNK):
        rows = slice(ci * SGU_CHUNK, (ci + 1) * SGU_CHUNK)
        for g in range(SGU_GROUPS):
            cols = slice(g * gw, (g + 1) * gw)
            vg = _rms(_gelu_tanh(s_ref[rows, cols].astype(F32))) * gain_ref[:, cols]
            mixed = _dot(ws_ref[g], vg.astype(BF16)) + bs_ref[:, g:g + 1]
            o_ref[rows, cols] = (_gelu_tanh(u_ref[rows, cols].astype(F32)) * mixed).astype(o_ref.dtype)


def _sgu(p, w_s, b_s_t, gain, width, u_blk, s_blk):
    m = p.shape[0]
    rows = 2 * SGU_CHUNK
    return pl.pallas_call(
        _sgu_kernel, grid=(m // rows,),
        in_specs=[pl.BlockSpec((rows, width), lambda i: (i, u_blk)),
                  pl.BlockSpec((rows, width), lambda i: (i, s_blk)),
                  pl.BlockSpec(w_s.shape, lambda i: (0, 0, 0)),
                  pl.BlockSpec(b_s_t.shape, lambda i: (0, 0)),
                  pl.BlockSpec((1, width), lambda i: (0, 0))],
        out_specs=pl.BlockSpec((rows, width), lambda i: (i, 0)),
        out_shape=jax.ShapeDtypeStruct((m, width), BF16),
        compiler_params=_cparams("parallel"), name="sgu",
    )(p, p, w_s, b_s_t, gain)


def _attn_kernel(sink_ref, bias_ref, q_ref, k_ref, v_ref, kc_ref, vc_ref, o_ref):
    kv, ng = pl.program_id(1), pl.program_id(2)
    seq = k_ref.shape[0]
    nb = seq // ATT_BLOCK
    band = 3 * ATT_BLOCK
    sink = jnp.concatenate([jnp.full((ATT_BLOCK, 1), sink_ref[kv * ATT_GROUP + g], F32)
                            for g in range(ATT_GROUP)], axis=0)
    k_ctx, v_ctx = kc_ref[...], vc_ref[...]
    for jj in range(ATT_QB):
        n = ng * ATT_QB + jj
        start = pl.multiple_of(jnp.clip((n - 1) * ATT_BLOCK, 0, seq - band), ATT_BLOCK)
        if jj == 0:
            bias = bias_ref[jnp.where(n == 0, 1, 0)]
        elif jj == ATT_QB - 1:
            bias = bias_ref[jnp.where(n == nb - 1, 2, 0)]
        else:
            bias = bias_ref[0]
        q = q_ref[jj * ATT_BLOCK:(jj + 1) * ATT_BLOCK, :]
        qs = jnp.concatenate([q[:, g * HEAD_DIM:(g + 1) * HEAD_DIM] for g in range(ATT_GROUP)], axis=0)
        keys = jnp.concatenate([k_ref[pl.ds(start, band), :], k_ctx], axis=0)
        vals = jnp.concatenate([v_ref[pl.ds(start, band), :], v_ctx], axis=0)
        s = _dot_nt(qs, keys)
        s = jnp.concatenate([s[:, :band] + bias, s[:, band:]], axis=1)
        m = jnp.maximum(sink, jnp.max(s, axis=-1, keepdims=True))
        p = jnp.exp(s - m)
        den = jnp.exp(sink - m) + jnp.sum(p, axis=-1, keepdims=True)
        o = _dot(p.astype(BF16), vals) / den
        o_ref[jj * ATT_BLOCK:(jj + 1) * ATT_BLOCK, :] = jnp.concatenate(
            [o[g * ATT_BLOCK:(g + 1) * ATT_BLOCK] for g in range(ATT_GROUP)], axis=1).astype(o_ref.dtype)


def _band_bias():
    t = (np.arange(ATT_GROUP * ATT_BLOCK) % ATT_BLOCK)[:, None]
    i = np.arange(3 * ATT_BLOCK)[None, :]
    shifts = (-ATT_BLOCK, 0, -2 * ATT_BLOCK)
    return jnp.asarray(np.stack([np.where(np.abs(i + sh - t) <= WINDOW, 0.0, -np.inf) for sh in shifts]), F32)


def _attention(qk, v, k_c, v_c, sink, batch, n_q_heads):
    m = qk.shape[0]
    seq, ctx_len = m // batch, k_c.shape[0] // batch
    kv_heads = n_q_heads // ATT_GROUP
    ng = seq // (ATT_BLOCK * ATT_QB)
    assert ng >= 2
    qw = ATT_GROUP * HEAD_DIM
    bias = _band_bias()
    return pl.pallas_call(
        _attn_kernel, grid=(batch, kv_heads, ng),
        in_specs=[pl.BlockSpec(memory_space=pltpu.SMEM),
                  pl.BlockSpec(bias.shape, lambda b, h, n: (0, 0, 0)),
                  pl.BlockSpec((ATT_QB * ATT_BLOCK, qw), lambda b, h, n: (b * ng + n, h)),
                  pl.BlockSpec((seq, HEAD_DIM), lambda b, h, n: (b, n_q_heads + h)),
                  pl.BlockSpec((seq, HEAD_DIM), lambda b, h, n: (b, h)),
                  pl.BlockSpec((ctx_len, HEAD_DIM), lambda b, h, n: (b, h)),
                  pl.BlockSpec((ctx_len, HEAD_DIM), lambda b, h, n: (b, h))],
        out_specs=pl.BlockSpec((ATT_QB * ATT_BLOCK, qw), lambda b, h, n: (b * ng + n, h)),
        out_shape=jax.ShapeDtypeStruct((m, n_q_heads * HEAD_DIM), BF16),
        compiler_params=_cparams("parallel", "parallel", "arbitrary"), name="attention",
    )(sink, bias, qk, qk, v, k_c, v_c)


def _rope_tables(seq):
    rows = seq // GRID_W
    row = jnp.repeat(jnp.arange(rows, dtype=F32), GRID_W)
    col = jnp.tile(jnp.arange(GRID_W, dtype=F32), rows)
    n_freq = HEAD_DIM // 4
    inv_freq = ROPE_THETA ** (-jnp.arange(n_freq, dtype=F32) / n_freq)
    ang = jnp.concatenate([row[:, None] * inv_freq, col[:, None] * inv_freq], axis=-1)
    cos, sin = jnp.cos(ang), jnp.sin(ang)
    return jnp.concatenate([cos, cos], axis=-1), jnp.concatenate([-sin, sin], axis=-1)


def kernel(x, c, ctx, c_ctx, ada_w, ada_b, ffn_w_gate, ffn_w_up, ffn_w_down, even_w_in, even_gate_w2_fwd, even_gate_b_fwd, even_gate_w2_bwd, even_gate_b_bwd, even_gla_norm_gain, even_sgu_norm_gain, even_sgu_w_s, even_sgu_b_s, even_w_out, odd_w_in, odd_q_norm_gain, odd_k_norm_gain, odd_sink, odd_w_out):
    batch, seq, d = x.shape
    ctx_len = ctx.shape[1]
    depth = ada_w.shape[0]
    assert depth == 2 and batch + 1 <= 8
    xl = x.reshape(batch * seq, d)
    xc = ctx.reshape(batch * ctx_len, d)

    cc = jnp.concatenate([c, c_ctx[None], jnp.zeros((8 - batch - 1, d), F32)], axis=0)
    mods = _ada(cc, ada_w, ada_b)

    def mod_l(layer, which):
        return mods[layer, :batch, which * d:(which + 1) * d].reshape(batch, 1, d)

    def mod_c(layer, which):
        return mods[layer, batch:batch + 1, which * d:(which + 1) * d].reshape(1, 1, d)

    all_ctx = batch * ctx_len

    dv = even_gla_norm_gain.shape[1]
    key_w = even_gate_w2_fwd.shape[2]
    dk = key_w // GLA_HEADS
    val_w = GLA_HEADS * dv
    sgu_w = even_sgu_norm_gain.shape[1]
    gate0 = 2 * key_w + 2 * val_w
    gate1 = gate0 + 2 * GLA_GATE_RANK
    w_qkvr = _cast_col_tiles(even_w_in, 0, 0, gate0, TN)
    w_us = _col_tiles(even_w_in[0, :, gate1:].astype(BF16), TN)
    w_gate = jnp.pad(even_w_in[0, :, gate0:gate1], ((0, 0), (0, LANE - 2 * GLA_GATE_RANK))).astype(BF16)[None]
    pad_rows = LANE - GLA_GATE_RANK
    w2f = jnp.pad(even_gate_w2_fwd[0], ((0, pad_rows), (0, 0))).astype(BF16)
    w2b = jnp.pad(even_gate_w2_bwd[0], ((GLA_GATE_RANK, pad_rows - GLA_GATE_RANK), (0, 0))).astype(BF16)
    b2f = even_gate_b_fwd[0].reshape(1, key_w)
    b2b = even_gate_b_bwd[0].reshape(1, key_w)

    h_l = _modulate(xl, mod_l(0, 0), mod_l(0, 1), seq)
    h_c = _modulate(xc, mod_c(0, 0), mod_c(0, 1), all_ctx)
    p_l, p_c = _proj(h_l, w_qkvr), _proj(h_c, w_qkvr)
    us_l, us_c = _proj(h_l, w_us), _proj(h_c, w_us)
    g_l, g_c = _proj(h_l, w_gate), _proj(h_c, w_gate)
    gla_c, gla_l = _gla(p_l, p_c, g_l, g_c, w2f, b2f, w2b, b2b, even_gla_norm_gain[0].reshape(1, dv),
                        batch, dk, dv)
    w_s = even_sgu_w_s[0].astype(BF16)
    b_s_t = even_sgu_b_s[0].T
    sgu_gain = even_sgu_norm_gain[0].reshape(1, sgu_w)
    sgu_l = _sgu(us_l, w_s, b_s_t, sgu_gain, sgu_w, 0, 1)
    sgu_c = _sgu(us_c, w_s, b_s_t, sgu_gain, sgu_w, 0, 1)
    assert val_w == sgu_w
    w_out = _cast_col_tiles(even_w_out, 0, 0, d, TN)
    x_l = _out_res([gla_l, sgu_l], w_out, xl, mod_l(0, 2), seq)
    x_c = _out_res([gla_c, sgu_c], w_out, xc, mod_c(0, 2), all_ctx)

    hidden = ffn_w_gate.shape[2]
    wg = _cast_col_tiles(ffn_w_gate, 0, 0, hidden, FFN_TH)
    wu = _cast_col_tiles(ffn_w_up, 0, 0, hidden, FFN_TH)
    wd = _cast_rows(ffn_w_down, 0, FFN_TH)
    x_l = _ffn(x_l, mod_l(0, 3), mod_l(0, 4), mod_l(0, 5), wg, wu, wd, seq)
    x_c = _ffn(x_c, mod_c(0, 3), mod_c(0, 4), mod_c(0, 5), wg, wu, wd, all_ctx)

    h_l = _modulate(x_l, mod_l(1, 0), mod_l(1, 1), seq)
    h_c = _modulate(x_c, mod_c(1, 0), mod_c(1, 1), all_ctx)

    n_heads = odd_sink.shape[1]
    q_w = n_heads * HEAD_DIM
    kv_w = q_w // ATT_GROUP
    qk_tile = 2 * TN
    assert kv_w == qk_tile
    w_qk = _cast_col_tiles(odd_w_in, 0, 0, q_w + kv_w, qk_tile)
    w_v = _cast_col_tiles(odd_w_in, 0, q_w + kv_w, kv_w, TN)
    k_gain = jnp.tile(odd_k_norm_gain[0], kv_w // HEAD_DIM)
    qk_gain = jnp.concatenate([jnp.tile(odd_q_norm_gain[0] * HEAD_DIM ** -0.5, n_heads), k_gain]).reshape(1, -1)
    cos2, sin2 = _rope_tables(seq)
    qk_l = _qk_proj(h_l, w_qk, 0, w_qk.shape[0], qk_gain, cos2, sin2, rope=True)
    v_l = _proj(h_l, w_v)
    k_c = _qk_proj(h_c, w_qk, q_w // qk_tile, 1, k_gain.reshape(1, -1), cos2, sin2, rope=False)
    v_c = _proj(h_c, w_v)
    att = _attention(qk_l, v_l, k_c, v_c, odd_sink[0], batch, n_heads)
    x_l = _out_res([att], _cast_col_tiles(odd_w_out, 0, 0, d, TN), x_l, mod_l(1, 2), seq)

    wg = _cast_col_tiles(ffn_w_gate, 1, 0, hidden, FFN_TH)
    wu = _cast_col_tiles(ffn_w_up, 1, 0, hidden, FFN_TH)
    wd = _cast_rows(ffn_w_down, 1, FFN_TH)
    return _ffn(x_l, mod_l(1, 3), mod_l(1, 4), mod_l(1, 5), wg, wu, wd, seq).reshape(batch, seq, d)
```

```python
import functools

import numpy as np
import jax
import jax.numpy as jnp
from jax import lax
from jax.experimental import pallas as pl
from jax.experimental.pallas import tpu as pltpu

F32 = jnp.float32
BF16 = jnp.bfloat16

EPS = 1e-6
LOG2E = float(np.log2(np.e))
N_MOD = 6
GRID_W = 64
ROPE_THETA = 10000.0

GLA_HEADS = 4
GLA_TAU = 16.0
GLA_GATE_RANK = 16
GLA_BLOCK = 64
GLA_TILE = 256
SGU_GROUPS = 8
SGU_CHUNK = 128
HEAD_DIM = 128
ATT_GROUP = 4
WINDOW = 128
ATT_BLOCK = 128
ATT_QB = 8

LANE = 128
MXU_W = 256
VMEM_LIMIT = 56 * 1024 * 1024
FFN_VMEM_LIMIT = 60 * 1024 * 1024

TM = 1024
TN = 512
FFN_TH = 256
FFN_X_ROWS = 128
ROWS_EW = 256


def _cparams(*sem):
    return pltpu.CompilerParams(dimension_semantics=sem, vmem_limit_bytes=VMEM_LIMIT)


def _dot(a, b):
    return jnp.dot(a, b, preferred_element_type=F32)


def _dot_nt(a, b):
    return lax.dot_general(a, b, (((1,), (1,)), ((), ())), preferred_element_type=F32)


def _dot_tn(a, b):
    return lax.dot_general(a, b, (((0,), (0,)), ((), ())), preferred_element_type=F32)


def _silu(x):
    return x * jax.nn.sigmoid(x)


def _gelu_tanh(x):
    c = np.float32(np.sqrt(2.0 / np.pi))
    return x * (0.5 * (1.0 + jnp.tanh(c * (x + 0.044715 * (x * x * x)))))


def _rms(x):
    return x * lax.rsqrt(jnp.mean(x * x, axis=-1, keepdims=True) + EPS)


def _ada_kernel(c_ref, w_ref, b_ref, o_ref):
    a = _silu(c_ref[...]).astype(BF16)
    o_ref[...] = _dot(a, w_ref[...].astype(BF16)) + b_ref[...]


def _ada(cc, ada_w, ada_b):
    depth, d, n = ada_w.shape
    rows = cc.shape[0]
    return pl.pallas_call(
        _ada_kernel,
        grid=(depth, n // TN),
        in_specs=[pl.BlockSpec((rows, d), lambda l, j: (0, 0)),
                  pl.BlockSpec((None, d, TN), lambda l, j: (l, 0, j)),
                  pl.BlockSpec((None, 1, TN), lambda l, j: (l, 0, j))],
        out_specs=pl.BlockSpec((None, rows, TN), lambda l, j: (l, 0, j)),
        out_shape=jax.ShapeDtypeStruct((depth, rows, n), F32),
        compiler_params=_cparams("parallel", "parallel"),
        name="ada",
    )(cc, ada_w, ada_b.reshape(depth, 1, n))


def _modulate_kernel(x_ref, sh_ref, sc_ref, h_ref):
    h_ref[...] = (_rms(x_ref[...]) * (1.0 + sc_ref[...]) + sh_ref[...]).astype(h_ref.dtype)


def _row_spec(d):
    return pl.BlockSpec((ROWS_EW, d), lambda i: (i, 0))


def _vec_spec(d, rows_per_vec):
    return pl.BlockSpec((None, 1, d), lambda i: ((i * ROWS_EW) // rows_per_vec, 0, 0))


def _modulate(x, shift, scale, rows_per_vec):
    m, d = x.shape
    return pl.pallas_call(
        _modulate_kernel, grid=(m // ROWS_EW,),
        in_specs=[_row_spec(d), _vec_spec(d, rows_per_vec), _vec_spec(d, rows_per_vec)],
        out_specs=_row_spec(d),
        out_shape=jax.ShapeDtypeStruct((m, d), BF16),
        compiler_params=_cparams("parallel"), name="modulate",
    )(x, shift, scale)


def _cast_kernel(w_ref, o_ref):
    o_ref[...] = w_ref[...].astype(o_ref.dtype)


def _cast_col_tiles(w, layer, col0, n_cols, tile):
    k = w.shape[1]
    assert col0 % tile == 0 and n_cols % tile == 0
    t0 = col0 // tile
    return pl.pallas_call(
        _cast_kernel, grid=(n_cols // tile,),
        in_specs=[pl.BlockSpec((None, k, tile), lambda j: (layer, 0, t0 + j))],
        out_specs=pl.BlockSpec((None, k, tile), lambda j: (j, 0, 0)),
        out_shape=jax.ShapeDtypeStruct((n_cols // tile, k, tile), BF16),
        compiler_params=_cparams("parallel"), name="cast_cols",
    )(w)


def _cast_rows(w, layer, tile):
    k, n = w.shape[1:]
    return pl.pallas_call(
        _cast_kernel, grid=(k // tile,),
        in_specs=[pl.BlockSpec((None, tile, n), lambda j: (layer, j, 0))],
        out_specs=pl.BlockSpec((tile, n), lambda j: (j, 0)),
        out_shape=jax.ShapeDtypeStruct((k, n), BF16),
        compiler_params=_cparams("parallel"), name="cast_rows",
    )(w)


def _col_tiles(w, tile):
    k, n = w.shape
    return jnp.transpose(w.reshape(k, n // tile, tile), (1, 0, 2))


def _proj_kernel(x_ref, w_ref, o_ref):
    o_ref[...] = _dot(x_ref[...], w_ref[...]).astype(o_ref.dtype)


def _proj(x, w3):
    m, k = x.shape
    nt, _, tn = w3.shape
    tm = min(TM, m)
    return pl.pallas_call(
        _proj_kernel, grid=(m // tm, nt),
        in_specs=[pl.BlockSpec((tm, k), lambda i, j: (i, 0)),
                  pl.BlockSpec((None, k, tn), lambda i, j: (j, 0, 0))],
        out_specs=pl.BlockSpec((tm, tn), lambda i, j: (i, j)),
        out_shape=jax.ShapeDtypeStruct((m, nt * tn), BF16),
        compiler_params=_cparams("parallel", "parallel"), name="proj",
    )(x, w3)


def _qk_proj_kernel(x_ref, w_ref, gain_ref, cos_ref, sin_ref, ones_ref, o_ref, *, rope):
    half = x_ref.shape[0] // 2
    for rr in range(2):
        rows = slice(rr * half, (rr + 1) * half)
        x = x_ref[rows, :]
        for cc in range(w_ref.shape[1] // TN):
            acc = _dot(x, w_ref[:, cc * TN:(cc + 1) * TN])
            for tt in range(TN // MXU_W):
                c0 = cc * TN + tt * MXU_W
                a = acc[:, tt * MXU_W:(tt + 1) * MXU_W]
                ssq = _dot((a * a).astype(BF16), ones_ref[...])
                y = a * lax.rsqrt(ssq * (1.0 / HEAD_DIM) + EPS) * gain_ref[:, c0:c0 + MXU_W]
                for hh in range(MXU_W // HEAD_DIM):
                    yh = y[:, hh * HEAD_DIM:(hh + 1) * HEAD_DIM]
                    if rope:
                        yh = (yh * cos_ref[rows, :]
                              + pltpu.roll(yh, HEAD_DIM // 2, 1) * sin_ref[rows, :])
                    o_ref[rows, c0 + hh * HEAD_DIM:c0 + (hh + 1) * HEAD_DIM] = yh.astype(o_ref.dtype)


def _qk_proj(x, w3, tile0, n_tiles, gain, cos2, sin2, rope):
    m, k = x.shape
    tn = w3.shape[2]
    tm = min(TM, m)
    seq_blocks = cos2.shape[0] // tm if rope else 1
    tab = pl.BlockSpec((tm, HEAD_DIM), lambda i, j: (i % seq_blocks, 0))
    head_of = np.arange(MXU_W) // HEAD_DIM
    ones = jnp.asarray(head_of[:, None] == head_of[None, :], BF16)
    return pl.pallas_call(
        functools.partial(_qk_proj_kernel, rope=rope), grid=(m // tm, n_tiles),
        in_specs=[pl.BlockSpec((tm, k), lambda i, j: (i, 0)),
                  pl.BlockSpec((None, k, tn), lambda i, j: (tile0 + j, 0, 0)),
                  pl.BlockSpec((1, tn), lambda i, j: (0, j)),
                  tab, tab,
                  pl.BlockSpec((MXU_W, MXU_W), lambda i, j: (0, 0))],
        out_specs=pl.BlockSpec((tm, tn), lambda i, j: (i, j)),
        out_shape=jax.ShapeDtypeStruct((m, n_tiles * tn), BF16),
        compiler_params=_cparams("parallel", "parallel"), name="qk_proj",
    )(x, w3, gain, cos2, sin2, ones)


def _out_res_kernel(*refs, n_in):
    xs, ws = refs[:n_in], refs[n_in:2 * n_in]
    res_ref, g_ref, o_ref = refs[2 * n_in:]
    acc = _dot(xs[0][...], ws[0][...])
    for x_ref, w_ref in zip(xs[1:], ws[1:]):
        acc = acc + _dot(x_ref[...], w_ref[...])
    o_ref[...] = res_ref[...] + g_ref[...] * acc


def _out_res(xs, w3, res, gate, rows_per_vec):
    m, n = res.shape
    nt, k, tn = w3.shape
    tm = min(TM, m)
    n_in = len(xs)
    kx = k // n_in
    assert all(x.shape[1] == kx for x in xs) and nt * tn == n

    def w_spec(part):
        return pl.BlockSpec((None, kx, tn), lambda i, j: (j, part, 0))

    in_specs = ([pl.BlockSpec((tm, kx), lambda i, j: (i, 0)) for _ in xs]
                + [w_spec(part) for part in range(n_in)]
                + [pl.BlockSpec((tm, tn), lambda i, j: (i, j)),
                   pl.BlockSpec((None, 1, tn), lambda i, j: ((i * tm) // rows_per_vec, 0, j))])
    return pl.pallas_call(
        functools.partial(_out_res_kernel, n_in=n_in), grid=(m // tm, nt),
        in_specs=in_specs,
        out_specs=pl.BlockSpec((tm, tn), lambda i, j: (i, j)),
        out_shape=jax.ShapeDtypeStruct((m, n), F32),
        compiler_params=_cparams("parallel", "parallel"), name="out_res",
    )(*xs, *([w3] * n_in), res, gate)


def _ffn_kernel(x_hbm, sh_ref, sc_ref, gate_ref, wg_ref, wu_ref, wd_ref, o_ref, h_ref, xbuf_ref, sem):
    i, j = pl.program_id(0), pl.program_id(1)
    tm = o_ref.shape[0]
    n_chunks = tm // FFN_X_ROWS

    def x_copy(ci, slot):
        src = x_hbm.at[pl.ds(i * tm + ci * FFN_X_ROWS, FFN_X_ROWS), :]
        return pltpu.make_async_copy(src, xbuf_ref.at[slot], sem.at[slot])

    @pl.when(j == 0)
    def _():
        x_copy(0, 0).start()
        for ci in range(n_chunks):
            slot = ci % 2
            if ci + 1 < n_chunks:
                x_copy(ci + 1, 1 - slot).start()
            x_copy(ci, slot).wait()
            rows = slice(ci * FFN_X_ROWS, (ci + 1) * FFN_X_ROWS)
            xc = xbuf_ref[slot]
            o_ref[rows, :] = xc
            h_ref[rows, :] = (_rms(xc) * (1.0 + sc_ref[...]) + sh_ref[...]).astype(BF16)

    h = h_ref[...]
    g = _dot(h, wg_ref[...])
    u = _dot(h, wu_ref[...])
    a = (_silu(g) * u).astype(BF16)
    o_ref[...] += gate_ref[...] * _dot(a, wd_ref[...])


def _ffn(x, shift, scale, gate, wg3, wu3, wd, rows_per_vec):
    m, d = x.shape
    n_tiles = wg3.shape[0]
    tm = min(TM, m)
    vec = pl.BlockSpec((None, 1, d), lambda i, j: ((i * tm) // rows_per_vec, 0, 0))
    return pl.pallas_call(
        _ffn_kernel, grid=(m // tm, n_tiles),
        in_specs=[pl.BlockSpec(memory_space=pl.ANY),
                  vec, vec, vec,
                  pl.BlockSpec((None, d, FFN_TH), lambda i, j: (j, 0, 0)),
                  pl.BlockSpec((None, d, FFN_TH), lambda i, j: (j, 0, 0)),
                  pl.BlockSpec((FFN_TH, d), lambda i, j: (j, 0))],
        out_specs=pl.BlockSpec((tm, d), lambda i, j: (i, 0)),
        out_shape=jax.ShapeDtypeStruct((m, d), F32),
        scratch_shapes=[pltpu.VMEM((tm, d), BF16), pltpu.VMEM((2, FFN_X_ROWS, d), F32),
                        pltpu.SemaphoreType.DMA((2,))],
        compiler_params=pltpu.CompilerParams(dimension_semantics=("parallel", "arbitrary"),
                                             vmem_limit_bytes=FFN_VMEM_LIMIT), name="ffn",
    )(x, shift, scale, gate, wg3, wu3, wd)


def _gla_scan_tile(q_ref, k_ref, v_ref, g_ref, w2_ref, b2_ref, s_ref, emit, reverse):
    dk = q_ref.shape[1] // GLA_HEADS
    dv = v_ref.shape[1] // GLA_HEADS
    row = lax.broadcasted_iota(jnp.int32, (GLA_BLOCK, GLA_BLOCK), 0)
    col = lax.broadcasted_iota(jnp.int32, (GLA_BLOCK, GLA_BLOCK), 1)
    if reverse:
        tri = (col >= row).astype(BF16)
        mask = col > row
        last_row = 0
    else:
        tri = (col <= row).astype(BF16)
        mask = col <= row
        last_row = GLA_BLOCK - 1
    n_steps = q_ref.shape[0] // GLA_BLOCK
    for step in range(n_steps):
        c = n_steps - 1 - step if reverse else step
        rows = slice(c * GLA_BLOCK, (c + 1) * GLA_BLOCK)
        q = q_ref[rows, :].astype(F32) * (dk ** -0.5)
        k = k_ref[rows, :].astype(F32)
        x = _dot(g_ref[rows, :], w2_ref[...]) + b2_ref[...]
        log_a = (jnp.minimum(x, 0.0) - jnp.log(1.0 + jnp.exp(-jnp.abs(x)))) * (1.0 / GLA_TAU)
        hi = log_a.astype(BF16)
        lo = (log_a - hi.astype(F32)).astype(BF16)
        b = _dot(tri, hi) + _dot(tri, lo)
        mid = b[GLA_BLOCK // 2:GLA_BLOCK // 2 + 1, :]
        b_last = b[last_row:last_row + 1, :]
        qe = q * jnp.exp(b - mid)
        ke = k * jnp.exp(mid - b)
        q_in = (qe * jnp.exp(mid)).astype(BF16)
        k_out = (ke * jnp.exp(b_last - mid)).astype(BF16)
        decay = jnp.exp(b_last)
        qe, ke = qe.astype(BF16), ke.astype(BF16)
        for h in range(GLA_HEADS):
            ks = slice(h * dk, (h + 1) * dk)
            v = v_ref[rows, h * dv:(h + 1) * dv]
            att = jnp.where(mask, _dot_nt(qe[:, ks], ke[:, ks]), 0.0).astype(BF16)
            s = s_ref[h]
            emit(rows, h, _dot_nt(q_in[:, ks], s.astype(BF16)) + _dot(att, v))
            s_ref[h] = s * decay[:, ks] + _dot_tn(v, k_out[:, ks])


def _gla_rev_kernel(qc_ref, kc_ref, vc_ref, gc_ref, ql_ref, kl_ref, vl_ref, gl_ref, w2_ref, b2_ref,
                    oc_ref, ol_ref, s_ref):
    t = pl.program_id(1)
    dv = vc_ref.shape[1] // GLA_HEADS

    def run(q_ref, k_ref, v_ref, g_ref, o_ref):
        def emit(rows, h, o):
            o_ref[rows, h * dv:(h + 1) * dv] = o
        _gla_scan_tile(q_ref, k_ref, v_ref, g_ref, w2_ref, b2_ref, s_ref, emit, reverse=True)

    @pl.when(t == 0)
    def _():
        s_ref[...] = jnp.zeros_like(s_ref)
        run(qc_ref, kc_ref, vc_ref, gc_ref, oc_ref)

    @pl.when(t > 0)
    def _():
        run(ql_ref, kl_ref, vl_ref, gl_ref, ol_ref)


def _gla_fwd_kernel(qc_ref, kc_ref, vc_ref, rc_ref, gc_ref, pc_ref, ql_ref, kl_ref, vl_ref, rl_ref, gl_ref, pl_ref,
                    w2_ref, b2_ref, gain_ref, oc_ref, ol_ref, s_ref):
    t = pl.program_id(1)
    dv = vc_ref.shape[1] // GLA_HEADS

    def run(q_ref, k_ref, v_ref, r_ref, g_ref, p_ref, o_ref):
        def emit(rows, h, o):
            cols = slice(h * dv, (h + 1) * dv)
            total = o + p_ref[rows, cols]
            r = r_ref[rows, cols].astype(F32)
            o_ref[rows, cols] = (_rms(total) * gain_ref[...] * _silu(r)).astype(o_ref.dtype)
        _gla_scan_tile(q_ref, k_ref, v_ref, g_ref, w2_ref, b2_ref, s_ref, emit, reverse=False)

    @pl.when(t == 0)
    def _():
        s_ref[...] = jnp.zeros_like(s_ref)
        run(qc_ref, kc_ref, vc_ref, rc_ref, gc_ref, pc_ref, oc_ref)

    @pl.when(t > 0)
    def _():
        run(ql_ref, kl_ref, vl_ref, rl_ref, gl_ref, pl_ref, ol_ref)


def _gla(p_l, p_c, g_l, g_c, w2f, b2f, w2b, b2b, gain, batch, dk, dv):
    seq, ctx_len = p_l.shape[0] // batch, p_c.shape[0] // batch
    assert ctx_len == GLA_TILE and seq % GLA_TILE == 0
    n_lat = seq // GLA_TILE
    key_w, val_w = GLA_HEADS * dk, GLA_HEADS * dv
    assert val_w == 2 * key_w

    def lat_fwd(b, t):
        return b * n_lat + jnp.maximum(t - 1, 0)

    def lat_rev(b, t):
        return b * n_lat + n_lat - jnp.maximum(t, 1)

    def tok_specs(row_blk, with_r):
        specs = [pl.BlockSpec((GLA_TILE, key_w), lambda b, t: (row_blk(b, t), 0)),
                 pl.BlockSpec((GLA_TILE, key_w), lambda b, t: (row_blk(b, t), 1)),
                 pl.BlockSpec((GLA_TILE, val_w), lambda b, t: (row_blk(b, t), 1))]
        if with_r:
            specs.append(pl.BlockSpec((GLA_TILE, val_w), lambda b, t: (row_blk(b, t), 2)))
        return specs + [pl.BlockSpec((GLA_TILE, LANE), lambda b, t: (row_blk(b, t), 0))]

    def out_spec(row_blk):
        return pl.BlockSpec((GLA_TILE, val_w), lambda b, t: (row_blk(b, t), 0))

    def ctx_blk(b, t):
        return b

    w2_spec = pl.BlockSpec((LANE, key_w), lambda b, t: (0, 0))
    b2_spec = pl.BlockSpec((1, key_w), lambda b, t: (0, 0))
    state = pltpu.VMEM((GLA_HEADS, dv, dk), F32)
    grid = (batch, 1 + n_lat)
    rev_c, rev_l = pl.pallas_call(
        _gla_rev_kernel, grid=grid,
        in_specs=tok_specs(ctx_blk, False) + tok_specs(lat_rev, False) + [w2_spec, b2_spec],
        out_specs=[out_spec(ctx_blk), out_spec(lat_rev)],
        out_shape=[jax.ShapeDtypeStruct((batch * ctx_len, val_w), F32),
                   jax.ShapeDtypeStruct((batch * seq, val_w), F32)],
        scratch_shapes=[state],
        compiler_params=_cparams("parallel", "arbitrary"), name="gla_rev",
    )(p_c, p_c, p_c, g_c, p_l, p_l, p_l, g_l, w2b, b2b)
    return pl.pallas_call(
        _gla_fwd_kernel, grid=grid,
        in_specs=tok_specs(ctx_blk, True) + [out_spec(ctx_blk)] + tok_specs(lat_fwd, True) + [out_spec(lat_fwd)]
        + [w2_spec, b2_spec, pl.BlockSpec((1, dv), lambda b, t: (0, 0))],
        out_specs=[out_spec(ctx_blk), out_spec(lat_fwd)],
        out_shape=[jax.ShapeDtypeStruct((batch * ctx_len, val_w), BF16),
                   jax.ShapeDtypeStruct((batch * seq, val_w), BF16)],
        scratch_shapes=[state],
        compiler_params=_cparams("parallel", "arbitrary"), name="gla_fwd",
    )(p_c, p_c, p_c, p_c, g_c, rev_c, p_l, p_l, p_l, p_l, g_l, rev_l, w2f, b2f, gain)


def _sgu_kernel(u_ref, s_ref, ws_ref, bs_ref, gain_ref, o_ref):
    gw = u_ref.shape[1] // SGU_GROUPS
    for ci in range(u_ref.shape[0] // SGU_CHUNK):
        rows = slice(ci * SGU_CHUNK, (ci + 1) * SGU_CHUNK)
        for g in range(SGU_GROUPS):
            cols = slice(g * gw, (g + 1) * gw)
            vg = _rms(_gelu_tanh(s_ref[rows, cols].astype(F32))) * gain_ref[:, cols]
            mixed = _dot(ws_ref[g], vg.astype(BF16)) + bs_ref[:, g:g + 1]
            o_ref[rows, cols] = (_gelu_tanh(u_ref[rows, cols].astype(F32)) * mixed).astype(o_ref.dtype)


def _sgu(p, w_s, b_s_t, gain, width, u_blk, s_blk):
    m = p.shape[0]
    rows = 2 * SGU_CHUNK
    return pl.pallas_call(
        _sgu_kernel, grid=(m // rows,),
        in_specs=[pl.BlockSpec((rows, width), lambda i: (i, u_blk)),
                  pl.BlockSpec((rows, width), lambda i: (i, s_blk)),
                  pl.BlockSpec(w_s.shape, lambda i: (0, 0, 0)),
                  pl.BlockSpec(b_s_t.shape, lambda i: (0, 0)),
                  pl.BlockSpec((1, width), lambda i: (0, 0))],
        out_specs=pl.BlockSpec((rows, width), lambda i: (i, 0)),
        out_shape=jax.ShapeDtypeStruct((m, width), BF16),
        compiler_params=_cparams("parallel"), name="sgu",
    )(p, p, w_s, b_s_t, gain)


def _attn_kernel(sink_ref, bias_ref, q_ref, k_ref, v_ref, kc_ref, vc_ref, o_ref):
    kv, ng = pl.program_id(1), pl.program_id(2)
    seq = k_ref.shape[0]
    nb = seq // ATT_BLOCK
    band = 3 * ATT_BLOCK
    sink = jnp.concatenate([jnp.full((ATT_BLOCK, HEAD_DIM), sink_ref[kv * ATT_GROUP + g] * LOG2E, F32)
                            for g in range(ATT_GROUP)], axis=0)
    k_ctx, v_ctx = kc_ref[...], vc_ref[...]
    ones = jnp.ones((band + k_ctx.shape[0], HEAD_DIM), BF16)
    for jj in range(ATT_QB):
        n = ng * ATT_QB + jj
        start = pl.multiple_of(jnp.clip((n - 1) * ATT_BLOCK, 0, seq - band), ATT_BLOCK)
        if jj == 0:
            bias = bias_ref[jnp.where(n == 0, 1, 0)]
        elif jj == ATT_QB - 1:
            bias = bias_ref[jnp.where(n == nb - 1, 2, 0)]
        else:
            bias = bias_ref[0]
        q = q_ref[jj * ATT_BLOCK:(jj + 1) * ATT_BLOCK, :]
        qs = jnp.concatenate([q[:, g * HEAD_DIM:(g + 1) * HEAD_DIM] for g in range(ATT_GROUP)], axis=0)
        keys = jnp.concatenate([k_ref[pl.ds(start, band), :], k_ctx], axis=0)
        vals = jnp.concatenate([v_ref[pl.ds(start, band), :], v_ctx], axis=0)
        s = _dot_nt(qs, keys)
        s = jnp.concatenate([s[:, :band] + bias, s[:, band:]], axis=1)
        m = jnp.maximum(sink, jnp.max(s, axis=-1, keepdims=True))
        p = jnp.exp2(s - jnp.concatenate([m] * (s.shape[1] // HEAD_DIM), axis=1)).astype(BF16)
        ov = _dot(p, jnp.concatenate([vals, ones], axis=1))
        o = ov[:, :HEAD_DIM] / (ov[:, HEAD_DIM:] + jnp.exp2(sink - m))
        o_ref[jj * ATT_BLOCK:(jj + 1) * ATT_BLOCK, :] = jnp.concatenate(
            [o[g * ATT_BLOCK:(g + 1) * ATT_BLOCK] for g in range(ATT_GROUP)], axis=1).astype(o_ref.dtype)


def _band_bias():
    t = (np.arange(ATT_GROUP * ATT_BLOCK) % ATT_BLOCK)[:, None]
    i = np.arange(3 * ATT_BLOCK)[None, :]
    shifts = (-ATT_BLOCK, 0, -2 * ATT_BLOCK)
    return jnp.asarray(np.stack([np.where(np.abs(i + sh - t) <= WINDOW, 0.0, -np.inf) for sh in shifts]), F32)


def _attention(qk, v, k_c, v_c, sink, batch, n_q_heads):
    m = qk.shape[0]
    seq, ctx_len = m // batch, k_c.shape[0] // batch
    kv_heads = n_q_heads // ATT_GROUP
    ng = seq // (ATT_BLOCK * ATT_QB)
    assert ng >= 2
    qw = ATT_GROUP * HEAD_DIM
    bias = _band_bias()
    return pl.pallas_call(
        _attn_kernel, grid=(batch, kv_heads, ng),
        in_specs=[pl.BlockSpec(memory_space=pltpu.SMEM),
                  pl.BlockSpec(bias.shape, lambda b, h, n: (0, 0, 0)),
                  pl.BlockSpec((ATT_QB * ATT_BLOCK, qw), lambda b, h, n: (b * ng + n, h)),
                  pl.BlockSpec((seq, HEAD_DIM), lambda b, h, n: (b, n_q_heads + h)),
                  pl.BlockSpec((seq, HEAD_DIM), lambda b, h, n: (b, h)),
                  pl.BlockSpec((ctx_len, HEAD_DIM), lambda b, h, n: (b, h)),
                  pl.BlockSpec((ctx_len, HEAD_DIM), lambda b, h, n: (b, h))],
        out_specs=pl.BlockSpec((ATT_QB * ATT_BLOCK, qw), lambda b, h, n: (b * ng + n, h)),
        out_shape=jax.ShapeDtypeStruct((m, n_q_heads * HEAD_DIM), BF16),
        compiler_params=_cparams("parallel", "parallel", "arbitrary"), name="attention",
    )(sink, bias, qk, qk, v, k_c, v_c)


def _rope_tables(seq):
    rows = seq // GRID_W
    row = jnp.repeat(jnp.arange(rows, dtype=F32), GRID_W)
    col = jnp.tile(jnp.arange(GRID_W, dtype=F32), rows)
    n_freq = HEAD_DIM // 4
    inv_freq = ROPE_THETA ** (-jnp.arange(n_freq, dtype=F32) / n_freq)
    ang = jnp.concatenate([row[:, None] * inv_freq, col[:, None] * inv_freq], axis=-1)
    cos, sin = jnp.cos(ang), jnp.sin(ang)
    return jnp.concatenate([cos, cos], axis=-1), jnp.concatenate([-sin, sin], axis=-1)


def kernel(x, c, ctx, c_ctx, ada_w, ada_b, ffn_w_gate, ffn_w_up, ffn_w_down, even_w_in, even_gate_w2_fwd, even_gate_b_fwd, even_gate_w2_bwd, even_gate_b_bwd, even_gla_norm_gain, even_sgu_norm_gain, even_sgu_w_s, even_sgu_b_s, even_w_out, odd_w_in, odd_q_norm_gain, odd_k_norm_gain, odd_sink, odd_w_out):
    batch, seq, d = x.shape
    ctx_len = ctx.shape[1]
    depth = ada_w.shape[0]
    assert depth == 2 and batch + 1 <= 8
    xl = x.reshape(batch * seq, d)
    xc = ctx.reshape(batch * ctx_len, d)

    cc = jnp.concatenate([c, c_ctx[None], jnp.zeros((8 - batch - 1, d), F32)], axis=0)
    mods = _ada(cc, ada_w, ada_b)

    def mod_l(layer, which):
        return mods[layer, :batch, which * d:(which + 1) * d].reshape(batch, 1, d)

    def mod_c(layer, which):
        return mods[layer, batch:batch + 1, which * d:(which + 1) * d].reshape(1, 1, d)

    all_ctx = batch * ctx_len

    dv = even_gla_norm_gain.shape[1]
    key_w = even_gate_w2_fwd.shape[2]
    dk = key_w // GLA_HEADS
    val_w = GLA_HEADS * dv
    sgu_w = even_sgu_norm_gain.shape[1]
    gate0 = 2 * key_w + 2 * val_w
    gate1 = gate0 + 2 * GLA_GATE_RANK
    w_qkvr = _cast_col_tiles(even_w_in, 0, 0, gate0, TN)
    w_us = _col_tiles(even_w_in[0, :, gate1:].astype(BF16), TN)
    w_gate = jnp.pad(even_w_in[0, :, gate0:gate1], ((0, 0), (0, LANE - 2 * GLA_GATE_RANK))).astype(BF16)[None]
    pad_rows = LANE - GLA_GATE_RANK
    w2f = jnp.pad(even_gate_w2_fwd[0], ((0, pad_rows), (0, 0))).astype(BF16)
    w2b = jnp.pad(even_gate_w2_bwd[0], ((GLA_GATE_RANK, pad_rows - GLA_GATE_RANK), (0, 0))).astype(BF16)
    b2f = even_gate_b_fwd[0].reshape(1, key_w)
    b2b = even_gate_b_bwd[0].reshape(1, key_w)

    h_l = _modulate(xl, mod_l(0, 0), mod_l(0, 1), seq)
    h_c = _modulate(xc, mod_c(0, 0), mod_c(0, 1), all_ctx)
    p_l, p_c = _proj(h_l, w_qkvr), _proj(h_c, w_qkvr)
    us_l, us_c = _proj(h_l, w_us), _proj(h_c, w_us)
    g_l, g_c = _proj(h_l, w_gate), _proj(h_c, w_gate)
    gla_c, gla_l = _gla(p_l, p_c, g_l, g_c, w2f, b2f, w2b, b2b, even_gla_norm_gain[0].reshape(1, dv),
                        batch, dk, dv)
    w_s = even_sgu_w_s[0].astype(BF16)
    b_s_t = even_sgu_b_s[0].T
    sgu_gain = even_sgu_norm_gain[0].reshape(1, sgu_w)
    sgu_l = _sgu(us_l, w_s, b_s_t, sgu_gain, sgu_w, 0, 1)
    sgu_c = _sgu(us_c, w_s, b_s_t, sgu_gain, sgu_w, 0, 1)
    assert val_w == sgu_w
    w_out = _cast_col_tiles(even_w_out, 0, 0, d, TN)
    x_l = _out_res([gla_l, sgu_l], w_out, xl, mod_l(0, 2), seq)
    x_c = _out_res([gla_c, sgu_c], w_out, xc, mod_c(0, 2), all_ctx)

    hidden = ffn_w_gate.shape[2]
    wg = _cast_col_tiles(ffn_w_gate, 0, 0, hidden, FFN_TH)
    wu = _cast_col_tiles(ffn_w_up, 0, 0, hidden, FFN_TH)
    wd = _cast_rows(ffn_w_down, 0, FFN_TH)
    x_l = _ffn(x_l, mod_l(0, 3), mod_l(0, 4), mod_l(0, 5), wg, wu, wd, seq)
    x_c = _ffn(x_c, mod_c(0, 3), mod_c(0, 4), mod_c(0, 5), wg, wu, wd, all_ctx)

    h_l = _modulate(x_l, mod_l(1, 0), mod_l(1, 1), seq)
    h_c = _modulate(x_c, mod_c(1, 0), mod_c(1, 1), all_ctx)

    n_heads = odd_sink.shape[1]
    q_w = n_heads * HEAD_DIM
    kv_w = q_w // ATT_GROUP
    qk_tile = 2 * TN
    assert kv_w == qk_tile
    w_qk = _cast_col_tiles(odd_w_in, 0, 0, q_w + kv_w, qk_tile)
    w_v = _cast_col_tiles(odd_w_in, 0, q_w + kv_w, kv_w, TN)
    k_gain = jnp.tile(odd_k_norm_gain[0], kv_w // HEAD_DIM)
    qk_gain = jnp.concatenate([jnp.tile(odd_q_norm_gain[0] * (HEAD_DIM ** -0.5 * LOG2E), n_heads),
                               k_gain]).reshape(1, -1)
    cos2, sin2 = _rope_tables(seq)
    qk_l = _qk_proj(h_l, w_qk, 0, w_qk.shape[0], qk_gain, cos2, sin2, rope=True)
    v_l = _proj(h_l, w_v)
    k_c = _qk_proj(h_c, w_qk, q_w // qk_tile, 1, k_gain.reshape(1, -1), cos2, sin2, rope=False)
    v_c = _proj(h_c, w_v)
    att = _attention(qk_l, v_l, k_c, v_c, odd_sink[0], batch, n_heads)
    x_l = _out_res([att], _cast_col_tiles(odd_w_out, 0, 0, d, TN), x_l, mod_l(1, 2), seq)

    wg = _cast_col_tiles(ffn_w_gate, 1, 0, hidden, FFN_TH)
    wu = _cast_col_tiles(ffn_w_up, 1, 0, hidden, FFN_TH)
    wd = _cast_rows(ffn_w_down, 1, FFN_TH)
    return _ffn(x_l, mod_l(1, 3), mod_l(1, 4), mod_l(1, 5), wg, wu, wd, seq).reshape(batch, seq, d)
```

```python
import functools

import numpy as np
import jax
import jax.numpy as jnp
from jax import lax
from jax.experimental import pallas as pl
from jax.experimental.pallas import tpu as pltpu

F32 = jnp.float32
BF16 = jnp.bfloat16

EPS = 1e-6
LOG2E = float(np.log2(np.e))
N_MOD = 6
GRID_W = 64
ROPE_THETA = 10000.0

GLA_HEADS = 4
GLA_TAU = 16.0
GLA_GATE_RANK = 16
GLA_BLOCK = 64
GLA_TILE = 256
SGU_GROUPS = 8
SGU_CHUNK = 128
HEAD_DIM = 128
ATT_GROUP = 4
WINDOW = 128
ATT_BLOCK = 128
ATT_QB = 8

LANE = 128
MXU_W = 256
VMEM_LIMIT = 56 * 1024 * 1024
FFN_VMEM_LIMIT = 60 * 1024 * 1024

TM = 1024
TN = 512
FFN_TH = 256
FFN_X_ROWS = 128
FFN_X_GROUP = 16
ROWS_EW = 256


def _cparams(*sem):
    return pltpu.CompilerParams(dimension_semantics=sem, vmem_limit_bytes=VMEM_LIMIT)


def _dot(a, b):
    return jnp.dot(a, b, preferred_element_type=F32)


def _dot_nt(a, b):
    return lax.dot_general(a, b, (((1,), (1,)), ((), ())), preferred_element_type=F32)


def _dot_tn(a, b):
    return lax.dot_general(a, b, (((0,), (0,)), ((), ())), preferred_element_type=F32)


def _silu(x):
    return x * jax.nn.sigmoid(x)


def _gelu_tanh(x):
    c = np.float32(np.sqrt(2.0 / np.pi))
    return x * (0.5 * (1.0 + jnp.tanh(c * (x + 0.044715 * (x * x * x)))))


def _rms(x):
    return x * lax.rsqrt(jnp.mean(x * x, axis=-1, keepdims=True) + EPS)


def _ada_kernel(c_ref, w_ref, b_ref, o_ref):
    a = _silu(c_ref[...]).astype(BF16)
    o_ref[...] = _dot(a, w_ref[...].astype(BF16)) + b_ref[...]


def _ada(cc, ada_w, ada_b):
    depth, d, n = ada_w.shape
    rows = cc.shape[0]
    return pl.pallas_call(
        _ada_kernel,
        grid=(depth, n // TN),
        in_specs=[pl.BlockSpec((rows, d), lambda l, j: (0, 0)),
                  pl.BlockSpec((None, d, TN), lambda l, j: (l, 0, j)),
                  pl.BlockSpec((None, 1, TN), lambda l, j: (l, 0, j))],
        out_specs=pl.BlockSpec((None, rows, TN), lambda l, j: (l, 0, j)),
        out_shape=jax.ShapeDtypeStruct((depth, rows, n), F32),
        compiler_params=_cparams("parallel", "parallel"),
        name="ada",
    )(cc, ada_w, ada_b.reshape(depth, 1, n))


def _modulate_kernel(x_ref, sh_ref, sc_ref, h_ref):
    h_ref[...] = (_rms(x_ref[...]) * (1.0 + sc_ref[...]) + sh_ref[...]).astype(h_ref.dtype)


def _row_spec(d):
    return pl.BlockSpec((ROWS_EW, d), lambda i: (i, 0))


def _vec_spec(d, rows_per_vec):
    return pl.BlockSpec((None, 1, d), lambda i: ((i * ROWS_EW) // rows_per_vec, 0, 0))


def _modulate(x, shift, scale, rows_per_vec):
    m, d = x.shape
    return pl.pallas_call(
        _modulate_kernel, grid=(m // ROWS_EW,),
        in_specs=[_row_spec(d), _vec_spec(d, rows_per_vec), _vec_spec(d, rows_per_vec)],
        out_specs=_row_spec(d),
        out_shape=jax.ShapeDtypeStruct((m, d), BF16),
        compiler_params=_cparams("parallel"), name="modulate",
    )(x, shift, scale)


def _cast_kernel(w_ref, o_ref):
    o_ref[...] = w_ref[...].astype(o_ref.dtype)


def _cast_col_tiles(w, layer, col0, n_cols, tile):
    k = w.shape[1]
    assert col0 % tile == 0 and n_cols % tile == 0
    t0 = col0 // tile
    return pl.pallas_call(
        _cast_kernel, grid=(n_cols // tile,),
        in_specs=[pl.BlockSpec((None, k, tile), lambda j: (layer, 0, t0 + j))],
        out_specs=pl.BlockSpec((None, k, tile), lambda j: (j, 0, 0)),
        out_shape=jax.ShapeDtypeStruct((n_cols // tile, k, tile), BF16),
        compiler_params=_cparams("parallel"), name="cast_cols",
    )(w)


def _cast_rows(w, layer, tile):
    k, n = w.shape[1:]
    return pl.pallas_call(
        _cast_kernel, grid=(k // tile,),
        in_specs=[pl.BlockSpec((None, tile, n), lambda j: (layer, j, 0))],
        out_specs=pl.BlockSpec((tile, n), lambda j: (j, 0)),
        out_shape=jax.ShapeDtypeStruct((k, n), BF16),
        compiler_params=_cparams("parallel"), name="cast_rows",
    )(w)


def _proj_kernel(x_ref, w_ref, o_ref):
    o_ref[...] = _dot(x_ref[...], w_ref[...]).astype(o_ref.dtype)


def _proj(x, w3):
    m, k = x.shape
    nt, _, tn = w3.shape
    tm = min(TM, m)
    return pl.pallas_call(
        _proj_kernel, grid=(m // tm, nt),
        in_specs=[pl.BlockSpec((tm, k), lambda i, j: (i, 0)),
                  pl.BlockSpec((None, k, tn), lambda i, j: (j, 0, 0))],
        out_specs=pl.BlockSpec((tm, tn), lambda i, j: (i, j)),
        out_shape=jax.ShapeDtypeStruct((m, nt * tn), BF16),
        compiler_params=_cparams("parallel", "parallel"), name="proj",
    )(x, w3)


def _qk_proj_kernel(x_ref, w_ref, gain_ref, cos_ref, sin_ref, ones_ref, o_ref, *, rope):
    half = x_ref.shape[0] // 2
    for rr in range(2):
        rows = slice(rr * half, (rr + 1) * half)
        x = x_ref[rows, :]
        for cc in range(w_ref.shape[1] // TN):
            acc = _dot(x, w_ref[:, cc * TN:(cc + 1) * TN])
            for tt in range(TN // MXU_W):
                c0 = cc * TN + tt * MXU_W
                a = acc[:, tt * MXU_W:(tt + 1) * MXU_W]
                ssq = _dot((a * a).astype(BF16), ones_ref[...])
                y = a * lax.rsqrt(ssq * (1.0 / HEAD_DIM) + EPS) * gain_ref[:, c0:c0 + MXU_W]
                for hh in range(MXU_W // HEAD_DIM):
                    yh = y[:, hh * HEAD_DIM:(hh + 1) * HEAD_DIM]
                    if rope:
                        yh = (yh * cos_ref[rows, :]
                              + pltpu.roll(yh, HEAD_DIM // 2, 1) * sin_ref[rows, :])
                    o_ref[rows, c0 + hh * HEAD_DIM:c0 + (hh + 1) * HEAD_DIM] = yh.astype(o_ref.dtype)


def _qk_proj(x, w3, tile0, n_tiles, gain, cos2, sin2, rope):
    m, k = x.shape
    tn = w3.shape[2]
    tm = min(TM, m)
    seq_blocks = cos2.shape[0] // tm if rope else 1
    tab = pl.BlockSpec((tm, HEAD_DIM), lambda i, j: (i % seq_blocks, 0))
    head_of = np.arange(MXU_W) // HEAD_DIM
    ones = jnp.asarray(head_of[:, None] == head_of[None, :], BF16)
    return pl.pallas_call(
        functools.partial(_qk_proj_kernel, rope=rope), grid=(m // tm, n_tiles),
        in_specs=[pl.BlockSpec((tm, k), lambda i, j: (i, 0)),
                  pl.BlockSpec((None, k, tn), lambda i, j: (tile0 + j, 0, 0)),
                  pl.BlockSpec((1, tn), lambda i, j: (0, j)),
                  tab, tab,
                  pl.BlockSpec((MXU_W, MXU_W), lambda i, j: (0, 0))],
        out_specs=pl.BlockSpec((tm, tn), lambda i, j: (i, j)),
        out_shape=jax.ShapeDtypeStruct((m, n_tiles * tn), BF16),
        compiler_params=_cparams("parallel", "parallel"), name="qk_proj",
    )(x, w3, gain, cos2, sin2, ones)


def _out_res_kernel(*refs, n_in):
    xs, ws = refs[:n_in], refs[n_in:2 * n_in]
    res_ref, g_ref, o_ref = refs[2 * n_in:]
    acc = _dot(xs[0][...], ws[0][...])
    for x_ref, w_ref in zip(xs[1:], ws[1:]):
        acc = acc + _dot(x_ref[...], w_ref[...])
    o_ref[...] = res_ref[...] + g_ref[...] * acc


def _out_res(xs, w3, res, gate, rows_per_vec):
    m, n = res.shape
    nt, k, tn = w3.shape
    tm = min(TM, m)
    n_in = len(xs)
    kx = k // n_in
    assert all(x.shape[1] == kx for x in xs) and nt * tn == n

    def w_spec(part):
        return pl.BlockSpec((None, kx, tn), lambda i, j: (j, part, 0))

    in_specs = ([pl.BlockSpec((tm, kx), lambda i, j: (i, 0)) for _ in xs]
                + [w_spec(part) for part in range(n_in)]
                + [pl.BlockSpec((tm, tn), lambda i, j: (i, j)),
                   pl.BlockSpec((None, 1, tn), lambda i, j: ((i * tm) // rows_per_vec, 0, j))])
    return pl.pallas_call(
        functools.partial(_out_res_kernel, n_in=n_in), grid=(m // tm, nt),
        in_specs=in_specs,
        out_specs=pl.BlockSpec((tm, tn), lambda i, j: (i, j)),
        out_shape=jax.ShapeDtypeStruct((m, n), F32),
        compiler_params=_cparams("parallel", "parallel"), name="out_res",
    )(*xs, *([w3] * n_in), res, gate)


def _ffn_kernel(x_hbm, sh_ref, sc_ref, gate_ref, wg_ref, wu_ref, wd_ref, o_ref, h_ref, xbuf_ref, sem):
    i, j = pl.program_id(0), pl.program_id(1)
    tm = o_ref.shape[0]
    n_chunks = tm // FFN_X_ROWS

    def x_copy(ci, slot):
        src = x_hbm.at[pl.ds(i * tm + ci * FFN_X_ROWS, FFN_X_ROWS), :]
        return pltpu.make_async_copy(src, xbuf_ref.at[slot], sem.at[slot])

    @pl.when(j == 0)
    def _():
        x_copy(0, 0).start()
        for ci in range(n_chunks):
            slot = ci % 2
            if ci + 1 < n_chunks:
                x_copy(ci + 1, 1 - slot).start()
            x_copy(ci, slot).wait()

            def group(gi, carry, ci=ci, slot=slot):
                src = pl.ds(pl.multiple_of(gi * FFN_X_GROUP, FFN_X_GROUP), FFN_X_GROUP)
                dst = pl.ds(pl.multiple_of(ci * FFN_X_ROWS + gi * FFN_X_GROUP, FFN_X_GROUP), FFN_X_GROUP)
                xg = xbuf_ref[slot, src, :]
                o_ref[dst, :] = xg
                h_ref[dst, :] = (_rms(xg) * (1.0 + sc_ref[...]) + sh_ref[...]).astype(BF16)
                return carry
            lax.fori_loop(0, FFN_X_ROWS // FFN_X_GROUP, group, 0, unroll=2)

    h = h_ref[...]
    g = _dot(h, wg_ref[...])
    u = _dot(h, wu_ref[...])
    a = (_silu(g) * u).astype(BF16)
    o_ref[...] += gate_ref[...] * _dot(a, wd_ref[...])


def _ffn(x, shift, scale, gate, wg3, wu3, wd, rows_per_vec):
    m, d = x.shape
    n_tiles = wg3.shape[0]
    tm = min(TM, m)
    vec = pl.BlockSpec((None, 1, d), lambda i, j: ((i * tm) // rows_per_vec, 0, 0))
    return pl.pallas_call(
        _ffn_kernel, grid=(m // tm, n_tiles),
        in_specs=[pl.BlockSpec(memory_space=pl.ANY),
                  vec, vec, vec,
                  pl.BlockSpec((None, d, FFN_TH), lambda i, j: (j, 0, 0)),
                  pl.BlockSpec((None, d, FFN_TH), lambda i, j: (j, 0, 0)),
                  pl.BlockSpec((FFN_TH, d), lambda i, j: (j, 0))],
        out_specs=pl.BlockSpec((tm, d), lambda i, j: (i, 0)),
        out_shape=jax.ShapeDtypeStruct((m, d), F32),
        scratch_shapes=[pltpu.VMEM((tm, d), BF16), pltpu.VMEM((2, FFN_X_ROWS, d), F32),
                        pltpu.SemaphoreType.DMA((2,))],
        compiler_params=pltpu.CompilerParams(dimension_semantics=("parallel", "arbitrary"),
                                             vmem_limit_bytes=FFN_VMEM_LIMIT), name="ffn",
    )(x, shift, scale, gate, wg3, wu3, wd)


def _gla_scan_tile(q_ref, k_ref, v_ref, g_ref, w2_ref, b2_ref, s_ref, emit, reverse):
    dk = q_ref.shape[1] // GLA_HEADS
    dv = v_ref.shape[1] // GLA_HEADS
    row = lax.broadcasted_iota(jnp.int32, (GLA_BLOCK, GLA_BLOCK), 0)
    col = lax.broadcasted_iota(jnp.int32, (GLA_BLOCK, GLA_BLOCK), 1)
    if reverse:
        tri = (col >= row).astype(BF16)
        mask = col > row
        last_row = 0
    else:
        tri = (col <= row).astype(BF16)
        mask = col <= row
        last_row = GLA_BLOCK - 1
    n_steps = q_ref.shape[0] // GLA_BLOCK
    order = [n_steps - 1 - st if reverse else st for st in range(n_steps)]

    parts = []
    for c in order:
        rows = slice(c * GLA_BLOCK, (c + 1) * GLA_BLOCK)
        q = q_ref[rows, :].astype(F32) * (dk ** -0.5)
        k = k_ref[rows, :].astype(F32)
        x = _dot(g_ref[rows, :], w2_ref[...]) + b2_ref[...]
        log_a = (jnp.minimum(x, 0.0) - jnp.log(1.0 + jnp.exp(-jnp.abs(x)))) * (1.0 / GLA_TAU)
        hi = log_a.astype(BF16)
        lo = (log_a - hi.astype(F32)).astype(BF16)
        b = _dot(tri, hi) + _dot(tri, lo)
        mid = b[GLA_BLOCK // 2:GLA_BLOCK // 2 + 1, :]
        b_last = b[last_row:last_row + 1, :]
        qe = q * jnp.exp(b - mid)
        ke = k * jnp.exp(mid - b)
        q_in = (qe * jnp.exp(mid)).astype(BF16)
        k_out = (ke * jnp.exp(b_last - mid)).astype(BF16)
        decay = jnp.exp(b_last)
        qe, ke = qe.astype(BF16), ke.astype(BF16)
        intra, kv = [], []
        for h in range(GLA_HEADS):
            ks = slice(h * dk, (h + 1) * dk)
            v = v_ref[rows, h * dv:(h + 1) * dv]
            att = jnp.where(mask, _dot_nt(qe[:, ks], ke[:, ks]), 0.0).astype(BF16)
            intra.append(_dot(att, v))
            kv.append(_dot_tn(v, k_out[:, ks]))
        parts.append((rows, q_in, decay, intra, kv))

    for h in range(GLA_HEADS):
        ks = slice(h * dk, (h + 1) * dk)
        s = s_ref[h]
        for rows, q_in, decay, intra, kv in parts:
            emit(rows, h, intra[h] + _dot_nt(q_in[:, ks], s.astype(BF16)))
            s = s * decay[:, ks] + kv[h]
        s_ref[h] = s


def _gla_rev_kernel(qc_ref, kc_ref, vc_ref, gc_ref, ql_ref, kl_ref, vl_ref, gl_ref, w2_ref, b2_ref,
                    oc_ref, ol_ref, s_ref):
    t = pl.program_id(1)
    dv = vc_ref.shape[1] // GLA_HEADS

    def run(q_ref, k_ref, v_ref, g_ref, o_ref):
        def emit(rows, h, o):
            o_ref[rows, h * dv:(h + 1) * dv] = o
        _gla_scan_tile(q_ref, k_ref, v_ref, g_ref, w2_ref, b2_ref, s_ref, emit, reverse=True)

    @pl.when(t == 0)
    def _():
        s_ref[...] = jnp.zeros_like(s_ref)
        run(qc_ref, kc_ref, vc_ref, gc_ref, oc_ref)

    @pl.when(t > 0)
    def _():
        run(ql_ref, kl_ref, vl_ref, gl_ref, ol_ref)


def _gla_fwd_kernel(qc_ref, kc_ref, vc_ref, rc_ref, gc_ref, pc_ref, ql_ref, kl_ref, vl_ref, rl_ref, gl_ref, pl_ref,
                    w2_ref, b2_ref, gain_ref, oc_ref, ol_ref, s_ref):
    t = pl.program_id(1)
    dv = vc_ref.shape[1] // GLA_HEADS

    def run(q_ref, k_ref, v_ref, r_ref, g_ref, p_ref, o_ref):
        def emit(rows, h, o):
            cols = slice(h * dv, (h + 1) * dv)
            total = o + p_ref[rows, cols]
            r = r_ref[rows, cols].astype(F32)
            o_ref[rows, cols] = (_rms(total) * gain_ref[...] * _silu(r)).astype(o_ref.dtype)
        _gla_scan_tile(q_ref, k_ref, v_ref, g_ref, w2_ref, b2_ref, s_ref, emit, reverse=False)

    @pl.when(t == 0)
    def _():
        s_ref[...] = jnp.zeros_like(s_ref)
        run(qc_ref, kc_ref, vc_ref, rc_ref, gc_ref, pc_ref, oc_ref)

    @pl.when(t > 0)
    def _():
        run(ql_ref, kl_ref, vl_ref, rl_ref, gl_ref, pl_ref, ol_ref)


def _gla(p_l, p_c, g_l, g_c, w2f, b2f, w2b, b2b, gain, batch, dk, dv):
    seq, ctx_len = p_l.shape[0] // batch, p_c.shape[0] // batch
    assert ctx_len == GLA_TILE and seq % GLA_TILE == 0
    n_lat = seq // GLA_TILE
    key_w, val_w = GLA_HEADS * dk, GLA_HEADS * dv
    assert val_w == 2 * key_w

    def lat_fwd(b, t):
        return b * n_lat + jnp.maximum(t - 1, 0)

    def lat_rev(b, t):
        return b * n_lat + n_lat - jnp.maximum(t, 1)

    def tok_specs(row_blk, with_r):
        specs = [pl.BlockSpec((GLA_TILE, key_w), lambda b, t: (row_blk(b, t), 0)),
                 pl.BlockSpec((GLA_TILE, key_w), lambda b, t: (row_blk(b, t), 1)),
                 pl.BlockSpec((GLA_TILE, val_w), lambda b, t: (row_blk(b, t), 1))]
        if with_r:
            specs.append(pl.BlockSpec((GLA_TILE, val_w), lambda b, t: (row_blk(b, t), 2)))
        return specs + [pl.BlockSpec((GLA_TILE, LANE), lambda b, t: (row_blk(b, t), 0))]

    def out_spec(row_blk):
        return pl.BlockSpec((GLA_TILE, val_w), lambda b, t: (row_blk(b, t), 0))

    def ctx_blk(b, t):
        return b

    w2_spec = pl.BlockSpec((LANE, key_w), lambda b, t: (0, 0))
    b2_spec = pl.BlockSpec((1, key_w), lambda b, t: (0, 0))
    state = pltpu.VMEM((GLA_HEADS, dv, dk), F32)
    grid = (batch, 1 + n_lat)
    rev_c, rev_l = pl.pallas_call(
        _gla_rev_kernel, grid=grid,
        in_specs=tok_specs(ctx_blk, False) + tok_specs(lat_rev, False) + [w2_spec, b2_spec],
        out_specs=[out_spec(ctx_blk), out_spec(lat_rev)],
        out_shape=[jax.ShapeDtypeStruct((batch * ctx_len, val_w), F32),
                   jax.ShapeDtypeStruct((batch * seq, val_w), F32)],
        scratch_shapes=[state],
        compiler_params=_cparams("parallel", "arbitrary"), name="gla_rev",
    )(p_c, p_c, p_c, g_c, p_l, p_l, p_l, g_l, w2b, b2b)
    return pl.pallas_call(
        _gla_fwd_kernel, grid=grid,
        in_specs=tok_specs(ctx_blk, True) + [out_spec(ctx_blk)] + tok_specs(lat_fwd, True) + [out_spec(lat_fwd)]
        + [w2_spec, b2_spec, pl.BlockSpec((1, dv), lambda b, t: (0, 0))],
        out_specs=[out_spec(ctx_blk), out_spec(lat_fwd)],
        out_shape=[jax.ShapeDtypeStruct((batch * ctx_len, val_w), BF16),
                   jax.ShapeDtypeStruct((batch * seq, val_w), BF16)],
        scratch_shapes=[state],
        compiler_params=_cparams("parallel", "arbitrary"), name="gla_fwd",
    )(p_c, p_c, p_c, p_c, g_c, rev_c, p_l, p_l, p_l, p_l, g_l, rev_l, w2f, b2f, gain)


def _sgu_kernel(u_ref, s_ref, ws_ref, bs_ref, gain_ref, o_ref):
    gw = u_ref.shape[1] // SGU_GROUPS
    for ci in range(u_ref.shape[0] // SGU_CHUNK):
        rows = slice(ci * SGU_CHUNK, (ci + 1) * SGU_CHUNK)
        for g in range(SGU_GROUPS):
            cols = slice(g * gw, (g + 1) * gw)
            vg = _rms(_gelu_tanh(s_ref[rows, cols].astype(F32))) * gain_ref[:, cols]
            mixed = _dot(ws_ref[g], vg.astype(BF16)) + bs_ref[:, g:g + 1]
            o_ref[rows, cols] = (_gelu_tanh(u_ref[rows, cols].astype(F32)) * mixed).astype(o_ref.dtype)


def _sgu(p, w_s, b_s_t, gain, width, u_blk, s_blk):
    m = p.shape[0]
    rows = 2 * SGU_CHUNK
    return pl.pallas_call(
        _sgu_kernel, grid=(m // rows,),
        in_specs=[pl.BlockSpec((rows, width), lambda i: (i, u_blk)),
                  pl.BlockSpec((rows, width), lambda i: (i, s_blk)),
                  pl.BlockSpec(w_s.shape, lambda i: (0, 0, 0)),
                  pl.BlockSpec(b_s_t.shape, lambda i: (0, 0)),
                  pl.BlockSpec((1, width), lambda i: (0, 0))],
        out_specs=pl.BlockSpec((rows, width), lambda i: (i, 0)),
        out_shape=jax.ShapeDtypeStruct((m, width), BF16),
        compiler_params=_cparams("parallel"), name="sgu",
    )(p, p, w_s, b_s_t, gain)


def _attn_kernel(sink_ref, bias_ref, q_ref, k_ref, v_ref, kc_ref, vc_ref, o_ref):
    kv, ng = pl.program_id(1), pl.program_id(2)
    seq = k_ref.shape[0]
    nb = seq // ATT_BLOCK
    band = 3 * ATT_BLOCK
    sink = jnp.concatenate([jnp.full((ATT_BLOCK, HEAD_DIM), sink_ref[kv * ATT_GROUP + g] * LOG2E, F32)
                            for g in range(ATT_GROUP)], axis=0)
    k_ctx, v_ctx = kc_ref[...], vc_ref[...]
    ones = jnp.ones((band + k_ctx.shape[0], HEAD_DIM), BF16)
    for jj in range(ATT_QB):
        n = ng * ATT_QB + jj
        start = pl.multiple_of(jnp.clip((n - 1) * ATT_BLOCK, 0, seq - band), ATT_BLOCK)
        if jj == 0:
            bias = bias_ref[jnp.where(n == 0, 1, 0)]
        elif jj == ATT_QB - 1:
            bias = bias_ref[jnp.where(n == nb - 1, 2, 0)]
        else:
            bias = bias_ref[0]
        q = q_ref[jj * ATT_BLOCK:(jj + 1) * ATT_BLOCK, :]
        qs = jnp.concatenate([q[:, g * HEAD_DIM:(g + 1) * HEAD_DIM] for g in range(ATT_GROUP)], axis=0)
        keys = jnp.concatenate([k_ref[pl.ds(start, band), :], k_ctx], axis=0)
        vals = jnp.concatenate([v_ref[pl.ds(start, band), :], v_ctx], axis=0)
        s = _dot_nt(qs, keys)
        s = jnp.concatenate([s[:, :band] + bias, s[:, band:]], axis=1)
        m = jnp.maximum(sink, jnp.max(s, axis=-1, keepdims=True))
        p = jnp.exp2(s - jnp.concatenate([m] * (s.shape[1] // HEAD_DIM), axis=1)).astype(BF16)
        ov = _dot(p, jnp.concatenate([vals, ones], axis=1))
        o = ov[:, :HEAD_DIM] / (ov[:, HEAD_DIM:] + jnp.exp2(sink - m))
        o_ref[jj * ATT_BLOCK:(jj + 1) * ATT_BLOCK, :] = jnp.concatenate(
            [o[g * ATT_BLOCK:(g + 1) * ATT_BLOCK] for g in range(ATT_GROUP)], axis=1).astype(o_ref.dtype)


def _band_bias():
    t = (np.arange(ATT_GROUP * ATT_BLOCK) % ATT_BLOCK)[:, None]
    i = np.arange(3 * ATT_BLOCK)[None, :]
    shifts = (-ATT_BLOCK, 0, -2 * ATT_BLOCK)
    return jnp.asarray(np.stack([np.where(np.abs(i + sh - t) <= WINDOW, 0.0, -np.inf) for sh in shifts]), F32)


def _attention(qk, v, k_c, v_c, sink, batch, n_q_heads):
    m = qk.shape[0]
    seq, ctx_len = m // batch, k_c.shape[0] // batch
    kv_heads = n_q_heads // ATT_GROUP
    ng = seq // (ATT_BLOCK * ATT_QB)
    assert ng >= 2
    qw = ATT_GROUP * HEAD_DIM
    bias = _band_bias()
    return pl.pallas_call(
        _attn_kernel, grid=(batch, kv_heads, ng),
        in_specs=[pl.BlockSpec(memory_space=pltpu.SMEM),
                  pl.BlockSpec(bias.shape, lambda b, h, n: (0, 0, 0)),
                  pl.BlockSpec((ATT_QB * ATT_BLOCK, qw), lambda b, h, n: (b * ng + n, h)),
                  pl.BlockSpec((seq, HEAD_DIM), lambda b, h, n: (b, n_q_heads + h)),
                  pl.BlockSpec((seq, HEAD_DIM), lambda b, h, n: (b, h)),
                  pl.BlockSpec((ctx_len, HEAD_DIM), lambda b, h, n: (b, h)),
                  pl.BlockSpec((ctx_len, HEAD_DIM), lambda b, h, n: (b, h))],
        out_specs=pl.BlockSpec((ATT_QB * ATT_BLOCK, qw), lambda b, h, n: (b * ng + n, h)),
        out_shape=jax.ShapeDtypeStruct((m, n_q_heads * HEAD_DIM), BF16),
        compiler_params=_cparams("parallel", "parallel", "arbitrary"), name="attention",
    )(sink, bias, qk, qk, v, k_c, v_c)


def _rope_tables(seq):
    rows = seq // GRID_W
    row = jnp.repeat(jnp.arange(rows, dtype=F32), GRID_W)
    col = jnp.tile(jnp.arange(GRID_W, dtype=F32), rows)
    n_freq = HEAD_DIM // 4
    inv_freq = ROPE_THETA ** (-jnp.arange(n_freq, dtype=F32) / n_freq)
    ang = jnp.concatenate([row[:, None] * inv_freq, col[:, None] * inv_freq], axis=-1)
    cos, sin = jnp.cos(ang), jnp.sin(ang)
    return jnp.concatenate([cos, cos], axis=-1), jnp.concatenate([-sin, sin], axis=-1)


def kernel(x, c, ctx, c_ctx, ada_w, ada_b, ffn_w_gate, ffn_w_up, ffn_w_down, even_w_in, even_gate_w2_fwd, even_gate_b_fwd, even_gate_w2_bwd, even_gate_b_bwd, even_gla_norm_gain, even_sgu_norm_gain, even_sgu_w_s, even_sgu_b_s, even_w_out, odd_w_in, odd_q_norm_gain, odd_k_norm_gain, odd_sink, odd_w_out):
    batch, seq, d = x.shape
    ctx_len = ctx.shape[1]
    depth = ada_w.shape[0]
    assert depth == 2 and batch + 1 <= 8
    xl = x.reshape(batch * seq, d)
    xc = ctx.reshape(batch * ctx_len, d)

    cc = jnp.concatenate([c, c_ctx[None], jnp.zeros((8 - batch - 1, d), F32)], axis=0)
    mods = _ada(cc, ada_w, ada_b)

    def mod_l(layer, which):
        return mods[layer, :batch, which * d:(which + 1) * d].reshape(batch, 1, d)

    def mod_c(layer, which):
        return mods[layer, batch:batch + 1, which * d:(which + 1) * d].reshape(1, 1, d)

    all_ctx = batch * ctx_len

    dv = even_gla_norm_gain.shape[1]
    key_w = even_gate_w2_fwd.shape[2]
    dk = key_w // GLA_HEADS
    val_w = GLA_HEADS * dv
    sgu_w = even_sgu_norm_gain.shape[1]
    gate0 = 2 * key_w + 2 * val_w
    gate1 = gate0 + 2 * GLA_GATE_RANK
    w_qkvr = _cast_col_tiles(even_w_in, 0, 0, gate0, TN)
    w_us = _cast_col_tiles(even_w_in[:, :, gate1:], 0, 0, 2 * sgu_w, TN)
    w_gate = jnp.pad(even_w_in[0, :, gate0:gate1], ((0, 0), (0, LANE - 2 * GLA_GATE_RANK))).astype(BF16)[None]
    pad_rows = LANE - GLA_GATE_RANK
    w2f = jnp.pad(even_gate_w2_fwd[0], ((0, pad_rows), (0, 0))).astype(BF16)
    w2b = jnp.pad(even_gate_w2_bwd[0], ((GLA_GATE_RANK, pad_rows - GLA_GATE_RANK), (0, 0))).astype(BF16)
    b2f = even_gate_b_fwd[0].reshape(1, key_w)
    b2b = even_gate_b_bwd[0].reshape(1, key_w)

    h_l = _modulate(xl, mod_l(0, 0), mod_l(0, 1), seq)
    h_c = _modulate(xc, mod_c(0, 0), mod_c(0, 1), all_ctx)
    p_l, p_c = _proj(h_l, w_qkvr), _proj(h_c, w_qkvr)
    us_l, us_c = _proj(h_l, w_us), _proj(h_c, w_us)
    g_l, g_c = _proj(h_l, w_gate), _proj(h_c, w_gate)
    gla_c, gla_l = _gla(p_l, p_c, g_l, g_c, w2f, b2f, w2b, b2b, even_gla_norm_gain[0].reshape(1, dv),
                        batch, dk, dv)
    w_s = even_sgu_w_s[0].astype(BF16)
    b_s_t = even_sgu_b_s[0].T
    sgu_gain = even_sgu_norm_gain[0].reshape(1, sgu_w)
    sgu_l = _sgu(us_l, w_s, b_s_t, sgu_gain, sgu_w, 0, 1)
    sgu_c = _sgu(us_c, w_s, b_s_t, sgu_gain, sgu_w, 0, 1)
    assert val_w == sgu_w
    w_out = _cast_col_tiles(even_w_out, 0, 0, d, TN)
    x_l = _out_res([gla_l, sgu_l], w_out, xl, mod_l(0, 2), seq)
    x_c = _out_res([gla_c, sgu_c], w_out, xc, mod_c(0, 2), all_ctx)

    hidden = ffn_w_gate.shape[2]
    wg = _cast_col_tiles(ffn_w_gate, 0, 0, hidden, FFN_TH)
    wu = _cast_col_tiles(ffn_w_up, 0, 0, hidden, FFN_TH)
    wd = _cast_rows(ffn_w_down, 0, FFN_TH)
    x_l = _ffn(x_l, mod_l(0, 3), mod_l(0, 4), mod_l(0, 5), wg, wu, wd, seq)
    x_c = _ffn(x_c, mod_c(0, 3), mod_c(0, 4), mod_c(0, 5), wg, wu, wd, all_ctx)

    h_l = _modulate(x_l, mod_l(1, 0), mod_l(1, 1), seq)
    h_c = _modulate(x_c, mod_c(1, 0), mod_c(1, 1), all_ctx)

    n_heads = odd_sink.shape[1]
    q_w = n_heads * HEAD_DIM
    kv_w = q_w // ATT_GROUP
    qk_tile = 2 * TN
    assert kv_w == qk_tile
    w_qk = _cast_col_tiles(odd_w_in, 0, 0, q_w + kv_w, qk_tile)
    w_v = _cast_col_tiles(odd_w_in, 0, q_w + kv_w, kv_w, TN)
    k_gain = jnp.tile(odd_k_norm_gain[0], kv_w // HEAD_DIM)
    qk_gain = jnp.concatenate([jnp.tile(odd_q_norm_gain[0] * (HEAD_DIM ** -0.5 * LOG2E), n_heads),
                               k_gain]).reshape(1, -1)
    cos2, sin2 = _rope_tables(seq)
    qk_l = _qk_proj(h_l, w_qk, 0, w_qk.shape[0], qk_gain, cos2, sin2, rope=True)
    v_l = _proj(h_l, w_v)
    k_c = _qk_proj(h_c, w_qk, q_w // qk_tile, 1, k_gain.reshape(1, -1), cos2, sin2, rope=False)
    v_c = _proj(h_c, w_v)
    att = _attention(qk_l, v_l, k_c, v_c, odd_sink[0], batch, n_heads)
    x_l = _out_res([att], _cast_col_tiles(odd_w_out, 0, 0, d, TN), x_l, mod_l(1, 2), seq)

    wg = _cast_col_tiles(ffn_w_gate, 1, 0, hidden, FFN_TH)
    wu = _cast_col_tiles(ffn_w_up, 1, 0, hidden, FFN_TH)
    wd = _cast_rows(ffn_w_down, 1, FFN_TH)
    return _ffn(x_l, mod_l(1, 3), mod_l(1, 4), mod_l(1, 5), wg, wu, wd, seq).reshape(batch, seq, d)
```

```python
import functools

import numpy as np
import jax
import jax.numpy as jnp
from jax import lax
from jax.experimental import pallas as pl
from jax.experimental.pallas import tpu as pltpu

F32 = jnp.float32
BF16 = jnp.bfloat16

EPS = 1e-6
LOG2E = float(np.log2(np.e))
N_MOD = 6
GRID_W = 64
ROPE_THETA = 10000.0

GLA_HEADS = 4
GLA_TAU = 16.0
GLA_GATE_RANK = 16
GLA_BLOCK = 64
GLA_TILE = 256
SGU_GROUPS = 8
SGU_CHUNK = 128
HEAD_DIM = 128
ATT_GROUP = 4
WINDOW = 128
ATT_BLOCK = 128
ATT_QB = 8

LANE = 128
MXU_W = 256
VMEM_LIMIT = 56 * 1024 * 1024
FFN_VMEM_LIMIT = 60 * 1024 * 1024

TM = 1024
TN = 512
TN_WIDE = 1024
FFN_TH = 256
FFN_X_ROWS = 128
FFN_X_GROUP = 16
ROWS_EW = 256


def _cparams(*sem):
    return pltpu.CompilerParams(dimension_semantics=sem, vmem_limit_bytes=VMEM_LIMIT)


def _dot(a, b):
    return jnp.dot(a, b, preferred_element_type=F32)


def _dot_nt(a, b):
    return lax.dot_general(a, b, (((1,), (1,)), ((), ())), preferred_element_type=F32)


def _dot_tn(a, b):
    return lax.dot_general(a, b, (((0,), (0,)), ((), ())), preferred_element_type=F32)


def _silu(x):
    return x * jax.nn.sigmoid(x)


def _gelu_tanh(x):
    c = np.float32(np.sqrt(2.0 / np.pi))
    return x * (0.5 * (1.0 + jnp.tanh(c * (x + 0.044715 * (x * x * x)))))


def _rms(x):
    return x * lax.rsqrt(jnp.mean(x * x, axis=-1, keepdims=True) + EPS)


def _ada_kernel(c_ref, w_ref, b_ref, o_ref):
    a = _silu(c_ref[...]).astype(BF16)
    o_ref[...] = _dot(a, w_ref[...].astype(BF16)) + b_ref[...]


def _ada(cc, ada_w, ada_b):
    depth, d, n = ada_w.shape
    rows = cc.shape[0]
    return pl.pallas_call(
        _ada_kernel,
        grid=(depth, n // TN),
        in_specs=[pl.BlockSpec((rows, d), lambda l, j: (0, 0)),
                  pl.BlockSpec((None, d, TN), lambda l, j: (l, 0, j)),
                  pl.BlockSpec((None, 1, TN), lambda l, j: (l, 0, j))],
        out_specs=pl.BlockSpec((None, rows, TN), lambda l, j: (l, 0, j)),
        out_shape=jax.ShapeDtypeStruct((depth, rows, n), F32),
        compiler_params=_cparams("parallel", "parallel"),
        name="ada",
    )(cc, ada_w, ada_b.reshape(depth, 1, n))


def _modulate_kernel(x_ref, sh_ref, sc_ref, h_ref):
    h_ref[...] = (_rms(x_ref[...]) * (1.0 + sc_ref[...]) + sh_ref[...]).astype(h_ref.dtype)


def _row_spec(d):
    return pl.BlockSpec((ROWS_EW, d), lambda i: (i, 0))


def _vec_spec(d, rows_per_vec):
    return pl.BlockSpec((None, 1, d), lambda i: ((i * ROWS_EW) // rows_per_vec, 0, 0))


def _modulate(x, shift, scale, rows_per_vec):
    m, d = x.shape
    return pl.pallas_call(
        _modulate_kernel, grid=(m // ROWS_EW,),
        in_specs=[_row_spec(d), _vec_spec(d, rows_per_vec), _vec_spec(d, rows_per_vec)],
        out_specs=_row_spec(d),
        out_shape=jax.ShapeDtypeStruct((m, d), BF16),
        compiler_params=_cparams("parallel"), name="modulate",
    )(x, shift, scale)


def _cast_kernel(w_ref, o_ref):
    o_ref[...] = w_ref[...].astype(o_ref.dtype)


def _cast_col_tiles(w, layer, col0, n_cols, tile):
    k = w.shape[1]
    assert col0 % tile == 0 and n_cols % tile == 0
    t0 = col0 // tile
    return pl.pallas_call(
        _cast_kernel, grid=(n_cols // tile,),
        in_specs=[pl.BlockSpec((None, k, tile), lambda j: (layer, 0, t0 + j))],
        out_specs=pl.BlockSpec((None, k, tile), lambda j: (j, 0, 0)),
        out_shape=jax.ShapeDtypeStruct((n_cols // tile, k, tile), BF16),
        compiler_params=_cparams("parallel"), name="cast_cols",
    )(w)


def _cast_pair_kernel(a_ref, b_ref, o_ref):
    tile = a_ref.shape[1]
    o_ref[:, :tile] = a_ref[...].astype(o_ref.dtype)
    o_ref[:, tile:] = b_ref[...].astype(o_ref.dtype)


def _cast_col_tile_pairs(wa, wb, layer, tile):
    k, n = wa.shape[1:]
    src = pl.BlockSpec((None, k, tile), lambda j: (layer, 0, j))
    return pl.pallas_call(
        _cast_pair_kernel, grid=(n // tile,),
        in_specs=[src, src],
        out_specs=pl.BlockSpec((None, k, 2 * tile), lambda j: (j, 0, 0)),
        out_shape=jax.ShapeDtypeStruct((n // tile, k, 2 * tile), BF16),
        compiler_params=_cparams("parallel"), name="cast_pairs",
    )(wa, wb)


def _cast_shifted_kernel(a_ref, b_ref, o_ref, *, shift):
    rows = 256
    for r0 in range(0, a_ref.shape[0], rows):
        a = a_ref[r0:r0 + rows, :]
        b = b_ref[r0:r0 + rows, :]
        o_ref[r0:r0 + rows, :] = jnp.concatenate([a[:, shift:], b[:, :shift]], axis=1).astype(o_ref.dtype)


def _cast_col_tiles_shifted(w, layer, col0, n_cols, tile):
    k = w.shape[1]
    shift = col0 % LANE
    base = col0 - shift
    assert shift and base % tile == 0 and n_cols % tile == 0
    return pl.pallas_call(
        functools.partial(_cast_shifted_kernel, shift=shift), grid=(n_cols // tile,),
        in_specs=[pl.BlockSpec((None, k, tile), lambda j: (layer, 0, base // tile + j)),
                  pl.BlockSpec((None, k, LANE), lambda j: (layer, 0, (base + (j + 1) * tile) // LANE))],
        out_specs=pl.BlockSpec((None, k, tile), lambda j: (j, 0, 0)),
        out_shape=jax.ShapeDtypeStruct((n_cols // tile, k, tile), BF16),
        compiler_params=_cparams("parallel"), name="cast_cols_shifted",
    )(w, w)


def _cast_rows(w, layer, tile):
    k, n = w.shape[1:]
    return pl.pallas_call(
        _cast_kernel, grid=(k // tile,),
        in_specs=[pl.BlockSpec((None, tile, n), lambda j: (layer, j, 0))],
        out_specs=pl.BlockSpec((tile, n), lambda j: (j, 0)),
        out_shape=jax.ShapeDtypeStruct((k, n), BF16),
        compiler_params=_cparams("parallel"), name="cast_rows",
    )(w)


def _proj_kernel(x_ref, w_ref, o_ref):
    o_ref[...] = _dot(x_ref[...], w_ref[...]).astype(o_ref.dtype)


def _proj(x, w3):
    m, k = x.shape
    nt, _, tn = w3.shape
    tm = min(TM, m)
    return pl.pallas_call(
        _proj_kernel, grid=(m // tm, nt),
        in_specs=[pl.BlockSpec((tm, k), lambda i, j: (i, 0)),
                  pl.BlockSpec((None, k, tn), lambda i, j: (j, 0, 0))],
        out_specs=pl.BlockSpec((tm, tn), lambda i, j: (i, j)),
        out_shape=jax.ShapeDtypeStruct((m, nt * tn), BF16),
        compiler_params=_cparams("parallel", "parallel"), name="proj",
    )(x, w3)


def _qk_proj_kernel(x_ref, w_ref, gain_ref, cos_ref, sin_ref, ones_ref, o_ref, *, rope):
    half = x_ref.shape[0] // 2
    for rr in range(2):
        rows = slice(rr * half, (rr + 1) * half)
        x = x_ref[rows, :]
        for cc in range(w_ref.shape[1] // TN):
            acc = _dot(x, w_ref[:, cc * TN:(cc + 1) * TN])
            for tt in range(TN // MXU_W):
                c0 = cc * TN + tt * MXU_W
                a = acc[:, tt * MXU_W:(tt + 1) * MXU_W]
                ssq = _dot((a * a).astype(BF16), ones_ref[...])
                y = a * lax.rsqrt(ssq * (1.0 / HEAD_DIM) + EPS) * gain_ref[:, c0:c0 + MXU_W]
                for hh in range(MXU_W // HEAD_DIM):
                    yh = y[:, hh * HEAD_DIM:(hh + 1) * HEAD_DIM]
                    if rope:
                        yh = (yh * cos_ref[rows, :]
                              + pltpu.roll(yh, HEAD_DIM // 2, 1) * sin_ref[rows, :])
                    o_ref[rows, c0 + hh * HEAD_DIM:c0 + (hh + 1) * HEAD_DIM] = yh.astype(o_ref.dtype)


def _qk_proj(x, w3, tile0, n_tiles, gain, cos2, sin2, rope):
    m, k = x.shape
    tn = w3.shape[2]
    tm = min(TM, m)
    seq_blocks = cos2.shape[0] // tm if rope else 1
    tab = pl.BlockSpec((tm, HEAD_DIM), lambda i, j: (i % seq_blocks, 0))
    head_of = np.arange(MXU_W) // HEAD_DIM
    ones = jnp.asarray(head_of[:, None] == head_of[None, :], BF16)
    return pl.pallas_call(
        functools.partial(_qk_proj_kernel, rope=rope), grid=(m // tm, n_tiles),
        in_specs=[pl.BlockSpec((tm, k), lambda i, j: (i, 0)),
                  pl.BlockSpec((None, k, tn), lambda i, j: (tile0 + j, 0, 0)),
                  pl.BlockSpec((1, tn), lambda i, j: (0, j)),
                  tab, tab,
                  pl.BlockSpec((MXU_W, MXU_W), lambda i, j: (0, 0))],
        out_specs=pl.BlockSpec((tm, tn), lambda i, j: (i, j)),
        out_shape=jax.ShapeDtypeStruct((m, n_tiles * tn), BF16),
        compiler_params=_cparams("parallel", "parallel"), name="qk_proj",
    )(x, w3, gain, cos2, sin2, ones)


def _out_res_kernel(*refs, n_in):
    xs, ws = refs[:n_in], refs[n_in:2 * n_in]
    res_ref, g_ref, o_ref = refs[2 * n_in:]
    acc = _dot(xs[0][...], ws[0][...])
    for x_ref, w_ref in zip(xs[1:], ws[1:]):
        acc = acc + _dot(x_ref[...], w_ref[...])
    o_ref[...] = res_ref[...] + g_ref[...] * acc


def _out_res(xs, w3, res, gate, rows_per_vec):
    m, n = res.shape
    nt, k, tn = w3.shape
    tm = min(TM, m)
    n_in = len(xs)
    kx = k // n_in
    assert all(x.shape[1] == kx for x in xs) and nt * tn == n

    def w_spec(part):
        return pl.BlockSpec((None, kx, tn), lambda i, j: (j, part, 0))

    in_specs = ([pl.BlockSpec((tm, kx), lambda i, j: (i, 0)) for _ in xs]
                + [w_spec(part) for part in range(n_in)]
                + [pl.BlockSpec((tm, tn), lambda i, j: (i, j)),
                   pl.BlockSpec((None, 1, tn), lambda i, j: ((i * tm) // rows_per_vec, 0, j))])
    return pl.pallas_call(
        functools.partial(_out_res_kernel, n_in=n_in), grid=(m // tm, nt),
        in_specs=in_specs,
        out_specs=pl.BlockSpec((tm, tn), lambda i, j: (i, j)),
        out_shape=jax.ShapeDtypeStruct((m, n), F32),
        compiler_params=_cparams("parallel", "parallel"), name="out_res",
    )(*xs, *([w3] * n_in), res, gate)


def _ffn_kernel(x_hbm, sh_ref, sc_ref, gate_ref, wgu_ref, wd_ref, o_ref, h_ref, xbuf_ref, sem):
    i, j = pl.program_id(0), pl.program_id(1)
    tm = o_ref.shape[0]
    n_chunks = tm // FFN_X_ROWS

    def x_copy(ci, slot):
        src = x_hbm.at[pl.ds(i * tm + ci * FFN_X_ROWS, FFN_X_ROWS), :]
        return pltpu.make_async_copy(src, xbuf_ref.at[slot], sem.at[slot])

    @pl.when(j == 0)
    def _():
        x_copy(0, 0).start()
        for ci in range(n_chunks):
            slot = ci % 2
            if ci + 1 < n_chunks:
                x_copy(ci + 1, 1 - slot).start()
            x_copy(ci, slot).wait()

            def group(gi, carry, ci=ci, slot=slot):
                src = pl.ds(pl.multiple_of(gi * FFN_X_GROUP, FFN_X_GROUP), FFN_X_GROUP)
                dst = pl.ds(pl.multiple_of(ci * FFN_X_ROWS + gi * FFN_X_GROUP, FFN_X_GROUP), FFN_X_GROUP)
                xg = xbuf_ref[slot, src, :]
                o_ref[dst, :] = xg
                h_ref[dst, :] = (_rms(xg) * (1.0 + sc_ref[...]) + sh_ref[...]).astype(BF16)
                return carry
            lax.fori_loop(0, FFN_X_ROWS // FFN_X_GROUP, group, 0, unroll=2)

    gu = _dot(h_ref[...], wgu_ref[...])
    a = (_silu(gu[:, :FFN_TH]) * gu[:, FFN_TH:]).astype(BF16)
    o_ref[...] += gate_ref[...] * _dot(a, wd_ref[...])


def _ffn(x, shift, scale, gate, wgu3, wd, rows_per_vec):
    m, d = x.shape
    n_tiles = wgu3.shape[0]
    tm = min(TM, m)
    vec = pl.BlockSpec((None, 1, d), lambda i, j: ((i * tm) // rows_per_vec, 0, 0))
    return pl.pallas_call(
        _ffn_kernel, grid=(m // tm, n_tiles),
        in_specs=[pl.BlockSpec(memory_space=pl.ANY),
                  vec, vec, vec,
                  pl.BlockSpec((None, d, 2 * FFN_TH), lambda i, j: (j, 0, 0)),
                  pl.BlockSpec((FFN_TH, d), lambda i, j: (j, 0))],
        out_specs=pl.BlockSpec((tm, d), lambda i, j: (i, 0)),
        out_shape=jax.ShapeDtypeStruct((m, d), F32),
        scratch_shapes=[pltpu.VMEM((tm, d), BF16), pltpu.VMEM((2, FFN_X_ROWS, d), F32),
                        pltpu.SemaphoreType.DMA((2,))],
        compiler_params=pltpu.CompilerParams(dimension_semantics=("parallel", "arbitrary"),
                                             vmem_limit_bytes=FFN_VMEM_LIMIT), name="ffn",
    )(x, shift, scale, gate, wgu3, wd)


def _gla_scan_tile(q_ref, k_ref, v_ref, g_ref, w2_ref, b2_ref, s_ref, emit, reverse):
    dk = q_ref.shape[1] // GLA_HEADS
    dv = v_ref.shape[1] // GLA_HEADS
    row = lax.broadcasted_iota(jnp.int32, (GLA_BLOCK, GLA_BLOCK), 0)
    col = lax.broadcasted_iota(jnp.int32, (GLA_BLOCK, GLA_BLOCK), 1)
    if reverse:
        tri = (col >= row).astype(BF16)
        mask = col > row
        last_row = 0
    else:
        tri = (col <= row).astype(BF16)
        mask = col <= row
        last_row = GLA_BLOCK - 1
    n_steps = q_ref.shape[0] // GLA_BLOCK
    order = [n_steps - 1 - st if reverse else st for st in range(n_steps)]

    parts = []
    for c in order:
        rows = slice(c * GLA_BLOCK, (c + 1) * GLA_BLOCK)
        q = q_ref[rows, :].astype(F32) * (dk ** -0.5)
        k = k_ref[rows, :].astype(F32)
        x = _dot(g_ref[rows, :], w2_ref[...]) + b2_ref[...]
        log_a = (jnp.minimum(x, 0.0) - jnp.log(1.0 + jnp.exp(-jnp.abs(x)))) * (1.0 / GLA_TAU)
        hi = log_a.astype(BF16)
        lo = (log_a - hi.astype(F32)).astype(BF16)
        b = _dot(tri, hi) + _dot(tri, lo)
        mid = b[GLA_BLOCK // 2:GLA_BLOCK // 2 + 1, :]
        b_last = b[last_row:last_row + 1, :]
        qe = q * jnp.exp(b - mid)
        ke = k * jnp.exp(mid - b)
        q_in = (qe * jnp.exp(mid)).astype(BF16)
        k_out = (ke * jnp.exp(b_last - mid)).astype(BF16)
        decay = jnp.exp(b_last)
        qe, ke = qe.astype(BF16), ke.astype(BF16)
        intra, kv = [], []
        for h in range(GLA_HEADS):
            ks = slice(h * dk, (h + 1) * dk)
            v = v_ref[rows, h * dv:(h + 1) * dv]
            att = jnp.where(mask, _dot_nt(qe[:, ks], ke[:, ks]), 0.0).astype(BF16)
            intra.append(_dot(att, v))
            kv.append(_dot_tn(v, k_out[:, ks]))
        parts.append((rows, q_in, decay, intra, kv))

    for h in range(GLA_HEADS):
        ks = slice(h * dk, (h + 1) * dk)
        s = s_ref[h]
        for rows, q_in, decay, intra, kv in parts:
            emit(rows, h, intra[h] + _dot_nt(q_in[:, ks], s.astype(BF16)))
            s = s * decay[:, ks] + kv[h]
        s_ref[h] = s


def _gla_rev_kernel(qc_ref, kc_ref, vc_ref, gc_ref, ql_ref, kl_ref, vl_ref, gl_ref, w2_ref, b2_ref,
                    oc_ref, ol_ref, s_ref):
    t = pl.program_id(1)
    dv = vc_ref.shape[1] // GLA_HEADS

    def run(q_ref, k_ref, v_ref, g_ref, o_ref):
        def emit(rows, h, o):
            o_ref[rows, h * dv:(h + 1) * dv] = o
        _gla_scan_tile(q_ref, k_ref, v_ref, g_ref, w2_ref, b2_ref, s_ref, emit, reverse=True)

    @pl.when(t == 0)
    def _():
        s_ref[...] = jnp.zeros_like(s_ref)
        run(qc_ref, kc_ref, vc_ref, gc_ref, oc_ref)

    @pl.when(t > 0)
    def _():
        run(ql_ref, kl_ref, vl_ref, gl_ref, ol_ref)


def _gla_fwd_kernel(qc_ref, kc_ref, vc_ref, rc_ref, gc_ref, pc_ref, ql_ref, kl_ref, vl_ref, rl_ref, gl_ref, pl_ref,
                    w2_ref, b2_ref, gain_ref, oc_ref, ol_ref, s_ref):
    t = pl.program_id(1)
    dv = vc_ref.shape[1] // GLA_HEADS

    def run(q_ref, k_ref, v_ref, r_ref, g_ref, p_ref, o_ref):
        def emit(rows, h, o):
            cols = slice(h * dv, (h + 1) * dv)
            total = o + p_ref[rows, cols]
            r = r_ref[rows, cols].astype(F32)
            o_ref[rows, cols] = (_rms(total) * gain_ref[...] * _silu(r)).astype(o_ref.dtype)
        _gla_scan_tile(q_ref, k_ref, v_ref, g_ref, w2_ref, b2_ref, s_ref, emit, reverse=False)

    @pl.when(t == 0)
    def _():
        s_ref[...] = jnp.zeros_like(s_ref)
        run(qc_ref, kc_ref, vc_ref, rc_ref, gc_ref, pc_ref, oc_ref)

    @pl.when(t > 0)
    def _():
        run(ql_ref, kl_ref, vl_ref, rl_ref, gl_ref, pl_ref, ol_ref)


def _gla(p_l, p_c, g_l, g_c, w2f, b2f, w2b, b2b, gain, batch, dk, dv):
    seq, ctx_len = p_l.shape[0] // batch, p_c.shape[0] // batch
    assert ctx_len == GLA_TILE and seq % GLA_TILE == 0
    n_lat = seq // GLA_TILE
    key_w, val_w = GLA_HEADS * dk, GLA_HEADS * dv
    assert val_w == 2 * key_w

    def lat_fwd(b, t):
        return b * n_lat + jnp.maximum(t - 1, 0)

    def lat_rev(b, t):
        return b * n_lat + n_lat - jnp.maximum(t, 1)

    def tok_specs(row_blk, with_r):
        specs = [pl.BlockSpec((GLA_TILE, key_w), lambda b, t: (row_blk(b, t), 0)),
                 pl.BlockSpec((GLA_TILE, key_w), lambda b, t: (row_blk(b, t), 1)),
                 pl.BlockSpec((GLA_TILE, val_w), lambda b, t: (row_blk(b, t), 1))]
        if with_r:
            specs.append(pl.BlockSpec((GLA_TILE, val_w), lambda b, t: (row_blk(b, t), 2)))
        return specs + [pl.BlockSpec((GLA_TILE, LANE), lambda b, t: (row_blk(b, t), 0))]

    def out_spec(row_blk):
        return pl.BlockSpec((GLA_TILE, val_w), lambda b, t: (row_blk(b, t), 0))

    def ctx_blk(b, t):
        return b

    w2_spec = pl.BlockSpec((LANE, key_w), lambda b, t: (0, 0))
    b2_spec = pl.BlockSpec((1, key_w), lambda b, t: (0, 0))
    state = pltpu.VMEM((GLA_HEADS, dv, dk), F32)
    grid = (batch, 1 + n_lat)
    rev_c, rev_l = pl.pallas_call(
        _gla_rev_kernel, grid=grid,
        in_specs=tok_specs(ctx_blk, False) + tok_specs(lat_rev, False) + [w2_spec, b2_spec],
        out_specs=[out_spec(ctx_blk), out_spec(lat_rev)],
        out_shape=[jax.ShapeDtypeStruct((batch * ctx_len, val_w), F32),
                   jax.ShapeDtypeStruct((batch * seq, val_w), F32)],
        scratch_shapes=[state],
        compiler_params=_cparams("parallel", "arbitrary"), name="gla_rev",
    )(p_c, p_c, p_c, g_c, p_l, p_l, p_l, g_l, w2b, b2b)
    return pl.pallas_call(
        _gla_fwd_kernel, grid=grid,
        in_specs=tok_specs(ctx_blk, True) + [out_spec(ctx_blk)] + tok_specs(lat_fwd, True) + [out_spec(lat_fwd)]
        + [w2_spec, b2_spec, pl.BlockSpec((1, dv), lambda b, t: (0, 0))],
        out_specs=[out_spec(ctx_blk), out_spec(lat_fwd)],
        out_shape=[jax.ShapeDtypeStruct((batch * ctx_len, val_w), BF16),
                   jax.ShapeDtypeStruct((batch * seq, val_w), BF16)],
        scratch_shapes=[state],
        compiler_params=_cparams("parallel", "arbitrary"), name="gla_fwd",
    )(p_c, p_c, p_c, p_c, g_c, rev_c, p_l, p_l, p_l, p_l, g_l, rev_l, w2f, b2f, gain)


def _sgu_kernel(u_ref, s_ref, ws_ref, bs_ref, gain_ref, o_ref):
    gw = u_ref.shape[1] // SGU_GROUPS
    for ci in range(u_ref.shape[0] // SGU_CHUNK):
        rows = slice(ci * SGU_CHUNK, (ci + 1) * SGU_CHUNK)
        for g in range(SGU_GROUPS):
            cols = slice(g * gw, (g + 1) * gw)
            vg = _rms(_gelu_tanh(s_ref[rows, cols].astype(F32))) * gain_ref[:, cols]
            mixed = _dot(ws_ref[g], vg.astype(BF16)) + bs_ref[:, g:g + 1]
            o_ref[rows, cols] = (_gelu_tanh(u_ref[rows, cols].astype(F32)) * mixed).astype(o_ref.dtype)


def _sgu(p, w_s, b_s_t, gain, width, u_blk, s_blk):
    m = p.shape[0]
    rows = 2 * SGU_CHUNK
    return pl.pallas_call(
        _sgu_kernel, grid=(m // rows,),
        in_specs=[pl.BlockSpec((rows, width), lambda i: (i, u_blk)),
                  pl.BlockSpec((rows, width), lambda i: (i, s_blk)),
                  pl.BlockSpec(w_s.shape, lambda i: (0, 0, 0)),
                  pl.BlockSpec(b_s_t.shape, lambda i: (0, 0)),
                  pl.BlockSpec((1, width), lambda i: (0, 0))],
        out_specs=pl.BlockSpec((rows, width), lambda i: (i, 0)),
        out_shape=jax.ShapeDtypeStruct((m, width), BF16),
        compiler_params=_cparams("parallel"), name="sgu",
    )(p, p, w_s, b_s_t, gain)


def _attn_kernel(sink_ref, bias_ref, q_ref, k_ref, v_ref, kc_ref, vc_ref, o_ref):
    kv, ng = pl.program_id(1), pl.program_id(2)
    seq = k_ref.shape[0]
    nb = seq // ATT_BLOCK
    band = 3 * ATT_BLOCK
    sink = jnp.concatenate([jnp.full((ATT_BLOCK, HEAD_DIM), sink_ref[kv * ATT_GROUP + g] * LOG2E, F32)
                            for g in range(ATT_GROUP)], axis=0)
    k_ctx, v_ctx = kc_ref[...], vc_ref[...]
    ones = jnp.ones((band + k_ctx.shape[0], HEAD_DIM), BF16)
    for jj in range(ATT_QB):
        n = ng * ATT_QB + jj
        start = pl.multiple_of(jnp.clip((n - 1) * ATT_BLOCK, 0, seq - band), ATT_BLOCK)
        if jj == 0:
            bias = bias_ref[jnp.where(n == 0, 1, 0)]
        elif jj == ATT_QB - 1:
            bias = bias_ref[jnp.where(n == nb - 1, 2, 0)]
        else:
            bias = bias_ref[0]
        q = q_ref[jj * ATT_BLOCK:(jj + 1) * ATT_BLOCK, :]
        qs = jnp.concatenate([q[:, g * HEAD_DIM:(g + 1) * HEAD_DIM] for g in range(ATT_GROUP)], axis=0)
        keys = jnp.concatenate([k_ref[pl.ds(start, band), :], k_ctx], axis=0)
        vals = jnp.concatenate([v_ref[pl.ds(start, band), :], v_ctx], axis=0)
        s = _dot_nt(qs, keys)
        s = jnp.concatenate([s[:, :band] + bias, s[:, band:]], axis=1)
        m = jnp.maximum(sink, jnp.max(s, axis=-1, keepdims=True))
        p = jnp.exp2(s - jnp.concatenate([m] * (s.shape[1] // HEAD_DIM), axis=1)).astype(BF16)
        ov = _dot(p, jnp.concatenate([vals, ones], axis=1))
        o = ov[:, :HEAD_DIM] / (ov[:, HEAD_DIM:] + jnp.exp2(sink - m))
        o_ref[jj * ATT_BLOCK:(jj + 1) * ATT_BLOCK, :] = jnp.concatenate(
            [o[g * ATT_BLOCK:(g + 1) * ATT_BLOCK] for g in range(ATT_GROUP)], axis=1).astype(o_ref.dtype)


def _band_bias():
    t = (np.arange(ATT_GROUP * ATT_BLOCK) % ATT_BLOCK)[:, None]
    i = np.arange(3 * ATT_BLOCK)[None, :]
    shifts = (-ATT_BLOCK, 0, -2 * ATT_BLOCK)
    return jnp.asarray(np.stack([np.where(np.abs(i + sh - t) <= WINDOW, 0.0, -np.inf) for sh in shifts]), F32)


def _attention(qk, v, k_c, v_c, sink, batch, n_q_heads):
    m = qk.shape[0]
    seq, ctx_len = m // batch, k_c.shape[0] // batch
    kv_heads = n_q_heads // ATT_GROUP
    ng = seq // (ATT_BLOCK * ATT_QB)
    assert ng >= 2
    qw = ATT_GROUP * HEAD_DIM
    bias = _band_bias()
    return pl.pallas_call(
        _attn_kernel, grid=(batch, kv_heads, ng),
        in_specs=[pl.BlockSpec(memory_space=pltpu.SMEM),
                  pl.BlockSpec(bias.shape, lambda b, h, n: (0, 0, 0)),
                  pl.BlockSpec((ATT_QB * ATT_BLOCK, qw), lambda b, h, n: (b * ng + n, h)),
                  pl.BlockSpec((seq, HEAD_DIM), lambda b, h, n: (b, n_q_heads + h)),
                  pl.BlockSpec((seq, HEAD_DIM), lambda b, h, n: (b, h)),
                  pl.BlockSpec((ctx_len, HEAD_DIM), lambda b, h, n: (b, h)),
                  pl.BlockSpec((ctx_len, HEAD_DIM), lambda b, h, n: (b, h))],
        out_specs=pl.BlockSpec((ATT_QB * ATT_BLOCK, qw), lambda b, h, n: (b * ng + n, h)),
        out_shape=jax.ShapeDtypeStruct((m, n_q_heads * HEAD_DIM), BF16),
        compiler_params=_cparams("parallel", "parallel", "arbitrary"), name="attention",
    )(sink, bias, qk, qk, v, k_c, v_c)


def _rope_tables(seq):
    rows = seq // GRID_W
    row = jnp.repeat(jnp.arange(rows, dtype=F32), GRID_W)
    col = jnp.tile(jnp.arange(GRID_W, dtype=F32), rows)
    n_freq = HEAD_DIM // 4
    inv_freq = ROPE_THETA ** (-jnp.arange(n_freq, dtype=F32) / n_freq)
    ang = jnp.concatenate([row[:, None] * inv_freq, col[:, None] * inv_freq], axis=-1)
    cos, sin = jnp.cos(ang), jnp.sin(ang)
    return jnp.concatenate([cos, cos], axis=-1), jnp.concatenate([-sin, sin], axis=-1)


def kernel(x, c, ctx, c_ctx, ada_w, ada_b, ffn_w_gate, ffn_w_up, ffn_w_down, even_w_in, even_gate_w2_fwd, even_gate_b_fwd, even_gate_w2_bwd, even_gate_b_bwd, even_gla_norm_gain, even_sgu_norm_gain, even_sgu_w_s, even_sgu_b_s, even_w_out, odd_w_in, odd_q_norm_gain, odd_k_norm_gain, odd_sink, odd_w_out):
    batch, seq, d = x.shape
    ctx_len = ctx.shape[1]
    depth = ada_w.shape[0]
    assert depth == 2 and batch + 1 <= 8
    xl = x.reshape(batch * seq, d)
    xc = ctx.reshape(batch * ctx_len, d)

    cc = jnp.concatenate([c, c_ctx[None], jnp.zeros((8 - batch - 1, d), F32)], axis=0)
    mods = _ada(cc, ada_w, ada_b)

    def mod_l(layer, which):
        return mods[layer, :batch, which * d:(which + 1) * d].reshape(batch, 1, d)

    def mod_c(layer, which):
        return mods[layer, batch:batch + 1, which * d:(which + 1) * d].reshape(1, 1, d)

    all_ctx = batch * ctx_len

    dv = even_gla_norm_gain.shape[1]
    key_w = even_gate_w2_fwd.shape[2]
    dk = key_w // GLA_HEADS
    val_w = GLA_HEADS * dv
    sgu_w = even_sgu_norm_gain.shape[1]
    gate0 = 2 * key_w + 2 * val_w
    gate1 = gate0 + 2 * GLA_GATE_RANK
    w_qkvr = _cast_col_tiles(even_w_in, 0, 0, gate0, TN_WIDE)
    w_us = _cast_col_tiles_shifted(even_w_in, 0, gate1, 2 * sgu_w, TN_WIDE)
    w_gate = jnp.pad(even_w_in[0, :, gate0:gate1], ((0, 0), (0, LANE - 2 * GLA_GATE_RANK))).astype(BF16)[None]
    pad_rows = LANE - GLA_GATE_RANK
    w2f = jnp.pad(even_gate_w2_fwd[0], ((0, pad_rows), (0, 0))).astype(BF16)
    w2b = jnp.pad(even_gate_w2_bwd[0], ((GLA_GATE_RANK, pad_rows - GLA_GATE_RANK), (0, 0))).astype(BF16)
    b2f = even_gate_b_fwd[0].reshape(1, key_w)
    b2b = even_gate_b_bwd[0].reshape(1, key_w)

    h_l = _modulate(xl, mod_l(0, 0), mod_l(0, 1), seq)
    h_c = _modulate(xc, mod_c(0, 0), mod_c(0, 1), all_ctx)
    p_l, p_c = _proj(h_l, w_qkvr), _proj(h_c, w_qkvr)
    us_l, us_c = _proj(h_l, w_us), _proj(h_c, w_us)
    g_l, g_c = _proj(h_l, w_gate), _proj(h_c, w_gate)
    gla_c, gla_l = _gla(p_l, p_c, g_l, g_c, w2f, b2f, w2b, b2b, even_gla_norm_gain[0].reshape(1, dv),
                        batch, dk, dv)
    w_s = even_sgu_w_s[0].astype(BF16)
    b_s_t = even_sgu_b_s[0].T
    sgu_gain = even_sgu_norm_gain[0].reshape(1, sgu_w)
    sgu_l = _sgu(us_l, w_s, b_s_t, sgu_gain, sgu_w, 0, 1)
    sgu_c = _sgu(us_c, w_s, b_s_t, sgu_gain, sgu_w, 0, 1)
    assert val_w == sgu_w
    w_out = _cast_col_tiles(even_w_out, 0, 0, d, TN_WIDE)
    x_l = _out_res([gla_l, sgu_l], w_out, xl, mod_l(0, 2), seq)
    x_c = _out_res([gla_c, sgu_c], w_out, xc, mod_c(0, 2), all_ctx)

    hidden = ffn_w_gate.shape[2]
    assert hidden % FFN_TH == 0
    wgu = _cast_col_tile_pairs(ffn_w_gate, ffn_w_up, 0, FFN_TH)
    wd = _cast_rows(ffn_w_down, 0, FFN_TH)
    x_l = _ffn(x_l, mod_l(0, 3), mod_l(0, 4), mod_l(0, 5), wgu, wd, seq)
    x_c = _ffn(x_c, mod_c(0, 3), mod_c(0, 4), mod_c(0, 5), wgu, wd, all_ctx)

    h_l = _modulate(x_l, mod_l(1, 0), mod_l(1, 1), seq)
    h_c = _modulate(x_c, mod_c(1, 0), mod_c(1, 1), all_ctx)

    n_heads = odd_sink.shape[1]
    q_w = n_heads * HEAD_DIM
    kv_w = q_w // ATT_GROUP
    qk_tile = TN_WIDE
    assert kv_w == qk_tile
    w_qk = _cast_col_tiles(odd_w_in, 0, 0, q_w + kv_w, qk_tile)
    w_v = _cast_col_tiles(odd_w_in, 0, q_w + kv_w, kv_w, TN_WIDE)
    k_gain = jnp.tile(odd_k_norm_gain[0], kv_w // HEAD_DIM)
    qk_gain = jnp.concatenate([jnp.tile(odd_q_norm_gain[0] * (HEAD_DIM ** -0.5 * LOG2E), n_heads),
                               k_gain]).reshape(1, -1)
    cos2, sin2 = _rope_tables(seq)
    qk_l = _qk_proj(h_l, w_qk, 0, w_qk.shape[0], qk_gain, cos2, sin2, rope=True)
    v_l = _proj(h_l, w_v)
    k_c = _qk_proj(h_c, w_qk, q_w // qk_tile, 1, k_gain.reshape(1, -1), cos2, sin2, rope=False)
    v_c = _proj(h_c, w_v)
    att = _attention(qk_l, v_l, k_c, v_c, odd_sink[0], batch, n_heads)
    x_l = _out_res([att], _cast_col_tiles(odd_w_out, 0, 0, d, TN_WIDE), x_l, mod_l(1, 2), seq)

    wgu = _cast_col_tile_pairs(ffn_w_gate, ffn_w_up, 1, FFN_TH)
    wd = _cast_rows(ffn_w_down, 1, FFN_TH)
    return _ffn(x_l, mod_l(1, 3), mod_l(1, 4), mod_l(1, 5), wgu, wd, seq).reshape(batch, seq, d)
```

```python
import functools

import numpy as np
import jax
import jax.numpy as jnp
from jax import lax
from jax.experimental import pallas as pl
from jax.experimental.pallas import tpu as pltpu

F32 = jnp.float32
BF16 = jnp.bfloat16

EPS = 1e-6
LOG2E = float(np.log2(np.e))
N_MOD = 6
GRID_W = 64
ROPE_THETA = 10000.0

GLA_HEADS = 4
GLA_TAU = 16.0
GLA_GATE_RANK = 16
GLA_BLOCK = 64
GLA_TILE = 256
SGU_GROUPS = 8
SGU_CHUNK = 128
HEAD_DIM = 128
ATT_GROUP = 4
WINDOW = 128
ATT_BLOCK = 128
ATT_QB = 8

LANE = 128
MXU_W = 256
VMEM_LIMIT = 56 * 1024 * 1024
FFN_VMEM_LIMIT = 60 * 1024 * 1024

TM = 1024
TN = 512
TN_WIDE = 1024
FFN_TH = 256
FFN_X_ROWS = 128
FFN_X_GROUP = 16
ROWS_EW = 256


def _cparams(*sem):
    return pltpu.CompilerParams(dimension_semantics=sem, vmem_limit_bytes=VMEM_LIMIT)


def _dot(a, b):
    return jnp.dot(a, b, preferred_element_type=F32)


def _dot_nt(a, b):
    return lax.dot_general(a, b, (((1,), (1,)), ((), ())), preferred_element_type=F32)


def _dot_tn(a, b):
    return lax.dot_general(a, b, (((0,), (0,)), ((), ())), preferred_element_type=F32)


def _silu(x):
    return x * jax.nn.sigmoid(x)


def _gelu_tanh(x):
    c = np.float32(np.sqrt(2.0 / np.pi))
    return x * (0.5 * (1.0 + jnp.tanh(c * (x + 0.044715 * (x * x * x)))))


def _rms(x):
    return x * lax.rsqrt(jnp.mean(x * x, axis=-1, keepdims=True) + EPS)


def _ada_kernel(c_ref, w_ref, b_ref, o_ref):
    a = _silu(c_ref[...]).astype(BF16)
    o_ref[...] = _dot(a, w_ref[...].astype(BF16)) + b_ref[...]


def _ada(cc, ada_w, ada_b):
    depth, d, n = ada_w.shape
    rows = cc.shape[0]
    return pl.pallas_call(
        _ada_kernel,
        grid=(depth, n // TN),
        in_specs=[pl.BlockSpec((rows, d), lambda l, j: (0, 0)),
                  pl.BlockSpec((None, d, TN), lambda l, j: (l, 0, j)),
                  pl.BlockSpec((None, 1, TN), lambda l, j: (l, 0, j))],
        out_specs=pl.BlockSpec((None, rows, TN), lambda l, j: (l, 0, j)),
        out_shape=jax.ShapeDtypeStruct((depth, rows, n), F32),
        compiler_params=_cparams("parallel", "parallel"),
        name="ada",
    )(cc, ada_w, ada_b.reshape(depth, 1, n))


def _modulate_kernel(x_ref, sh_ref, sc_ref, h_ref):
    h_ref[...] = (_rms(x_ref[...]) * (1.0 + sc_ref[...]) + sh_ref[...]).astype(h_ref.dtype)


def _row_spec(d):
    return pl.BlockSpec((ROWS_EW, d), lambda i: (i, 0))


def _vec_spec(d, rows_per_vec):
    return pl.BlockSpec((None, 1, d), lambda i: ((i * ROWS_EW) // rows_per_vec, 0, 0))


def _modulate(x, shift, scale, rows_per_vec):
    m, d = x.shape
    return pl.pallas_call(
        _modulate_kernel, grid=(m // ROWS_EW,),
        in_specs=[_row_spec(d), _vec_spec(d, rows_per_vec), _vec_spec(d, rows_per_vec)],
        out_specs=_row_spec(d),
        out_shape=jax.ShapeDtypeStruct((m, d), BF16),
        compiler_params=_cparams("parallel"), name="modulate",
    )(x, shift, scale)


def _cast_kernel(w_ref, o_ref):
    o_ref[...] = w_ref[...].astype(o_ref.dtype)


def _cast_col_tiles(w, layer, col0, n_cols, tile):
    k = w.shape[1]
    assert col0 % tile == 0 and n_cols % tile == 0
    t0 = col0 // tile
    return pl.pallas_call(
        _cast_kernel, grid=(n_cols // tile,),
        in_specs=[pl.BlockSpec((None, k, tile), lambda j: (layer, 0, t0 + j))],
        out_specs=pl.BlockSpec((None, k, tile), lambda j: (j, 0, 0)),
        out_shape=jax.ShapeDtypeStruct((n_cols // tile, k, tile), BF16),
        compiler_params=_cparams("parallel"), name="cast_cols",
    )(w)


def _cast_pair_kernel(a_ref, b_ref, o_ref):
    tile = a_ref.shape[1]
    o_ref[:, :tile] = a_ref[...].astype(o_ref.dtype)
    o_ref[:, tile:] = b_ref[...].astype(o_ref.dtype)


def _cast_col_tile_pairs(wa, wb, layer, tile):
    k, n = wa.shape[1:]
    src = pl.BlockSpec((None, k, tile), lambda j: (layer, 0, j))
    return pl.pallas_call(
        _cast_pair_kernel, grid=(n // tile,),
        in_specs=[src, src],
        out_specs=pl.BlockSpec((None, k, 2 * tile), lambda j: (j, 0, 0)),
        out_shape=jax.ShapeDtypeStruct((n // tile, k, 2 * tile), BF16),
        compiler_params=_cparams("parallel"), name="cast_pairs",
    )(wa, wb)


def _cast_row_tiles(wt, layer, row0, n_rows, tile):
    k = wt.shape[2]
    assert row0 % 8 == 0 and n_rows % tile == 0
    return pl.pallas_call(
        _cast_kernel, grid=(n_rows // tile,),
        in_specs=[pl.BlockSpec((pl.Element(1), pl.Element(tile), pl.Element(k)),
                               lambda j: (layer, pl.multiple_of(row0 + j * tile, 8), 0))],
        out_specs=pl.BlockSpec((1, tile, k), lambda j: (j, 0, 0)),
        out_shape=jax.ShapeDtypeStruct((n_rows // tile, tile, k), BF16),
        compiler_params=_cparams("parallel"), name="cast_row_tiles",
    )(wt)


def _cast_rows(w, layer, tile):
    k, n = w.shape[1:]
    return pl.pallas_call(
        _cast_kernel, grid=(k // tile,),
        in_specs=[pl.BlockSpec((None, tile, n), lambda j: (layer, j, 0))],
        out_specs=pl.BlockSpec((tile, n), lambda j: (j, 0)),
        out_shape=jax.ShapeDtypeStruct((k, n), BF16),
        compiler_params=_cparams("parallel"), name="cast_rows",
    )(w)


def _proj_kernel(x_ref, w_ref, o_ref, *, transposed):
    dot = _dot_nt if transposed else _dot
    o_ref[...] = dot(x_ref[...], w_ref[...]).astype(o_ref.dtype)


def _proj(x, w3, transposed=False):
    m, k = x.shape
    nt = w3.shape[0]
    tn = w3.shape[1] if transposed else w3.shape[2]
    tm = min(TM, m)
    return pl.pallas_call(
        functools.partial(_proj_kernel, transposed=transposed), grid=(m // tm, nt),
        in_specs=[pl.BlockSpec((tm, k), lambda i, j: (i, 0)),
                  pl.BlockSpec((None,) + w3.shape[1:], lambda i, j: (j, 0, 0))],
        out_specs=pl.BlockSpec((tm, tn), lambda i, j: (i, j)),
        out_shape=jax.ShapeDtypeStruct((m, nt * tn), BF16),
        compiler_params=_cparams("parallel", "parallel"), name="proj",
    )(x, w3)


def _qk_proj_kernel(x_ref, w_ref, gain_ref, cos_ref, sin_ref, ones_ref, o_ref, *, rope):
    half = x_ref.shape[0] // 2
    for rr in range(2):
        rows = slice(rr * half, (rr + 1) * half)
        x = x_ref[rows, :]
        for cc in range(w_ref.shape[1] // TN):
            acc = _dot(x, w_ref[:, cc * TN:(cc + 1) * TN])
            for tt in range(TN // MXU_W):
                c0 = cc * TN + tt * MXU_W
                a = acc[:, tt * MXU_W:(tt + 1) * MXU_W]
                ssq = _dot((a * a).astype(BF16), ones_ref[...])
                y = a * lax.rsqrt(ssq * (1.0 / HEAD_DIM) + EPS) * gain_ref[:, c0:c0 + MXU_W]
                for hh in range(MXU_W // HEAD_DIM):
                    yh = y[:, hh * HEAD_DIM:(hh + 1) * HEAD_DIM]
                    if rope:
                        yh = (yh * cos_ref[rows, :]
                              + pltpu.roll(yh, HEAD_DIM // 2, 1) * sin_ref[rows, :])
                    o_ref[rows, c0 + hh * HEAD_DIM:c0 + (hh + 1) * HEAD_DIM] = yh.astype(o_ref.dtype)


def _qk_proj(x, w3, tile0, n_tiles, gain, cos2, sin2, rope):
    m, k = x.shape
    tn = w3.shape[2]
    tm = min(TM, m)
    seq_blocks = cos2.shape[0] // tm if rope else 1
    tab = pl.BlockSpec((tm, HEAD_DIM), lambda i, j: (i % seq_blocks, 0))
    head_of = np.arange(MXU_W) // HEAD_DIM
    ones = jnp.asarray(head_of[:, None] == head_of[None, :], BF16)
    return pl.pallas_call(
        functools.partial(_qk_proj_kernel, rope=rope), grid=(m // tm, n_tiles),
        in_specs=[pl.BlockSpec((tm, k), lambda i, j: (i, 0)),
                  pl.BlockSpec((None, k, tn), lambda i, j: (tile0 + j, 0, 0)),
                  pl.BlockSpec((1, tn), lambda i, j: (0, j)),
                  tab, tab,
                  pl.BlockSpec((MXU_W, MXU_W), lambda i, j: (0, 0))],
        out_specs=pl.BlockSpec((tm, tn), lambda i, j: (i, j)),
        out_shape=jax.ShapeDtypeStruct((m, n_tiles * tn), BF16),
        compiler_params=_cparams("parallel", "parallel"), name="qk_proj",
    )(x, w3, gain, cos2, sin2, ones)


def _out_res_kernel(*refs, n_in):
    xs, ws = refs[:n_in], refs[n_in:2 * n_in]
    res_ref, g_ref, o_ref = refs[2 * n_in:]
    acc = _dot(xs[0][...], ws[0][...])
    for x_ref, w_ref in zip(xs[1:], ws[1:]):
        acc = acc + _dot(x_ref[...], w_ref[...])
    o_ref[...] = res_ref[...] + g_ref[...] * acc


def _out_res(xs, w3, res, gate, rows_per_vec):
    m, n = res.shape
    nt, k, tn = w3.shape
    tm = min(TM, m)
    n_in = len(xs)
    kx = k // n_in
    assert all(x.shape[1] == kx for x in xs) and nt * tn == n

    def w_spec(part):
        return pl.BlockSpec((None, kx, tn), lambda i, j: (j, part, 0))

    in_specs = ([pl.BlockSpec((tm, kx), lambda i, j: (i, 0)) for _ in xs]
                + [w_spec(part) for part in range(n_in)]
                + [pl.BlockSpec((tm, tn), lambda i, j: (i, j)),
                   pl.BlockSpec((None, 1, tn), lambda i, j: ((i * tm) // rows_per_vec, 0, j))])
    return pl.pallas_call(
        functools.partial(_out_res_kernel, n_in=n_in), grid=(m // tm, nt),
        in_specs=in_specs,
        out_specs=pl.BlockSpec((tm, tn), lambda i, j: (i, j)),
        out_shape=jax.ShapeDtypeStruct((m, n), F32),
        compiler_params=_cparams("parallel", "parallel"), name="out_res",
    )(*xs, *([w3] * n_in), res, gate)


def _ffn_kernel(x_hbm, sh_ref, sc_ref, gate_ref, wgu_ref, wd_ref, o_ref, h_ref, xbuf_ref, sem):
    i, j = pl.program_id(0), pl.program_id(1)
    tm = o_ref.shape[0]
    n_chunks = tm // FFN_X_ROWS

    def x_copy(ci, slot):
        src = x_hbm.at[pl.ds(i * tm + ci * FFN_X_ROWS, FFN_X_ROWS), :]
        return pltpu.make_async_copy(src, xbuf_ref.at[slot], sem.at[slot])

    @pl.when(j == 0)
    def _():
        x_copy(0, 0).start()
        for ci in range(n_chunks):
            slot = ci % 2
            if ci + 1 < n_chunks:
                x_copy(ci + 1, 1 - slot).start()
            x_copy(ci, slot).wait()

            def group(gi, carry, ci=ci, slot=slot):
                src = pl.ds(pl.multiple_of(gi * FFN_X_GROUP, FFN_X_GROUP), FFN_X_GROUP)
                dst = pl.ds(pl.multiple_of(ci * FFN_X_ROWS + gi * FFN_X_GROUP, FFN_X_GROUP), FFN_X_GROUP)
                xg = xbuf_ref[slot, src, :]
                o_ref[dst, :] = xg
                h_ref[dst, :] = (_rms(xg) * (1.0 + sc_ref[...]) + sh_ref[...]).astype(BF16)
                return carry
            lax.fori_loop(0, FFN_X_ROWS // FFN_X_GROUP, group, 0, unroll=2)

    gu = _dot(h_ref[...], wgu_ref[...])
    a = (_silu(gu[:, :FFN_TH]) * gu[:, FFN_TH:]).astype(BF16)
    o_ref[...] += gate_ref[...] * _dot(a, wd_ref[...])


def _ffn(x, shift, scale, gate, wgu3, wd, rows_per_vec):
    m, d = x.shape
    n_tiles = wgu3.shape[0]
    tm = min(TM, m)
    vec = pl.BlockSpec((None, 1, d), lambda i, j: ((i * tm) // rows_per_vec, 0, 0))
    return pl.pallas_call(
        _ffn_kernel, grid=(m // tm, n_tiles),
        in_specs=[pl.BlockSpec(memory_space=pl.ANY),
                  vec, vec, vec,
                  pl.BlockSpec((None, d, 2 * FFN_TH), lambda i, j: (j, 0, 0)),
                  pl.BlockSpec((FFN_TH, d), lambda i, j: (j, 0))],
        out_specs=pl.BlockSpec((tm, d), lambda i, j: (i, 0)),
        out_shape=jax.ShapeDtypeStruct((m, d), F32),
        scratch_shapes=[pltpu.VMEM((tm, d), BF16), pltpu.VMEM((2, FFN_X_ROWS, d), F32),
                        pltpu.SemaphoreType.DMA((2,))],
        compiler_params=pltpu.CompilerParams(dimension_semantics=("parallel", "arbitrary"),
                                             vmem_limit_bytes=FFN_VMEM_LIMIT), name="ffn",
    )(x, shift, scale, gate, wgu3, wd)


def _gla_scan_tile(q_ref, k_ref, v_ref, g_ref, w2_ref, b2_ref, s_ref, emit, reverse):
    dk = q_ref.shape[1] // GLA_HEADS
    dv = v_ref.shape[1] // GLA_HEADS
    row = lax.broadcasted_iota(jnp.int32, (GLA_BLOCK, GLA_BLOCK), 0)
    col = lax.broadcasted_iota(jnp.int32, (GLA_BLOCK, GLA_BLOCK), 1)
    if reverse:
        tri = (col >= row).astype(BF16)
        mask = col > row
        last_row = 0
    else:
        tri = (col <= row).astype(BF16)
        mask = col <= row
        last_row = GLA_BLOCK - 1
    n_steps = q_ref.shape[0] // GLA_BLOCK
    order = [n_steps - 1 - st if reverse else st for st in range(n_steps)]

    parts = []
    for c in order:
        rows = slice(c * GLA_BLOCK, (c + 1) * GLA_BLOCK)
        q = q_ref[rows, :].astype(F32) * (dk ** -0.5)
        k = k_ref[rows, :].astype(F32)
        x = _dot(g_ref[rows, :], w2_ref[...]) + b2_ref[...]
        log_a = (jnp.minimum(x, 0.0) - jnp.log(1.0 + jnp.exp(-jnp.abs(x)))) * (1.0 / GLA_TAU)
        hi = log_a.astype(BF16)
        lo = (log_a - hi.astype(F32)).astype(BF16)
        b = _dot(tri, hi) + _dot(tri, lo)
        mid = b[GLA_BLOCK // 2:GLA_BLOCK // 2 + 1, :]
        b_last = b[last_row:last_row + 1, :]
        qe = q * jnp.exp(b - mid)
        ke = k * jnp.exp(mid - b)
        q_in = (qe * jnp.exp(mid)).astype(BF16)
        k_out = (ke * jnp.exp(b_last - mid)).astype(BF16)
        decay = jnp.exp(b_last)
        qe, ke = qe.astype(BF16), ke.astype(BF16)
        intra, kv = [], []
        for h in range(GLA_HEADS):
            ks = slice(h * dk, (h + 1) * dk)
            v = v_ref[rows, h * dv:(h + 1) * dv]
            att = jnp.where(mask, _dot_nt(qe[:, ks], ke[:, ks]), 0.0).astype(BF16)
            intra.append(_dot(att, v))
            kv.append(_dot_tn(v, k_out[:, ks]))
        parts.append((rows, q_in, decay, intra, kv))

    for h in range(GLA_HEADS):
        ks = slice(h * dk, (h + 1) * dk)
        s = s_ref[h]
        for rows, q_in, decay, intra, kv in parts:
            emit(rows, h, intra[h] + _dot_nt(q_in[:, ks], s.astype(BF16)))
            s = s * decay[:, ks] + kv[h]
        s_ref[h] = s


def _gla_rev_kernel(qc_ref, kc_ref, vc_ref, gc_ref, ql_ref, kl_ref, vl_ref, gl_ref, w2_ref, b2_ref,
                    oc_ref, ol_ref, s_ref):
    t = pl.program_id(1)
    dv = vc_ref.shape[1] // GLA_HEADS

    def run(q_ref, k_ref, v_ref, g_ref, o_ref):
        def emit(rows, h, o):
            o_ref[rows, h * dv:(h + 1) * dv] = o
        _gla_scan_tile(q_ref, k_ref, v_ref, g_ref, w2_ref, b2_ref, s_ref, emit, reverse=True)

    @pl.when(t == 0)
    def _():
        s_ref[...] = jnp.zeros_like(s_ref)
        run(qc_ref, kc_ref, vc_ref, gc_ref, oc_ref)

    @pl.when(t > 0)
    def _():
        run(ql_ref, kl_ref, vl_ref, gl_ref, ol_ref)


def _gla_fwd_kernel(qc_ref, kc_ref, vc_ref, rc_ref, gc_ref, pc_ref, ql_ref, kl_ref, vl_ref, rl_ref, gl_ref, pl_ref,
                    w2_ref, b2_ref, gain_ref, oc_ref, ol_ref, s_ref):
    t = pl.program_id(1)
    dv = vc_ref.shape[1] // GLA_HEADS

    def run(q_ref, k_ref, v_ref, r_ref, g_ref, p_ref, o_ref):
        def emit(rows, h, o):
            cols = slice(h * dv, (h + 1) * dv)
            total = o + p_ref[rows, cols]
            r = r_ref[rows, cols].astype(F32)
            o_ref[rows, cols] = (_rms(total) * gain_ref[...] * _silu(r)).astype(o_ref.dtype)
        _gla_scan_tile(q_ref, k_ref, v_ref, g_ref, w2_ref, b2_ref, s_ref, emit, reverse=False)

    @pl.when(t == 0)
    def _():
        s_ref[...] = jnp.zeros_like(s_ref)
        run(qc_ref, kc_ref, vc_ref, rc_ref, gc_ref, pc_ref, oc_ref)

    @pl.when(t > 0)
    def _():
        run(ql_ref, kl_ref, vl_ref, rl_ref, gl_ref, pl_ref, ol_ref)


def _gla(p_l, p_c, g_l, g_c, w2f, b2f, w2b, b2b, gain, batch, dk, dv):
    seq, ctx_len = p_l.shape[0] // batch, p_c.shape[0] // batch
    assert ctx_len == GLA_TILE and seq % GLA_TILE == 0
    n_lat = seq // GLA_TILE
    key_w, val_w = GLA_HEADS * dk, GLA_HEADS * dv
    assert val_w == 2 * key_w

    def lat_fwd(b, t):
        return b * n_lat + jnp.maximum(t - 1, 0)

    def lat_rev(b, t):
        return b * n_lat + n_lat - jnp.maximum(t, 1)

    def tok_specs(row_blk, with_r):
        specs = [pl.BlockSpec((GLA_TILE, key_w), lambda b, t: (row_blk(b, t), 0)),
                 pl.BlockSpec((GLA_TILE, key_w), lambda b, t: (row_blk(b, t), 1)),
                 pl.BlockSpec((GLA_TILE, val_w), lambda b, t: (row_blk(b, t), 1))]
        if with_r:
            specs.append(pl.BlockSpec((GLA_TILE, val_w), lambda b, t: (row_blk(b, t), 2)))
        return specs + [pl.BlockSpec((GLA_TILE, LANE), lambda b, t: (row_blk(b, t), 0))]

    def out_spec(row_blk):
        return pl.BlockSpec((GLA_TILE, val_w), lambda b, t: (row_blk(b, t), 0))

    def ctx_blk(b, t):
        return b

    w2_spec = pl.BlockSpec((LANE, key_w), lambda b, t: (0, 0))
    b2_spec = pl.BlockSpec((1, key_w), lambda b, t: (0, 0))
    state = pltpu.VMEM((GLA_HEADS, dv, dk), F32)
    grid = (batch, 1 + n_lat)
    rev_c, rev_l = pl.pallas_call(
        _gla_rev_kernel, grid=grid,
        in_specs=tok_specs(ctx_blk, False) + tok_specs(lat_rev, False) + [w2_spec, b2_spec],
        out_specs=[out_spec(ctx_blk), out_spec(lat_rev)],
        out_shape=[jax.ShapeDtypeStruct((batch * ctx_len, val_w), F32),
                   jax.ShapeDtypeStruct((batch * seq, val_w), F32)],
        scratch_shapes=[state],
        compiler_params=_cparams("parallel", "arbitrary"), name="gla_rev",
    )(p_c, p_c, p_c, g_c, p_l, p_l, p_l, g_l, w2b, b2b)
    return pl.pallas_call(
        _gla_fwd_kernel, grid=grid,
        in_specs=tok_specs(ctx_blk, True) + [out_spec(ctx_blk)] + tok_specs(lat_fwd, True) + [out_spec(lat_fwd)]
        + [w2_spec, b2_spec, pl.BlockSpec((1, dv), lambda b, t: (0, 0))],
        out_specs=[out_spec(ctx_blk), out_spec(lat_fwd)],
        out_shape=[jax.ShapeDtypeStruct((batch * ctx_len, val_w), BF16),
                   jax.ShapeDtypeStruct((batch * seq, val_w), BF16)],
        scratch_shapes=[state],
        compiler_params=_cparams("parallel", "arbitrary"), name="gla_fwd",
    )(p_c, p_c, p_c, p_c, g_c, rev_c, p_l, p_l, p_l, p_l, g_l, rev_l, w2f, b2f, gain)


def _sgu_kernel(u_ref, s_ref, ws_ref, bs_ref, gain_ref, o_ref):
    gw = u_ref.shape[1] // SGU_GROUPS
    for ci in range(u_ref.shape[0] // SGU_CHUNK):
        rows = slice(ci * SGU_CHUNK, (ci + 1) * SGU_CHUNK)
        for g in range(SGU_GROUPS):
            cols = slice(g * gw, (g + 1) * gw)
            vg = _rms(_gelu_tanh(s_ref[rows, cols].astype(F32))) * gain_ref[:, cols]
            mixed = _dot(ws_ref[g], vg.astype(BF16)) + bs_ref[:, g:g + 1]
            o_ref[rows, cols] = (_gelu_tanh(u_ref[rows, cols].astype(F32)) * mixed).astype(o_ref.dtype)


def _sgu(p, w_s, b_s_t, gain, width, u_blk, s_blk):
    m = p.shape[0]
    rows = 2 * SGU_CHUNK
    return pl.pallas_call(
        _sgu_kernel, grid=(m // rows,),
        in_specs=[pl.BlockSpec((rows, width), lambda i: (i, u_blk)),
                  pl.BlockSpec((rows, width), lambda i: (i, s_blk)),
                  pl.BlockSpec(w_s.shape, lambda i: (0, 0, 0)),
                  pl.BlockSpec(b_s_t.shape, lambda i: (0, 0)),
                  pl.BlockSpec((1, width), lambda i: (0, 0))],
        out_specs=pl.BlockSpec((rows, width), lambda i: (i, 0)),
        out_shape=jax.ShapeDtypeStruct((m, width), BF16),
        compiler_params=_cparams("parallel"), name="sgu",
    )(p, p, w_s, b_s_t, gain)


def _attn_kernel(sink_ref, bias_ref, q_ref, k_ref, v_ref, kc_ref, vc_ref, o_ref):
    kv, ng = pl.program_id(1), pl.program_id(2)
    seq = k_ref.shape[0]
    nb = seq // ATT_BLOCK
    band = 3 * ATT_BLOCK
    sink = jnp.concatenate([jnp.full((ATT_BLOCK, HEAD_DIM), sink_ref[kv * ATT_GROUP + g] * LOG2E, F32)
                            for g in range(ATT_GROUP)], axis=0)
    k_ctx, v_ctx = kc_ref[...], vc_ref[...]
    ones = jnp.ones((band + k_ctx.shape[0], HEAD_DIM), BF16)
    for jj in range(ATT_QB):
        n = ng * ATT_QB + jj
        start = pl.multiple_of(jnp.clip((n - 1) * ATT_BLOCK, 0, seq - band), ATT_BLOCK)
        if jj == 0:
            bias = bias_ref[jnp.where(n == 0, 1, 0)]
        elif jj == ATT_QB - 1:
            bias = bias_ref[jnp.where(n == nb - 1, 2, 0)]
        else:
            bias = bias_ref[0]
        q = q_ref[jj * ATT_BLOCK:(jj + 1) * ATT_BLOCK, :]
        qs = jnp.concatenate([q[:, g * HEAD_DIM:(g + 1) * HEAD_DIM] for g in range(ATT_GROUP)], axis=0)
        keys = jnp.concatenate([k_ref[pl.ds(start, band), :], k_ctx], axis=0)
        vals = jnp.concatenate([v_ref[pl.ds(start, band), :], v_ctx], axis=0)
        s = _dot_nt(qs, keys)
        s = jnp.concatenate([s[:, :band] + bias, s[:, band:]], axis=1)
        m = jnp.maximum(sink, jnp.max(s, axis=-1, keepdims=True))
        p = jnp.exp2(s - jnp.concatenate([m] * (s.shape[1] // HEAD_DIM), axis=1)).astype(BF16)
        ov = _dot(p, jnp.concatenate([vals, ones], axis=1))
        o = ov[:, :HEAD_DIM] / (ov[:, HEAD_DIM:] + jnp.exp2(sink - m))
        o_ref[jj * ATT_BLOCK:(jj + 1) * ATT_BLOCK, :] = jnp.concatenate(
            [o[g * ATT_BLOCK:(g + 1) * ATT_BLOCK] for g in range(ATT_GROUP)], axis=1).astype(o_ref.dtype)


def _band_bias():
    t = (np.arange(ATT_GROUP * ATT_BLOCK) % ATT_BLOCK)[:, None]
    i = np.arange(3 * ATT_BLOCK)[None, :]
    shifts = (-ATT_BLOCK, 0, -2 * ATT_BLOCK)
    return jnp.asarray(np.stack([np.where(np.abs(i + sh - t) <= WINDOW, 0.0, -np.inf) for sh in shifts]), F32)


def _attention(qk, v, k_c, v_c, sink, batch, n_q_heads):
    m = qk.shape[0]
    seq, ctx_len = m // batch, k_c.shape[0] // batch
    kv_heads = n_q_heads // ATT_GROUP
    ng = seq // (ATT_BLOCK * ATT_QB)
    assert ng >= 2
    qw = ATT_GROUP * HEAD_DIM
    bias = _band_bias()
    return pl.pallas_call(
        _attn_kernel, grid=(batch, kv_heads, ng),
        in_specs=[pl.BlockSpec(memory_space=pltpu.SMEM),
                  pl.BlockSpec(bias.shape, lambda b, h, n: (0, 0, 0)),
                  pl.BlockSpec((ATT_QB * ATT_BLOCK, qw), lambda b, h, n: (b * ng + n, h)),
                  pl.BlockSpec((seq, HEAD_DIM), lambda b, h, n: (b, n_q_heads + h)),
                  pl.BlockSpec((seq, HEAD_DIM), lambda b, h, n: (b, h)),
                  pl.BlockSpec((ctx_len, HEAD_DIM), lambda b, h, n: (b, h)),
                  pl.BlockSpec((ctx_len, HEAD_DIM), lambda b, h, n: (b, h))],
        out_specs=pl.BlockSpec((ATT_QB * ATT_BLOCK, qw), lambda b, h, n: (b * ng + n, h)),
        out_shape=jax.ShapeDtypeStruct((m, n_q_heads * HEAD_DIM), BF16),
        compiler_params=_cparams("parallel", "parallel", "arbitrary"), name="attention",
    )(sink, bias, qk, qk, v, k_c, v_c)


def _rope_tables(seq):
    rows = seq // GRID_W
    row = jnp.repeat(jnp.arange(rows, dtype=F32), GRID_W)
    col = jnp.tile(jnp.arange(GRID_W, dtype=F32), rows)
    n_freq = HEAD_DIM // 4
    inv_freq = ROPE_THETA ** (-jnp.arange(n_freq, dtype=F32) / n_freq)
    ang = jnp.concatenate([row[:, None] * inv_freq, col[:, None] * inv_freq], axis=-1)
    cos, sin = jnp.cos(ang), jnp.sin(ang)
    return jnp.concatenate([cos, cos], axis=-1), jnp.concatenate([-sin, sin], axis=-1)


def kernel(x, c, ctx, c_ctx, ada_w, ada_b, ffn_w_gate, ffn_w_up, ffn_w_down, even_w_in, even_gate_w2_fwd, even_gate_b_fwd, even_gate_w2_bwd, even_gate_b_bwd, even_gla_norm_gain, even_sgu_norm_gain, even_sgu_w_s, even_sgu_b_s, even_w_out, odd_w_in, odd_q_norm_gain, odd_k_norm_gain, odd_sink, odd_w_out):
    batch, seq, d = x.shape
    ctx_len = ctx.shape[1]
    depth = ada_w.shape[0]
    assert depth == 2 and batch + 1 <= 8
    xl = x.reshape(batch * seq, d)
    xc = ctx.reshape(batch * ctx_len, d)

    cc = jnp.concatenate([c, c_ctx[None], jnp.zeros((8 - batch - 1, d), F32)], axis=0)
    mods = _ada(cc, ada_w, ada_b)

    def mod_l(layer, which):
        return mods[layer, :batch, which * d:(which + 1) * d].reshape(batch, 1, d)

    def mod_c(layer, which):
        return mods[layer, batch:batch + 1, which * d:(which + 1) * d].reshape(1, 1, d)

    all_ctx = batch * ctx_len

    dv = even_gla_norm_gain.shape[1]
    key_w = even_gate_w2_fwd.shape[2]
    dk = key_w // GLA_HEADS
    val_w = GLA_HEADS * dv
    sgu_w = even_sgu_norm_gain.shape[1]
    gate0 = 2 * key_w + 2 * val_w
    gate1 = gate0 + 2 * GLA_GATE_RANK
    w_in_t = jnp.swapaxes(even_w_in, 1, 2)
    w_qkvr = _cast_row_tiles(w_in_t, 0, 0, gate0, TN_WIDE)
    w_us = _cast_row_tiles(w_in_t, 0, gate1, 2 * sgu_w, TN_WIDE)
    w_gate = jnp.pad(w_in_t[0, gate0:gate1], ((0, LANE - 2 * GLA_GATE_RANK), (0, 0))).astype(BF16)[None]
    pad_rows = LANE - GLA_GATE_RANK
    w2f = jnp.pad(even_gate_w2_fwd[0], ((0, pad_rows), (0, 0))).astype(BF16)
    w2b = jnp.pad(even_gate_w2_bwd[0], ((GLA_GATE_RANK, pad_rows - GLA_GATE_RANK), (0, 0))).astype(BF16)
    b2f = even_gate_b_fwd[0].reshape(1, key_w)
    b2b = even_gate_b_bwd[0].reshape(1, key_w)

    h_l = _modulate(xl, mod_l(0, 0), mod_l(0, 1), seq)
    h_c = _modulate(xc, mod_c(0, 0), mod_c(0, 1), all_ctx)
    p_l, p_c = _proj(h_l, w_qkvr, True), _proj(h_c, w_qkvr, True)
    us_l, us_c = _proj(h_l, w_us, True), _proj(h_c, w_us, True)
    g_l, g_c = _proj(h_l, w_gate, True), _proj(h_c, w_gate, True)
    gla_c, gla_l = _gla(p_l, p_c, g_l, g_c, w2f, b2f, w2b, b2b, even_gla_norm_gain[0].reshape(1, dv),
                        batch, dk, dv)
    w_s = even_sgu_w_s[0].astype(BF16)
    b_s_t = even_sgu_b_s[0].T
    sgu_gain = even_sgu_norm_gain[0].reshape(1, sgu_w)
    sgu_l = _sgu(us_l, w_s, b_s_t, sgu_gain, sgu_w, 0, 1)
    sgu_c = _sgu(us_c, w_s, b_s_t, sgu_gain, sgu_w, 0, 1)
    assert val_w == sgu_w
    w_out = _cast_col_tiles(even_w_out, 0, 0, d, TN_WIDE)
    x_l = _out_res([gla_l, sgu_l], w_out, xl, mod_l(0, 2), seq)
    x_c = _out_res([gla_c, sgu_c], w_out, xc, mod_c(0, 2), all_ctx)

    hidden = ffn_w_gate.shape[2]
    assert hidden % FFN_TH == 0
    wgu = _cast_col_tile_pairs(ffn_w_gate, ffn_w_up, 0, FFN_TH)
    wd = _cast_rows(ffn_w_down, 0, FFN_TH)
    x_l = _ffn(x_l, mod_l(0, 3), mod_l(0, 4), mod_l(0, 5), wgu, wd, seq)
    x_c = _ffn(x_c, mod_c(0, 3), mod_c(0, 4), mod_c(0, 5), wgu, wd, all_ctx)

    h_l = _modulate(x_l, mod_l(1, 0), mod_l(1, 1), seq)
    h_c = _modulate(x_c, mod_c(1, 0), mod_c(1, 1), all_ctx)

    n_heads = odd_sink.shape[1]
    q_w = n_heads * HEAD_DIM
    kv_w = q_w // ATT_GROUP
    qk_tile = TN_WIDE
    assert kv_w == qk_tile
    w_qk = _cast_col_tiles(odd_w_in, 0, 0, q_w + kv_w, qk_tile)
    w_v = _cast_col_tiles(odd_w_in, 0, q_w + kv_w, kv_w, TN_WIDE)
    k_gain = jnp.tile(odd_k_norm_gain[0], kv_w // HEAD_DIM)
    qk_gain = jnp.concatenate([jnp.tile(odd_q_norm_gain[0] * (HEAD_DIM ** -0.5 * LOG2E), n_heads),
                               k_gain]).reshape(1, -1)
    cos2, sin2 = _rope_tables(seq)
    qk_l = _qk_proj(h_l, w_qk, 0, w_qk.shape[0], qk_gain, cos2, sin2, rope=True)
    v_l = _proj(h_l, w_v)
    k_c = _qk_proj(h_c, w_qk, q_w // qk_tile, 1, k_gain.reshape(1, -1), cos2, sin2, rope=False)
    v_c = _proj(h_c, w_v)
    att = _attention(qk_l, v_l, k_c, v_c, odd_sink[0], batch, n_heads)
    x_l = _out_res([att], _cast_col_tiles(odd_w_out, 0, 0, d, TN_WIDE), x_l, mod_l(1, 2), seq)

    wgu = _cast_col_tile_pairs(ffn_w_gate, ffn_w_up, 1, FFN_TH)
    wd = _cast_rows(ffn_w_down, 1, FFN_TH)
    return _ffn(x_l, mod_l(1, 3), mod_l(1, 4), mod_l(1, 5), wgu, wd, seq).reshape(batch, seq, d)
```

```python
import functools

import numpy as np
import jax
import jax.numpy as jnp
from jax import lax
from jax.experimental import pallas as pl
from jax.experimental.pallas import tpu as pltpu

F32 = jnp.float32
BF16 = jnp.bfloat16

EPS = 1e-6
LOG2E = float(np.log2(np.e))
N_MOD = 6
GRID_W = 64
ROPE_THETA = 10000.0

GLA_HEADS = 4
GLA_TAU = 16.0
GLA_GATE_RANK = 16
GLA_BLOCK = 64
GLA_TILE = 256
SGU_GROUPS = 8
SGU_CHUNK = 128
HEAD_DIM = 128
ATT_GROUP = 4
WINDOW = 128
ATT_BLOCK = 128
ATT_QB = 8

LANE = 128
MXU_W = 256
VMEM_LIMIT = 56 * 1024 * 1024
FFN_VMEM_LIMIT = 60 * 1024 * 1024

TM = 1024
TN = 512
TN_WIDE = 1024
FFN_TH = 256
X_ROWS = 128
X_GROUP = 16


def _cparams(*sem):
    return pltpu.CompilerParams(dimension_semantics=sem, vmem_limit_bytes=VMEM_LIMIT)


def _dot(a, b):
    return jnp.dot(a, b, preferred_element_type=F32)


def _dot_nt(a, b):
    return lax.dot_general(a, b, (((1,), (1,)), ((), ())), preferred_element_type=F32)


def _dot_tn(a, b):
    return lax.dot_general(a, b, (((0,), (0,)), ((), ())), preferred_element_type=F32)


def _silu(x):
    return x * jax.nn.sigmoid(x)


def _gelu_tanh(x):
    c = np.float32(np.sqrt(2.0 / np.pi))
    return x * (0.5 * (1.0 + jnp.tanh(c * (x + 0.044715 * (x * x * x)))))


def _rms(x):
    return x * lax.rsqrt(jnp.mean(x * x, axis=-1, keepdims=True) + EPS)


def _ada_kernel(c_ref, w_ref, b_ref, o_ref):
    a = _silu(c_ref[...]).astype(BF16)
    o_ref[...] = _dot(a, w_ref[...].astype(BF16)) + b_ref[...]


def _ada(cc, ada_w, ada_b):
    depth, d, n = ada_w.shape
    rows = cc.shape[0]
    return pl.pallas_call(
        _ada_kernel,
        grid=(depth, n // TN),
        in_specs=[pl.BlockSpec((rows, d), lambda l, j: (0, 0)),
                  pl.BlockSpec((None, d, TN), lambda l, j: (l, 0, j)),
                  pl.BlockSpec((None, 1, TN), lambda l, j: (l, 0, j))],
        out_specs=pl.BlockSpec((None, rows, TN), lambda l, j: (l, 0, j)),
        out_shape=jax.ShapeDtypeStruct((depth, rows, n), F32),
        compiler_params=_cparams("parallel", "parallel"),
        name="ada",
    )(cc, ada_w, ada_b.reshape(depth, 1, n))


def _stream_modulate(x_hbm, row0, sh_ref, sc_ref, h_ref, xbuf_ref, sem):
    n_chunks = h_ref.shape[0] // X_ROWS

    def x_copy(ci, slot):
        return pltpu.make_async_copy(x_hbm.at[pl.ds(row0 + ci * X_ROWS, X_ROWS), :], xbuf_ref.at[slot], sem.at[slot])

    x_copy(0, 0).start()
    for ci in range(n_chunks):
        slot = ci % 2
        if ci + 1 < n_chunks:
            x_copy(ci + 1, 1 - slot).start()
        x_copy(ci, slot).wait()

        def group(gi, carry, ci=ci, slot=slot):
            src = pl.ds(pl.multiple_of(gi * X_GROUP, X_GROUP), X_GROUP)
            dst = pl.ds(pl.multiple_of(ci * X_ROWS + gi * X_GROUP, X_GROUP), X_GROUP)
            h_ref[dst, :] = (_rms(xbuf_ref[slot, src, :]) * (1.0 + sc_ref[...]) + sh_ref[...]).astype(h_ref.dtype)
            return carry
        lax.fori_loop(0, X_ROWS // X_GROUP, group, 0, unroll=True)


def _modulate_specs(tm, d, rows_per_vec):
    vec = pl.BlockSpec((None, 1, d), lambda i, j: ((i * tm) // rows_per_vec, 0, 0))
    return ([pl.BlockSpec(memory_space=pl.ANY), vec, vec],
            pl.BlockSpec((tm, d), lambda i, j: (i, 0)),
            [pltpu.VMEM((2, X_ROWS, d), F32), pltpu.SemaphoreType.DMA((2,))])


def _cast_kernel(w_ref, o_ref):
    o_ref[...] = w_ref[...].astype(o_ref.dtype)


def _cast_col_tiles(w, layer, col0, n_cols, tile):
    k = w.shape[1]
    assert col0 % tile == 0 and n_cols % tile == 0
    t0 = col0 // tile
    return pl.pallas_call(
        _cast_kernel, grid=(n_cols // tile,),
        in_specs=[pl.BlockSpec((None, k, tile), lambda j: (layer, 0, t0 + j))],
        out_specs=pl.BlockSpec((None, k, tile), lambda j: (j, 0, 0)),
        out_shape=jax.ShapeDtypeStruct((n_cols // tile, k, tile), BF16),
        compiler_params=_cparams("parallel"), name="cast_cols",
    )(w)


def _cast_pair_kernel(a_ref, b_ref, o_ref):
    tile = a_ref.shape[1]
    o_ref[:, :tile] = a_ref[...].astype(o_ref.dtype)
    o_ref[:, tile:] = b_ref[...].astype(o_ref.dtype)


def _cast_col_tile_pairs(wa, wb, layer, tile):
    k, n = wa.shape[1:]
    src = pl.BlockSpec((None, k, tile), lambda j: (layer, 0, j))
    return pl.pallas_call(
        _cast_pair_kernel, grid=(n // tile,),
        in_specs=[src, src],
        out_specs=pl.BlockSpec((None, k, 2 * tile), lambda j: (j, 0, 0)),
        out_shape=jax.ShapeDtypeStruct((n // tile, k, 2 * tile), BF16),
        compiler_params=_cparams("parallel"), name="cast_pairs",
    )(wa, wb)


def _cast_row_tiles(wt, layer, row0, n_rows, tile):
    k = wt.shape[2]
    assert row0 % 8 == 0 and n_rows % tile == 0
    return pl.pallas_call(
        _cast_kernel, grid=(n_rows // tile,),
        in_specs=[pl.BlockSpec((pl.Element(1), pl.Element(tile), pl.Element(k)),
                               lambda j: (layer, pl.multiple_of(row0 + j * tile, 8), 0))],
        out_specs=pl.BlockSpec((1, tile, k), lambda j: (j, 0, 0)),
        out_shape=jax.ShapeDtypeStruct((n_rows // tile, tile, k), BF16),
        compiler_params=_cparams("parallel"), name="cast_row_tiles",
    )(wt)


def _cast_rows(w, layer, tile):
    k, n = w.shape[1:]
    return pl.pallas_call(
        _cast_kernel, grid=(k // tile,),
        in_specs=[pl.BlockSpec((None, tile, n), lambda j: (layer, j, 0))],
        out_specs=pl.BlockSpec((tile, n), lambda j: (j, 0)),
        out_shape=jax.ShapeDtypeStruct((k, n), BF16),
        compiler_params=_cparams("parallel"), name="cast_rows",
    )(w)


def _proj_kernel(x_ref, w_ref, o_ref, *, transposed):
    dot = _dot_nt if transposed else _dot
    o_ref[...] = dot(x_ref[...], w_ref[...]).astype(o_ref.dtype)


def _proj(x, w3, transposed=False):
    m, k = x.shape
    nt = w3.shape[0]
    tn = w3.shape[1] if transposed else w3.shape[2]
    tm = min(TM, m)
    return pl.pallas_call(
        functools.partial(_proj_kernel, transposed=transposed), grid=(m // tm, nt),
        in_specs=[pl.BlockSpec((tm, k), lambda i, j: (i, 0)),
                  pl.BlockSpec((None,) + w3.shape[1:], lambda i, j: (j, 0, 0))],
        out_specs=pl.BlockSpec((tm, tn), lambda i, j: (i, j)),
        out_shape=jax.ShapeDtypeStruct((m, nt * tn), BF16),
        compiler_params=_cparams("parallel", "parallel"), name="proj",
    )(x, w3)


def _mod_proj_kernel(x_hbm, sh_ref, sc_ref, w_ref, o_ref, h_ref, xbuf_ref, sem, *, transposed):
    @pl.when(pl.program_id(1) == 0)
    def _():
        _stream_modulate(x_hbm, pl.program_id(0) * h_ref.shape[0], sh_ref, sc_ref, h_ref, xbuf_ref, sem)

    dot = _dot_nt if transposed else _dot
    o_ref[...] = dot(h_ref[...], w_ref[...]).astype(o_ref.dtype)


def _mod_proj(x, shift, scale, rows_per_vec, w3, transposed=False):
    m, k = x.shape
    nt = w3.shape[0]
    tn = w3.shape[1] if transposed else w3.shape[2]
    tm = min(TM, m)
    mod_in, h_spec, scratch = _modulate_specs(tm, k, rows_per_vec)
    return pl.pallas_call(
        functools.partial(_mod_proj_kernel, transposed=transposed), grid=(m // tm, nt),
        in_specs=mod_in + [pl.BlockSpec((None,) + w3.shape[1:], lambda i, j: (j, 0, 0))],
        out_specs=[pl.BlockSpec((tm, tn), lambda i, j: (i, j)), h_spec],
        out_shape=[jax.ShapeDtypeStruct((m, nt * tn), BF16), jax.ShapeDtypeStruct((m, k), BF16)],
        scratch_shapes=scratch,
        compiler_params=_cparams("parallel", "arbitrary"), name="mod_proj",
    )(x, shift, scale, w3)


def _qk_proj_kernel(x_hbm, sh_ref, sc_ref, w_ref, gain_ref, cos_ref, sin_ref, ones_ref, o_ref, h_ref,
                    xbuf_ref, sem, *, rope):
    @pl.when(pl.program_id(1) == 0)
    def _():
        _stream_modulate(x_hbm, pl.program_id(0) * h_ref.shape[0], sh_ref, sc_ref, h_ref, xbuf_ref, sem)

    half = h_ref.shape[0] // 2
    for rr in range(2):
        rows = slice(rr * half, (rr + 1) * half)
        x = h_ref[rows, :]
        for cc in range(w_ref.shape[1] // TN):
            acc = _dot(x, w_ref[:, cc * TN:(cc + 1) * TN])
            for tt in range(TN // MXU_W):
                c0 = cc * TN + tt * MXU_W
                a = acc[:, tt * MXU_W:(tt + 1) * MXU_W]
                ssq = _dot((a * a).astype(BF16), ones_ref[...])
                y = a * lax.rsqrt(ssq * (1.0 / HEAD_DIM) + EPS) * gain_ref[:, c0:c0 + MXU_W]
                for hh in range(MXU_W // HEAD_DIM):
                    yh = y[:, hh * HEAD_DIM:(hh + 1) * HEAD_DIM]
                    if rope:
                        yh = (yh * cos_ref[rows, :]
                              + pltpu.roll(yh, HEAD_DIM // 2, 1) * sin_ref[rows, :])
                    o_ref[rows, c0 + hh * HEAD_DIM:c0 + (hh + 1) * HEAD_DIM] = yh.astype(o_ref.dtype)


def _qk_proj(x, shift, scale, rows_per_vec, w3, tile0, n_tiles, gain, cos2, sin2, rope):
    m, k = x.shape
    tn = w3.shape[2]
    tm = min(TM, m)
    seq_blocks = cos2.shape[0] // tm if rope else 1
    tab = pl.BlockSpec((tm, HEAD_DIM), lambda i, j: (i % seq_blocks, 0))
    head_of = np.arange(MXU_W) // HEAD_DIM
    ones = jnp.asarray(head_of[:, None] == head_of[None, :], BF16)
    mod_in, h_spec, scratch = _modulate_specs(tm, k, rows_per_vec)
    return pl.pallas_call(
        functools.partial(_qk_proj_kernel, rope=rope), grid=(m // tm, n_tiles),
        in_specs=mod_in + [pl.BlockSpec((None, k, tn), lambda i, j: (tile0 + j, 0, 0)),
                           pl.BlockSpec((1, tn), lambda i, j: (0, j)),
                           tab, tab,
                           pl.BlockSpec((MXU_W, MXU_W), lambda i, j: (0, 0))],
        out_specs=[pl.BlockSpec((tm, tn), lambda i, j: (i, j)), h_spec],
        out_shape=[jax.ShapeDtypeStruct((m, n_tiles * tn), BF16), jax.ShapeDtypeStruct((m, k), BF16)],
        scratch_shapes=scratch,
        compiler_params=_cparams("parallel", "arbitrary"), name="qk_proj",
    )(x, shift, scale, w3, gain, cos2, sin2, ones)


def _out_res_kernel(*refs, n_in):
    xs, ws = refs[:n_in], refs[n_in:2 * n_in]
    res_ref, g_ref, o_ref = refs[2 * n_in:]
    acc = _dot(xs[0][...], ws[0][...])
    for x_ref, w_ref in zip(xs[1:], ws[1:]):
        acc = acc + _dot(x_ref[...], w_ref[...])
    o_ref[...] = res_ref[...] + g_ref[...] * acc


def _out_res(xs, w3, res, gate, rows_per_vec):
    m, n = res.shape
    nt, k, tn = w3.shape
    tm = min(TM, m)
    n_in = len(xs)
    kx = k // n_in
    assert all(x.shape[1] == kx for x in xs) and nt * tn == n

    def w_spec(part):
        return pl.BlockSpec((None, kx, tn), lambda i, j: (j, part, 0))

    in_specs = ([pl.BlockSpec((tm, kx), lambda i, j: (i, 0)) for _ in xs]
                + [w_spec(part) for part in range(n_in)]
                + [pl.BlockSpec((tm, tn), lambda i, j: (i, j)),
                   pl.BlockSpec((None, 1, tn), lambda i, j: ((i * tm) // rows_per_vec, 0, j))])
    return pl.pallas_call(
        functools.partial(_out_res_kernel, n_in=n_in), grid=(m // tm, nt),
        in_specs=in_specs,
        out_specs=pl.BlockSpec((tm, tn), lambda i, j: (i, j)),
        out_shape=jax.ShapeDtypeStruct((m, n), F32),
        compiler_params=_cparams("parallel", "parallel"), name="out_res",
    )(*xs, *([w3] * n_in), res, gate)


def _ffn_kernel(x_hbm, sh_ref, sc_ref, gate_ref, wgu_ref, wd_ref, o_ref, h_ref, sem):
    i, j = pl.program_id(0), pl.program_id(1)
    tm = o_ref.shape[0]
    n_chunks = tm // X_ROWS

    def x_copy(ci):
        rows = pl.ds(ci * X_ROWS, X_ROWS)
        src = x_hbm.at[pl.ds(i * tm + ci * X_ROWS, X_ROWS), :]
        return pltpu.make_async_copy(src, o_ref.at[rows, :], sem.at[ci])

    @pl.when(j == 0)
    def _():
        for ci in range(n_chunks):
            x_copy(ci).start()
        for ci in range(n_chunks):
            x_copy(ci).wait()

            def group(gi, carry, ci=ci):
                rows = pl.ds(pl.multiple_of(ci * X_ROWS + gi * X_GROUP, X_GROUP), X_GROUP)
                h_ref[rows, :] = (_rms(o_ref[rows, :]) * (1.0 + sc_ref[...]) + sh_ref[...]).astype(BF16)
                return carry
            lax.fori_loop(0, X_ROWS // X_GROUP, group, 0, unroll=True)

    gu = _dot(h_ref[...], wgu_ref[...])
    a = (_silu(gu[:, :FFN_TH]) * gu[:, FFN_TH:]).astype(BF16)
    o_ref[...] += gate_ref[...] * _dot(a, wd_ref[...])


def _ffn(x, shift, scale, gate, wgu3, wd, rows_per_vec):
    m, d = x.shape
    n_tiles = wgu3.shape[0]
    tm = min(TM, m)
    vec = pl.BlockSpec((None, 1, d), lambda i, j: ((i * tm) // rows_per_vec, 0, 0))
    return pl.pallas_call(
        _ffn_kernel, grid=(m // tm, n_tiles),
        in_specs=[pl.BlockSpec(memory_space=pl.ANY),
                  vec, vec, vec,
                  pl.BlockSpec((None, d, 2 * FFN_TH), lambda i, j: (j, 0, 0)),
                  pl.BlockSpec((FFN_TH, d), lambda i, j: (j, 0))],
        out_specs=pl.BlockSpec((tm, d), lambda i, j: (i, 0)),
        out_shape=jax.ShapeDtypeStruct((m, d), F32),
        scratch_shapes=[pltpu.VMEM((tm, d), BF16), pltpu.SemaphoreType.DMA((tm // X_ROWS,))],
        compiler_params=pltpu.CompilerParams(dimension_semantics=("parallel", "arbitrary"),
                                             vmem_limit_bytes=FFN_VMEM_LIMIT), name="ffn",
    )(x, shift, scale, gate, wgu3, wd)


def _gla_scan_tile(q_ref, k_ref, v_ref, g_ref, w2_ref, b2_ref, s_ref, emit, reverse):
    dk = q_ref.shape[1] // GLA_HEADS
    dv = v_ref.shape[1] // GLA_HEADS
    row = lax.broadcasted_iota(jnp.int32, (GLA_BLOCK, GLA_BLOCK), 0)
    col = lax.broadcasted_iota(jnp.int32, (GLA_BLOCK, GLA_BLOCK), 1)
    if reverse:
        tri = (col >= row).astype(BF16)
        mask = col > row
        last_row = 0
    else:
        tri = (col <= row).astype(BF16)
        mask = col <= row
        last_row = GLA_BLOCK - 1
    n_steps = q_ref.shape[0] // GLA_BLOCK
    order = [n_steps - 1 - st if reverse else st for st in range(n_steps)]

    parts = []
    for c in order:
        rows = slice(c * GLA_BLOCK, (c + 1) * GLA_BLOCK)
        q = q_ref[rows, :].astype(F32) * (dk ** -0.5)
        k = k_ref[rows, :].astype(F32)
        x = _dot(g_ref[rows, :], w2_ref[...]) + b2_ref[...]
        log_a = (jnp.minimum(x, 0.0) - jnp.log(1.0 + jnp.exp(-jnp.abs(x)))) * (1.0 / GLA_TAU)
        hi = log_a.astype(BF16)
        lo = (log_a - hi.astype(F32)).astype(BF16)
        b = _dot(tri, hi) + _dot(tri, lo)
        mid = b[GLA_BLOCK // 2:GLA_BLOCK // 2 + 1, :]
        b_last = b[last_row:last_row + 1, :]
        qe = q * jnp.exp(b - mid)
        ke = k * jnp.exp(mid - b)
        q_in = (qe * jnp.exp(mid)).astype(BF16)
        k_out = (ke * jnp.exp(b_last - mid)).astype(BF16)
        decay = jnp.exp(b_last)
        qe, ke = qe.astype(BF16), ke.astype(BF16)
        intra, kv = [], []
        for h in range(GLA_HEADS):
            ks = slice(h * dk, (h + 1) * dk)
            v = v_ref[rows, h * dv:(h + 1) * dv]
            att = jnp.where(mask, _dot_nt(qe[:, ks], ke[:, ks]), 0.0).astype(BF16)
            intra.append(_dot(att, v))
            kv.append(_dot_tn(v, k_out[:, ks]))
        parts.append((rows, q_in, decay, intra, kv))

    for h in range(GLA_HEADS):
        ks = slice(h * dk, (h + 1) * dk)
        s = s_ref[h]
        for rows, q_in, decay, intra, kv in parts:
            emit(rows, h, intra[h] + _dot_nt(q_in[:, ks], s.astype(BF16)))
            s = s * decay[:, ks] + kv[h]
        s_ref[h] = s


def _gla_rev_kernel(qc_ref, kc_ref, vc_ref, gc_ref, ql_ref, kl_ref, vl_ref, gl_ref, w2_ref, b2_ref,
                    oc_ref, ol_ref, s_ref):
    t = pl.program_id(1)
    dv = vc_ref.shape[1] // GLA_HEADS

    def run(q_ref, k_ref, v_ref, g_ref, o_ref):
        def emit(rows, h, o):
            o_ref[rows, h * dv:(h + 1) * dv] = o
        _gla_scan_tile(q_ref, k_ref, v_ref, g_ref, w2_ref, b2_ref, s_ref, emit, reverse=True)

    @pl.when(t == 0)
    def _():
        s_ref[...] = jnp.zeros_like(s_ref)
        run(qc_ref, kc_ref, vc_ref, gc_ref, oc_ref)

    @pl.when(t > 0)
    def _():
        run(ql_ref, kl_ref, vl_ref, gl_ref, ol_ref)


def _gla_fwd_kernel(qc_ref, kc_ref, vc_ref, rc_ref, gc_ref, pc_ref, ql_ref, kl_ref, vl_ref, rl_ref, gl_ref, pl_ref,
                    w2_ref, b2_ref, gain_ref, oc_ref, ol_ref, s_ref):
    t = pl.program_id(1)
    dv = vc_ref.shape[1] // GLA_HEADS

    def run(q_ref, k_ref, v_ref, r_ref, g_ref, p_ref, o_ref):
        def emit(rows, h, o):
            cols = slice(h * dv, (h + 1) * dv)
            total = o + p_ref[rows, cols]
            r = r_ref[rows, cols].astype(F32)
            o_ref[rows, cols] = (_rms(total) * gain_ref[...] * _silu(r)).astype(o_ref.dtype)
        _gla_scan_tile(q_ref, k_ref, v_ref, g_ref, w2_ref, b2_ref, s_ref, emit, reverse=False)

    @pl.when(t == 0)
    def _():
        s_ref[...] = jnp.zeros_like(s_ref)
        run(qc_ref, kc_ref, vc_ref, rc_ref, gc_ref, pc_ref, oc_ref)

    @pl.when(t > 0)
    def _():
        run(ql_ref, kl_ref, vl_ref, rl_ref, gl_ref, pl_ref, ol_ref)


def _gla(p_l, p_c, g_l, g_c, w2f, b2f, w2b, b2b, gain, batch, dk, dv):
    seq, ctx_len = p_l.shape[0] // batch, p_c.shape[0] // batch
    assert ctx_len == GLA_TILE and seq % GLA_TILE == 0
    n_lat = seq // GLA_TILE
    key_w, val_w = GLA_HEADS * dk, GLA_HEADS * dv
    assert val_w == 2 * key_w

    def lat_fwd(b, t):
        return b * n_lat + jnp.maximum(t - 1, 0)

    def lat_rev(b, t):
        return b * n_lat + n_lat - jnp.maximum(t, 1)

    def tok_specs(row_blk, with_r):
        specs = [pl.BlockSpec((GLA_TILE, key_w), lambda b, t: (row_blk(b, t), 0)),
                 pl.BlockSpec((GLA_TILE, key_w), lambda b, t: (row_blk(b, t), 1)),
                 pl.BlockSpec((GLA_TILE, val_w), lambda b, t: (row_blk(b, t), 1))]
        if with_r:
            specs.append(pl.BlockSpec((GLA_TILE, val_w), lambda b, t: (row_blk(b, t), 2)))
        return specs + [pl.BlockSpec((GLA_TILE, LANE), lambda b, t: (row_blk(b, t), 0))]

    def out_spec(row_blk):
        return pl.BlockSpec((GLA_TILE, val_w), lambda b, t: (row_blk(b, t), 0))

    def ctx_blk(b, t):
        return b

    w2_spec = pl.BlockSpec((LANE, key_w), lambda b, t: (0, 0))
    b2_spec = pl.BlockSpec((1, key_w), lambda b, t: (0, 0))
    state = pltpu.VMEM((GLA_HEADS, dv, dk), F32)
    grid = (batch, 1 + n_lat)
    rev_c, rev_l = pl.pallas_call(
        _gla_rev_kernel, grid=grid,
        in_specs=tok_specs(ctx_blk, False) + tok_specs(lat_rev, False) + [w2_spec, b2_spec],
        out_specs=[out_spec(ctx_blk), out_spec(lat_rev)],
        out_shape=[jax.ShapeDtypeStruct((batch * ctx_len, val_w), F32),
                   jax.ShapeDtypeStruct((batch * seq, val_w), F32)],
        scratch_shapes=[state],
        compiler_params=_cparams("parallel", "arbitrary"), name="gla_rev",
    )(p_c, p_c, p_c, g_c, p_l, p_l, p_l, g_l, w2b, b2b)
    return pl.pallas_call(
        _gla_fwd_kernel, grid=grid,
        in_specs=tok_specs(ctx_blk, True) + [out_spec(ctx_blk)] + tok_specs(lat_fwd, True) + [out_spec(lat_fwd)]
        + [w2_spec, b2_spec, pl.BlockSpec((1, dv), lambda b, t: (0, 0))],
        out_specs=[out_spec(ctx_blk), out_spec(lat_fwd)],
        out_shape=[jax.ShapeDtypeStruct((batch * ctx_len, val_w), BF16),
                   jax.ShapeDtypeStruct((batch * seq, val_w), BF16)],
        scratch_shapes=[state],
        compiler_params=_cparams("parallel", "arbitrary"), name="gla_fwd",
    )(p_c, p_c, p_c, p_c, g_c, rev_c, p_l, p_l, p_l, p_l, g_l, rev_l, w2f, b2f, gain)


def _sgu_kernel(u_ref, s_ref, ws_ref, bs_ref, gain_ref, o_ref):
    gw = u_ref.shape[1] // SGU_GROUPS
    for ci in range(u_ref.shape[0] // SGU_CHUNK):
        rows = slice(ci * SGU_CHUNK, (ci + 1) * SGU_CHUNK)
        for g in range(SGU_GROUPS):
            cols = slice(g * gw, (g + 1) * gw)
            vg = _rms(_gelu_tanh(s_ref[rows, cols].astype(F32))) * gain_ref[:, cols]
            mixed = _dot(ws_ref[g], vg.astype(BF16)) + bs_ref[:, g:g + 1]
            o_ref[rows, cols] = (_gelu_tanh(u_ref[rows, cols].astype(F32)) * mixed).astype(o_ref.dtype)


def _sgu(p, w_s, b_s_t, gain, width, u_blk, s_blk):
    m = p.shape[0]
    rows = 2 * SGU_CHUNK
    return pl.pallas_call(
        _sgu_kernel, grid=(m // rows,),
        in_specs=[pl.BlockSpec((rows, width), lambda i: (i, u_blk)),
                  pl.BlockSpec((rows, width), lambda i: (i, s_blk)),
                  pl.BlockSpec(w_s.shape, lambda i: (0, 0, 0)),
                  pl.BlockSpec(b_s_t.shape, lambda i: (0, 0)),
                  pl.BlockSpec((1, width), lambda i: (0, 0))],
        out_specs=pl.BlockSpec((rows, width), lambda i: (i, 0)),
        out_shape=jax.ShapeDtypeStruct((m, width), BF16),
        compiler_params=_cparams("parallel"), name="sgu",
    )(p, p, w_s, b_s_t, gain)


def _attn_kernel(sink_ref, bias_ref, q_ref, k_ref, v_ref, kc_ref, vc_ref, o_ref):
    kv, ng = pl.program_id(1), pl.program_id(2)
    seq = k_ref.shape[0]
    nb = seq // ATT_BLOCK
    band = 3 * ATT_BLOCK
    sink = jnp.concatenate([jnp.full((ATT_BLOCK, HEAD_DIM), sink_ref[kv * ATT_GROUP + g] * LOG2E, F32)
                            for g in range(ATT_GROUP)], axis=0)
    k_ctx, v_ctx = kc_ref[...], vc_ref[...]
    ones = jnp.ones((band + k_ctx.shape[0], HEAD_DIM), BF16)
    for jj in range(ATT_QB):
        n = ng * ATT_QB + jj
        start = pl.multiple_of(jnp.clip((n - 1) * ATT_BLOCK, 0, seq - band), ATT_BLOCK)
        if jj == 0:
            bias = bias_ref[jnp.where(n == 0, 1, 0)]
        elif jj == ATT_QB - 1:
            bias = bias_ref[jnp.where(n == nb - 1, 2, 0)]
        else:
            bias = bias_ref[0]
        q = q_ref[jj * ATT_BLOCK:(jj + 1) * ATT_BLOCK, :]
        qs = jnp.concatenate([q[:, g * HEAD_DIM:(g + 1) * HEAD_DIM] for g in range(ATT_GROUP)], axis=0)
        keys = jnp.concatenate([k_ref[pl.ds(start, band), :], k_ctx], axis=0)
        vals = jnp.concatenate([v_ref[pl.ds(start, band), :], v_ctx], axis=0)
        s = _dot_nt(qs, keys)
        s = jnp.concatenate([s[:, :band] + bias, s[:, band:]], axis=1)
        m = jnp.maximum(sink, jnp.max(s, axis=-1, keepdims=True))
        p = jnp.exp2(s - jnp.concatenate([m] * (s.shape[1] // HEAD_DIM), axis=1)).astype(BF16)
        ov = _dot(p, jnp.concatenate([vals, ones], axis=1))
        o = ov[:, :HEAD_DIM] / (ov[:, HEAD_DIM:] + jnp.exp2(sink - m))
        o_ref[jj * ATT_BLOCK:(jj + 1) * ATT_BLOCK, :] = jnp.concatenate(
            [o[g * ATT_BLOCK:(g + 1) * ATT_BLOCK] for g in range(ATT_GROUP)], axis=1).astype(o_ref.dtype)


def _band_bias():
    t = (np.arange(ATT_GROUP * ATT_BLOCK) % ATT_BLOCK)[:, None]
    i = np.arange(3 * ATT_BLOCK)[None, :]
    shifts = (-ATT_BLOCK, 0, -2 * ATT_BLOCK)
    return jnp.asarray(np.stack([np.where(np.abs(i + sh - t) <= WINDOW, 0.0, -np.inf) for sh in shifts]), F32)


def _attention(qk, v, k_c, v_c, sink, batch, n_q_heads):
    m = qk.shape[0]
    seq, ctx_len = m // batch, k_c.shape[0] // batch
    kv_heads = n_q_heads // ATT_GROUP
    ng = seq // (ATT_BLOCK * ATT_QB)
    assert ng >= 2
    qw = ATT_GROUP * HEAD_DIM
    bias = _band_bias()
    return pl.pallas_call(
        _attn_kernel, grid=(batch, kv_heads, ng),
        in_specs=[pl.BlockSpec(memory_space=pltpu.SMEM),
                  pl.BlockSpec(bias.shape, lambda b, h, n: (0, 0, 0)),
                  pl.BlockSpec((ATT_QB * ATT_BLOCK, qw), lambda b, h, n: (b * ng + n, h)),
                  pl.BlockSpec((seq, HEAD_DIM), lambda b, h, n: (b, n_q_heads + h)),
                  pl.BlockSpec((seq, HEAD_DIM), lambda b, h, n: (b, h)),
                  pl.BlockSpec((ctx_len, HEAD_DIM), lambda b, h, n: (b, h)),
                  pl.BlockSpec((ctx_len, HEAD_DIM), lambda b, h, n: (b, h))],
        out_specs=pl.BlockSpec((ATT_QB * ATT_BLOCK, qw), lambda b, h, n: (b * ng + n, h)),
        out_shape=jax.ShapeDtypeStruct((m, n_q_heads * HEAD_DIM), BF16),
        compiler_params=_cparams("parallel", "parallel", "arbitrary"), name="attention",
    )(sink, bias, qk, qk, v, k_c, v_c)


def _rope_tables(seq):
    rows = seq // GRID_W
    row = jnp.repeat(jnp.arange(rows, dtype=F32), GRID_W)
    col = jnp.tile(jnp.arange(GRID_W, dtype=F32), rows)
    n_freq = HEAD_DIM // 4
    inv_freq = ROPE_THETA ** (-jnp.arange(n_freq, dtype=F32) / n_freq)
    ang = jnp.concatenate([row[:, None] * inv_freq, col[:, None] * inv_freq], axis=-1)
    cos, sin = jnp.cos(ang), jnp.sin(ang)
    return jnp.concatenate([cos, cos], axis=-1), jnp.concatenate([-sin, sin], axis=-1)


def kernel(x, c, ctx, c_ctx, ada_w, ada_b, ffn_w_gate, ffn_w_up, ffn_w_down, even_w_in, even_gate_w2_fwd, even_gate_b_fwd, even_gate_w2_bwd, even_gate_b_bwd, even_gla_norm_gain, even_sgu_norm_gain, even_sgu_w_s, even_sgu_b_s, even_w_out, odd_w_in, odd_q_norm_gain, odd_k_norm_gain, odd_sink, odd_w_out):
    batch, seq, d = x.shape
    ctx_len = ctx.shape[1]
    depth = ada_w.shape[0]
    assert depth == 2 and batch + 1 <= 8
    xl = x.reshape(batch * seq, d)
    xc = ctx.reshape(batch * ctx_len, d)

    cc = jnp.concatenate([c, c_ctx[None], jnp.zeros((8 - batch - 1, d), F32)], axis=0)
    mods = _ada(cc, ada_w, ada_b)

    def mod_l(layer, which):
        return mods[layer, :batch, which * d:(which + 1) * d].reshape(batch, 1, d)

    def mod_c(layer, which):
        return mods[layer, batch:batch + 1, which * d:(which + 1) * d].reshape(1, 1, d)

    all_ctx = batch * ctx_len

    dv = even_gla_norm_gain.shape[1]
    key_w = even_gate_w2_fwd.shape[2]
    dk = key_w // GLA_HEADS
    val_w = GLA_HEADS * dv
    sgu_w = even_sgu_norm_gain.shape[1]
    gate0 = 2 * key_w + 2 * val_w
    gate1 = gate0 + 2 * GLA_GATE_RANK
    w_in_t = jnp.swapaxes(even_w_in, 1, 2)
    w_qkvr = _cast_row_tiles(w_in_t, 0, 0, gate0, TN_WIDE)
    w_us = _cast_row_tiles(w_in_t, 0, gate1, 2 * sgu_w, TN_WIDE)
    w_gate = jnp.pad(w_in_t[0, gate0:gate1], ((0, LANE - 2 * GLA_GATE_RANK), (0, 0))).astype(BF16)[None]
    pad_rows = LANE - GLA_GATE_RANK
    w2f = jnp.pad(even_gate_w2_fwd[0], ((0, pad_rows), (0, 0))).astype(BF16)
    w2b = jnp.pad(even_gate_w2_bwd[0], ((GLA_GATE_RANK, pad_rows - GLA_GATE_RANK), (0, 0))).astype(BF16)
    b2f = even_gate_b_fwd[0].reshape(1, key_w)
    b2b = even_gate_b_bwd[0].reshape(1, key_w)

    p_l, h_l = _mod_proj(xl, mod_l(0, 0), mod_l(0, 1), seq, w_qkvr, True)
    p_c, h_c = _mod_proj(xc, mod_c(0, 0), mod_c(0, 1), all_ctx, w_qkvr, True)
    us_l, us_c = _proj(h_l, w_us, True), _proj(h_c, w_us, True)
    g_l, g_c = _proj(h_l, w_gate, True), _proj(h_c, w_gate, True)
    gla_c, gla_l = _gla(p_l, p_c, g_l, g_c, w2f, b2f, w2b, b2b, even_gla_norm_gain[0].reshape(1, dv),
                        batch, dk, dv)
    w_s = even_sgu_w_s[0].astype(BF16)
    b_s_t = even_sgu_b_s[0].T
    sgu_gain = even_sgu_norm_gain[0].reshape(1, sgu_w)
    sgu_l = _sgu(us_l, w_s, b_s_t, sgu_gain, sgu_w, 0, 1)
    sgu_c = _sgu(us_c, w_s, b_s_t, sgu_gain, sgu_w, 0, 1)
    assert val_w == sgu_w
    w_out = _cast_col_tiles(even_w_out, 0, 0, d, TN_WIDE)
    x_l = _out_res([gla_l, sgu_l], w_out, xl, mod_l(0, 2), seq)
    x_c = _out_res([gla_c, sgu_c], w_out, xc, mod_c(0, 2), all_ctx)

    hidden = ffn_w_gate.shape[2]
    assert hidden % FFN_TH == 0
    wgu = _cast_col_tile_pairs(ffn_w_gate, ffn_w_up, 0, FFN_TH)
    wd = _cast_rows(ffn_w_down, 0, FFN_TH)
    x_l = _ffn(x_l, mod_l(0, 3), mod_l(0, 4), mod_l(0, 5), wgu, wd, seq)
    x_c = _ffn(x_c, mod_c(0, 3), mod_c(0, 4), mod_c(0, 5), wgu, wd, all_ctx)


    n_heads = odd_sink.shape[1]
    q_w = n_heads * HEAD_DIM
    kv_w = q_w // ATT_GROUP
    qk_tile = TN_WIDE
    assert kv_w == qk_tile
    w_qk = _cast_col_tiles(odd_w_in, 0, 0, q_w + kv_w, qk_tile)
    w_v = _cast_col_tiles(odd_w_in, 0, q_w + kv_w, kv_w, TN_WIDE)
    k_gain = jnp.tile(odd_k_norm_gain[0], kv_w // HEAD_DIM)
    qk_gain = jnp.concatenate([jnp.tile(odd_q_norm_gain[0] * (HEAD_DIM ** -0.5 * LOG2E), n_heads),
                               k_gain]).reshape(1, -1)
    cos2, sin2 = _rope_tables(seq)
    qk_l, h_l = _qk_proj(x_l, mod_l(1, 0), mod_l(1, 1), seq, w_qk, 0, w_qk.shape[0], qk_gain, cos2, sin2, rope=True)
    v_l = _proj(h_l, w_v)
    k_c, h_c = _qk_proj(x_c, mod_c(1, 0), mod_c(1, 1), all_ctx, w_qk, q_w // qk_tile, 1, k_gain.reshape(1, -1),
                        cos2, sin2, rope=False)
    v_c = _proj(h_c, w_v)
    att = _attention(qk_l, v_l, k_c, v_c, odd_sink[0], batch, n_heads)
    x_l = _out_res([att], _cast_col_tiles(odd_w_out, 0, 0, d, TN_WIDE), x_l, mod_l(1, 2), seq)

    wgu = _cast_col_tile_pairs(ffn_w_gate, ffn_w_up, 1, FFN_TH)
    wd = _cast_rows(ffn_w_down, 1, FFN_TH)
    return _ffn(x_l, mod_l(1, 3), mod_l(1, 4), mod_l(1, 5), wgu, wd, seq).reshape(batch, seq, d)
```

```python
import functools

import numpy as np
import jax
import jax.numpy as jnp
from jax import lax
from jax.experimental import pallas as pl
from jax.experimental.pallas import tpu as pltpu

F32 = jnp.float32
BF16 = jnp.bfloat16

EPS = 1e-6
LOG2E = float(np.log2(np.e))
N_MOD = 6
GRID_W = 64
ROPE_THETA = 10000.0

GLA_HEADS = 4
GLA_TAU = 16.0
GLA_GATE_RANK = 16
GLA_BLOCK = 64
GLA_TILE = 256
SGU_GROUPS = 8
SGU_CHUNK = 128
HEAD_DIM = 128
ATT_GROUP = 4
WINDOW = 128
ATT_BLOCK = 128
ATT_QB = 8

LANE = 128
MXU_W = 256
VMEM_LIMIT = 56 * 1024 * 1024
FFN_VMEM_LIMIT = 60 * 1024 * 1024

TM = 1024
TN = 512
TN_WIDE = 1024
FFN_TH = 256
X_ROWS = 128
X_GROUP = 16
ROWS_EW = 512


def _cparams(*sem):
    return pltpu.CompilerParams(dimension_semantics=sem, vmem_limit_bytes=VMEM_LIMIT)


def _dot(a, b):
    return jnp.dot(a, b, preferred_element_type=F32)


def _dot_nt(a, b):
    return lax.dot_general(a, b, (((1,), (1,)), ((), ())), preferred_element_type=F32)


def _dot_tn(a, b):
    return lax.dot_general(a, b, (((0,), (0,)), ((), ())), preferred_element_type=F32)


def _silu(x):
    return x * jax.nn.sigmoid(x)


def _gelu_tanh(x):
    c = np.float32(np.sqrt(2.0 / np.pi))
    return x * (0.5 * (1.0 + jnp.tanh(c * (x + 0.044715 * (x * x * x)))))


def _rms(x):
    return x * lax.rsqrt(jnp.mean(x * x, axis=-1, keepdims=True) + EPS)


def _ada_kernel(c_ref, w_ref, b_ref, o_ref):
    a = _silu(c_ref[...]).astype(BF16)
    o_ref[...] = _dot(a, w_ref[...].astype(BF16)) + b_ref[...]


def _ada(cc, ada_w, ada_b):
    depth, d, n = ada_w.shape
    rows = cc.shape[0]
    return pl.pallas_call(
        _ada_kernel,
        grid=(depth, n // TN),
        in_specs=[pl.BlockSpec((rows, d), lambda l, j: (0, 0)),
                  pl.BlockSpec((None, d, TN), lambda l, j: (l, 0, j)),
                  pl.BlockSpec((None, 1, TN), lambda l, j: (l, 0, j))],
        out_specs=pl.BlockSpec((None, rows, TN), lambda l, j: (l, 0, j)),
        out_shape=jax.ShapeDtypeStruct((depth, rows, n), F32),
        compiler_params=_cparams("parallel", "parallel"),
        name="ada",
    )(cc, ada_w, ada_b.reshape(depth, 1, n))


def _modulate_kernel(x_ref, sh_ref, sc_ref, h_ref):
    h_ref[...] = (_rms(x_ref[...]) * (1.0 + sc_ref[...]) + sh_ref[...]).astype(h_ref.dtype)


def _row_spec(d):
    return pl.BlockSpec((ROWS_EW, d), lambda i: (i, 0))


def _vec_spec(d, rows_per_vec):
    return pl.BlockSpec((None, 1, d), lambda i: ((i * ROWS_EW) // rows_per_vec, 0, 0))


def _modulate(x, shift, scale, rows_per_vec):
    m, d = x.shape
    return pl.pallas_call(
        _modulate_kernel, grid=(m // ROWS_EW,),
        in_specs=[_row_spec(d), _vec_spec(d, rows_per_vec), _vec_spec(d, rows_per_vec)],
        out_specs=_row_spec(d),
        out_shape=jax.ShapeDtypeStruct((m, d), BF16),
        compiler_params=_cparams("parallel"), name="modulate",
    )(x, shift, scale)


def _cast_kernel(w_ref, o_ref):
    o_ref[...] = w_ref[...].astype(o_ref.dtype)


def _cast_col_tiles(w, layer, col0, n_cols, tile):
    k = w.shape[1]
    assert col0 % tile == 0 and n_cols % tile == 0
    t0 = col0 // tile
    return pl.pallas_call(
        _cast_kernel, grid=(n_cols // tile,),
        in_specs=[pl.BlockSpec((None, k, tile), lambda j: (layer, 0, t0 + j))],
        out_specs=pl.BlockSpec((None, k, tile), lambda j: (j, 0, 0)),
        out_shape=jax.ShapeDtypeStruct((n_cols // tile, k, tile), BF16),
        compiler_params=_cparams("parallel"), name="cast_cols",
    )(w)


def _cast_pair_kernel(a_ref, b_ref, o_ref):
    tile = a_ref.shape[1]
    o_ref[:, :tile] = a_ref[...].astype(o_ref.dtype)
    o_ref[:, tile:] = b_ref[...].astype(o_ref.dtype)


def _cast_col_tile_pairs(wa, wb, layer, tile):
    k, n = wa.shape[1:]
    src = pl.BlockSpec((None, k, tile), lambda j: (layer, 0, j))
    return pl.pallas_call(
        _cast_pair_kernel, grid=(n // tile,),
        in_specs=[src, src],
        out_specs=pl.BlockSpec((None, k, 2 * tile), lambda j: (j, 0, 0)),
        out_shape=jax.ShapeDtypeStruct((n // tile, k, 2 * tile), BF16),
        compiler_params=_cparams("parallel"), name="cast_pairs",
    )(wa, wb)


def _cast_row_tiles(wt, layer, row0, n_rows, tile):
    k = wt.shape[2]
    assert row0 % 8 == 0 and n_rows % tile == 0
    return pl.pallas_call(
        _cast_kernel, grid=(n_rows // tile,),
        in_specs=[pl.BlockSpec((pl.Element(1), pl.Element(tile), pl.Element(k)),
                               lambda j: (layer, pl.multiple_of(row0 + j * tile, 8), 0))],
        out_specs=pl.BlockSpec((1, tile, k), lambda j: (j, 0, 0)),
        out_shape=jax.ShapeDtypeStruct((n_rows // tile, tile, k), BF16),
        compiler_params=_cparams("parallel"), name="cast_row_tiles",
    )(wt)


def _cast_rows(w, layer, tile):
    k, n = w.shape[1:]
    return pl.pallas_call(
        _cast_kernel, grid=(k // tile,),
        in_specs=[pl.BlockSpec((None, tile, n), lambda j: (layer, j, 0))],
        out_specs=pl.BlockSpec((tile, n), lambda j: (j, 0)),
        out_shape=jax.ShapeDtypeStruct((k, n), BF16),
        compiler_params=_cparams("parallel"), name="cast_rows",
    )(w)


def _proj_kernel(x_ref, w_ref, o_ref, *, transposed):
    dot = _dot_nt if transposed else _dot
    o_ref[...] = dot(x_ref[...], w_ref[...]).astype(o_ref.dtype)


def _proj(x, w3, transposed=False):
    m, k = x.shape
    nt = w3.shape[0]
    tn = w3.shape[1] if transposed else w3.shape[2]
    tm = min(TM, m)
    return pl.pallas_call(
        functools.partial(_proj_kernel, transposed=transposed), grid=(m // tm, nt),
        in_specs=[pl.BlockSpec((tm, k), lambda i, j: (i, 0)),
                  pl.BlockSpec((None,) + w3.shape[1:], lambda i, j: (j, 0, 0))],
        out_specs=pl.BlockSpec((tm, tn), lambda i, j: (i, j)),
        out_shape=jax.ShapeDtypeStruct((m, nt * tn), BF16),
        compiler_params=_cparams("parallel", "parallel"), name="proj",
    )(x, w3)


def _qk_proj_kernel(x_ref, w_ref, gain_ref, cos_ref, sin_ref, ones_ref, o_ref, *, rope):
    half = x_ref.shape[0] // 2
    for rr in range(2):
        rows = slice(rr * half, (rr + 1) * half)
        x = x_ref[rows, :]
        for cc in range(w_ref.shape[1] // TN):
            acc = _dot(x, w_ref[:, cc * TN:(cc + 1) * TN])
            for tt in range(TN // MXU_W):
                c0 = cc * TN + tt * MXU_W
                a = acc[:, tt * MXU_W:(tt + 1) * MXU_W]
                ssq = _dot((a * a).astype(BF16), ones_ref[...])
                y = a * lax.rsqrt(ssq * (1.0 / HEAD_DIM) + EPS) * gain_ref[:, c0:c0 + MXU_W]
                for hh in range(MXU_W // HEAD_DIM):
                    yh = y[:, hh * HEAD_DIM:(hh + 1) * HEAD_DIM]
                    if rope:
                        yh = (yh * cos_ref[rows, :]
                              + pltpu.roll(yh, HEAD_DIM // 2, 1) * sin_ref[rows, :])
                    o_ref[rows, c0 + hh * HEAD_DIM:c0 + (hh + 1) * HEAD_DIM] = yh.astype(o_ref.dtype)


def _qk_proj(x, w3, tile0, n_tiles, gain, cos2, sin2, rope):
    m, k = x.shape
    tn = w3.shape[2]
    tm = min(TM, m)
    seq_blocks = cos2.shape[0] // tm if rope else 1
    tab = pl.BlockSpec((tm, HEAD_DIM), lambda i, j: (i % seq_blocks, 0))
    head_of = np.arange(MXU_W) // HEAD_DIM
    ones = jnp.asarray(head_of[:, None] == head_of[None, :], BF16)
    return pl.pallas_call(
        functools.partial(_qk_proj_kernel, rope=rope), grid=(m // tm, n_tiles),
        in_specs=[pl.BlockSpec((tm, k), lambda i, j: (i, 0)),
                  pl.BlockSpec((None, k, tn), lambda i, j: (tile0 + j, 0, 0)),
                  pl.BlockSpec((1, tn), lambda i, j: (0, j)),
                  tab, tab,
                  pl.BlockSpec((MXU_W, MXU_W), lambda i, j: (0, 0))],
        out_specs=pl.BlockSpec((tm, tn), lambda i, j: (i, j)),
        out_shape=jax.ShapeDtypeStruct((m, n_tiles * tn), BF16),
        compiler_params=_cparams("parallel", "parallel"), name="qk_proj",
    )(x, w3, gain, cos2, sin2, ones)


def _out_res_kernel(*refs, n_in):
    xs, ws = refs[:n_in], refs[n_in:2 * n_in]
    res_ref, g_ref, o_ref = refs[2 * n_in:]
    acc = _dot(xs[0][...], ws[0][...])
    for x_ref, w_ref in zip(xs[1:], ws[1:]):
        acc = acc + _dot(x_ref[...], w_ref[...])
    o_ref[...] = res_ref[...] + g_ref[...] * acc


def _out_res(xs, w3, res, gate, rows_per_vec):
    m, n = res.shape
    nt, k, tn = w3.shape
    tm = min(TM, m)
    n_in = len(xs)
    kx = k // n_in
    assert all(x.shape[1] == kx for x in xs) and nt * tn == n

    def w_spec(part):
        return pl.BlockSpec((None, kx, tn), lambda i, j: (j, part, 0))

    in_specs = ([pl.BlockSpec((tm, kx), lambda i, j: (i, 0)) for _ in xs]
                + [w_spec(part) for part in range(n_in)]
                + [pl.BlockSpec((tm, tn), lambda i, j: (i, j)),
                   pl.BlockSpec((None, 1, tn), lambda i, j: ((i * tm) // rows_per_vec, 0, j))])
    return pl.pallas_call(
        functools.partial(_out_res_kernel, n_in=n_in), grid=(m // tm, nt),
        in_specs=in_specs,
        out_specs=pl.BlockSpec((tm, tn), lambda i, j: (i, j)),
        out_shape=jax.ShapeDtypeStruct((m, n), F32),
        compiler_params=_cparams("parallel", "parallel"), name="out_res",
    )(*xs, *([w3] * n_in), res, gate)


def _ffn_kernel(x_hbm, sh_ref, sc_ref, gate_ref, wgu_ref, wd_ref, o_ref, h_ref, sem):
    i, j = pl.program_id(0), pl.program_id(1)
    tm = o_ref.shape[0]
    n_chunks = tm // X_ROWS

    def x_copy(ci):
        rows = pl.ds(ci * X_ROWS, X_ROWS)
        src = x_hbm.at[pl.ds(i * tm + ci * X_ROWS, X_ROWS), :]
        return pltpu.make_async_copy(src, o_ref.at[rows, :], sem.at[ci])

    @pl.when(j == 0)
    def _():
        for ci in range(n_chunks):
            x_copy(ci).start()
        for ci in range(n_chunks):
            x_copy(ci).wait()

            def group(gi, carry, ci=ci):
                rows = pl.ds(pl.multiple_of(ci * X_ROWS + gi * X_GROUP, X_GROUP), X_GROUP)
                h_ref[rows, :] = (_rms(o_ref[rows, :]) * (1.0 + sc_ref[...]) + sh_ref[...]).astype(BF16)
                return carry
            lax.fori_loop(0, X_ROWS // X_GROUP, group, 0, unroll=True)

    gu = _dot(h_ref[...], wgu_ref[...])
    a = (_silu(gu[:, :FFN_TH]) * gu[:, FFN_TH:]).astype(BF16)
    o_ref[...] += gate_ref[...] * _dot(a, wd_ref[...])


def _ffn(x, shift, scale, gate, wgu3, wd, rows_per_vec):
    m, d = x.shape
    n_tiles = wgu3.shape[0]
    tm = min(TM, m)
    vec = pl.BlockSpec((None, 1, d), lambda i, j: ((i * tm) // rows_per_vec, 0, 0))
    return pl.pallas_call(
        _ffn_kernel, grid=(m // tm, n_tiles),
        in_specs=[pl.BlockSpec(memory_space=pl.ANY),
                  vec, vec, vec,
                  pl.BlockSpec((None, d, 2 * FFN_TH), lambda i, j: (j, 0, 0)),
                  pl.BlockSpec((FFN_TH, d), lambda i, j: (j, 0))],
        out_specs=pl.BlockSpec((tm, d), lambda i, j: (i, 0)),
        out_shape=jax.ShapeDtypeStruct((m, d), F32),
        scratch_shapes=[pltpu.VMEM((tm, d), BF16), pltpu.SemaphoreType.DMA((tm // X_ROWS,))],
        compiler_params=pltpu.CompilerParams(dimension_semantics=("parallel", "arbitrary"),
                                             vmem_limit_bytes=FFN_VMEM_LIMIT), name="ffn",
    )(x, shift, scale, gate, wgu3, wd)


def _gla_scan_tile(q_ref, k_ref, v_ref, g_ref, w2_ref, b2_ref, s_ref, emit, reverse):
    dk = q_ref.shape[1] // GLA_HEADS
    dv = v_ref.shape[1] // GLA_HEADS
    row = lax.broadcasted_iota(jnp.int32, (GLA_BLOCK, GLA_BLOCK), 0)
    col = lax.broadcasted_iota(jnp.int32, (GLA_BLOCK, GLA_BLOCK), 1)
    if reverse:
        tri = (col >= row).astype(BF16)
        mask = col > row
        last_row = 0
    else:
        tri = (col <= row).astype(BF16)
        mask = col <= row
        last_row = GLA_BLOCK - 1
    n_steps = q_ref.shape[0] // GLA_BLOCK
    order = [n_steps - 1 - st if reverse else st for st in range(n_steps)]

    parts = []
    for c in order:
        rows = slice(c * GLA_BLOCK, (c + 1) * GLA_BLOCK)
        q = q_ref[rows, :].astype(F32) * (dk ** -0.5)
        k = k_ref[rows, :].astype(F32)
        x = _dot(g_ref[rows, :], w2_ref[...]) + b2_ref[...]
        log_a = (jnp.minimum(x, 0.0) - jnp.log(1.0 + jnp.exp(-jnp.abs(x)))) * (1.0 / GLA_TAU)
        hi = log_a.astype(BF16)
        lo = (log_a - hi.astype(F32)).astype(BF16)
        b = _dot(tri, hi) + _dot(tri, lo)
        mid = b[GLA_BLOCK // 2:GLA_BLOCK // 2 + 1, :]
        b_last = b[last_row:last_row + 1, :]
        qe = q * jnp.exp(b - mid)
        ke = k * jnp.exp(mid - b)
        q_in = (qe * jnp.exp(mid)).astype(BF16)
        k_out = (ke * jnp.exp(b_last - mid)).astype(BF16)
        decay = jnp.exp(b_last)
        qe, ke = qe.astype(BF16), ke.astype(BF16)
        intra, kv = [], []
        for h in range(GLA_HEADS):
            ks = slice(h * dk, (h + 1) * dk)
            v = v_ref[rows, h * dv:(h + 1) * dv]
            att = jnp.where(mask, _dot_nt(qe[:, ks], ke[:, ks]), 0.0).astype(BF16)
            intra.append(_dot(att, v))
            kv.append(_dot_tn(v, k_out[:, ks]))
        parts.append((rows, q_in, decay, intra, kv))

    for h in range(GLA_HEADS):
        ks = slice(h * dk, (h + 1) * dk)
        s = s_ref[h]
        for rows, q_in, decay, intra, kv in parts:
            emit(rows, h, intra[h] + _dot_nt(q_in[:, ks], s.astype(BF16)))
            s = s * decay[:, ks] + kv[h]
        s_ref[h] = s


def _gla_rev_kernel(qc_ref, kc_ref, vc_ref, gc_ref, ql_ref, kl_ref, vl_ref, gl_ref, w2_ref, b2_ref,
                    oc_ref, ol_ref, s_ref):
    t = pl.program_id(1)
    dv = vc_ref.shape[1] // GLA_HEADS

    def run(q_ref, k_ref, v_ref, g_ref, o_ref):
        def emit(rows, h, o):
            o_ref[rows, h * dv:(h + 1) * dv] = o
        _gla_scan_tile(q_ref, k_ref, v_ref, g_ref, w2_ref, b2_ref, s_ref, emit, reverse=True)

    @pl.when(t == 0)
    def _():
        s_ref[...] = jnp.zeros_like(s_ref)
        run(qc_ref, kc_ref, vc_ref, gc_ref, oc_ref)

    @pl.when(t > 0)
    def _():
        run(ql_ref, kl_ref, vl_ref, gl_ref, ol_ref)


def _gla_fwd_kernel(qc_ref, kc_ref, vc_ref, rc_ref, gc_ref, pc_ref, ql_ref, kl_ref, vl_ref, rl_ref, gl_ref, pl_ref,
                    w2_ref, b2_ref, gain_ref, oc_ref, ol_ref, s_ref):
    t = pl.program_id(1)
    dv = vc_ref.shape[1] // GLA_HEADS

    def run(q_ref, k_ref, v_ref, r_ref, g_ref, p_ref, o_ref):
        def emit(rows, h, o):
            cols = slice(h * dv, (h + 1) * dv)
            total = o + p_ref[rows, cols]
            r = r_ref[rows, cols].astype(F32)
            o_ref[rows, cols] = (_rms(total) * gain_ref[...] * _silu(r)).astype(o_ref.dtype)
        _gla_scan_tile(q_ref, k_ref, v_ref, g_ref, w2_ref, b2_ref, s_ref, emit, reverse=False)

    @pl.when(t == 0)
    def _():
        s_ref[...] = jnp.zeros_like(s_ref)
        run(qc_ref, kc_ref, vc_ref, rc_ref, gc_ref, pc_ref, oc_ref)

    @pl.when(t > 0)
    def _():
        run(ql_ref, kl_ref, vl_ref, rl_ref, gl_ref, pl_ref, ol_ref)


def _gla(p_l, p_c, g_l, g_c, w2f, b2f, w2b, b2b, gain, batch, dk, dv):
    seq, ctx_len = p_l.shape[0] // batch, p_c.shape[0] // batch
    assert ctx_len == GLA_TILE and seq % GLA_TILE == 0
    n_lat = seq // GLA_TILE
    key_w, val_w = GLA_HEADS * dk, GLA_HEADS * dv
    assert val_w == 2 * key_w

    def lat_fwd(b, t):
        return b * n_lat + jnp.maximum(t - 1, 0)

    def lat_rev(b, t):
        return b * n_lat + n_lat - jnp.maximum(t, 1)

    def tok_specs(row_blk, with_r):
        specs = [pl.BlockSpec((GLA_TILE, key_w), lambda b, t: (row_blk(b, t), 0)),
                 pl.BlockSpec((GLA_TILE, key_w), lambda b, t: (row_blk(b, t), 1)),
                 pl.BlockSpec((GLA_TILE, val_w), lambda b, t: (row_blk(b, t), 1))]
        if with_r:
            specs.append(pl.BlockSpec((GLA_TILE, val_w), lambda b, t: (row_blk(b, t), 2)))
        return specs + [pl.BlockSpec((GLA_TILE, LANE), lambda b, t: (row_blk(b, t), 0))]

    def out_spec(row_blk):
        return pl.BlockSpec((GLA_TILE, val_w), lambda b, t: (row_blk(b, t), 0))

    def ctx_blk(b, t):
        return b

    w2_spec = pl.BlockSpec((LANE, key_w), lambda b, t: (0, 0))
    b2_spec = pl.BlockSpec((1, key_w), lambda b, t: (0, 0))
    state = pltpu.VMEM((GLA_HEADS, dv, dk), F32)
    grid = (batch, 1 + n_lat)
    rev_c, rev_l = pl.pallas_call(
        _gla_rev_kernel, grid=grid,
        in_specs=tok_specs(ctx_blk, False) + tok_specs(lat_rev, False) + [w2_spec, b2_spec],
        out_specs=[out_spec(ctx_blk), out_spec(lat_rev)],
        out_shape=[jax.ShapeDtypeStruct((batch * ctx_len, val_w), F32),
                   jax.ShapeDtypeStruct((batch * seq, val_w), F32)],
        scratch_shapes=[state],
        compiler_params=_cparams("parallel", "arbitrary"), name="gla_rev",
    )(p_c, p_c, p_c, g_c, p_l, p_l, p_l, g_l, w2b, b2b)
    return pl.pallas_call(
        _gla_fwd_kernel, grid=grid,
        in_specs=tok_specs(ctx_blk, True) + [out_spec(ctx_blk)] + tok_specs(lat_fwd, True) + [out_spec(lat_fwd)]
        + [w2_spec, b2_spec, pl.BlockSpec((1, dv), lambda b, t: (0, 0))],
        out_specs=[out_spec(ctx_blk), out_spec(lat_fwd)],
        out_shape=[jax.ShapeDtypeStruct((batch * ctx_len, val_w), BF16),
                   jax.ShapeDtypeStruct((batch * seq, val_w), BF16)],
        scratch_shapes=[state],
        compiler_params=_cparams("parallel", "arbitrary"), name="gla_fwd",
    )(p_c, p_c, p_c, p_c, g_c, rev_c, p_l, p_l, p_l, p_l, g_l, rev_l, w2f, b2f, gain)


def _sgu_kernel(u_ref, s_ref, ws_ref, bs_ref, gain_ref, o_ref):
    gw = u_ref.shape[1] // SGU_GROUPS
    for ci in range(u_ref.shape[0] // SGU_CHUNK):
        rows = slice(ci * SGU_CHUNK, (ci + 1) * SGU_CHUNK)
        for g in range(SGU_GROUPS):
            cols = slice(g * gw, (g + 1) * gw)
            vg = _rms(_gelu_tanh(s_ref[rows, cols].astype(F32))) * gain_ref[:, cols]
            mixed = _dot(ws_ref[g], vg.astype(BF16)) + bs_ref[:, g:g + 1]
            o_ref[rows, cols] = (_gelu_tanh(u_ref[rows, cols].astype(F32)) * mixed).astype(o_ref.dtype)


def _sgu(p, w_s, b_s_t, gain, width, u_blk, s_blk):
    m = p.shape[0]
    rows = 2 * SGU_CHUNK
    return pl.pallas_call(
        _sgu_kernel, grid=(m // rows,),
        in_specs=[pl.BlockSpec((rows, width), lambda i: (i, u_blk)),
                  pl.BlockSpec((rows, width), lambda i: (i, s_blk)),
                  pl.BlockSpec(w_s.shape, lambda i: (0, 0, 0)),
                  pl.BlockSpec(b_s_t.shape, lambda i: (0, 0)),
                  pl.BlockSpec((1, width), lambda i: (0, 0))],
        out_specs=pl.BlockSpec((rows, width), lambda i: (i, 0)),
        out_shape=jax.ShapeDtypeStruct((m, width), BF16),
        compiler_params=_cparams("parallel"), name="sgu",
    )(p, p, w_s, b_s_t, gain)


def _attn_kernel(sink_ref, bias_ref, q_ref, k_ref, v_ref, kc_ref, vc_ref, o_ref):
    kv, ng = pl.program_id(1), pl.program_id(2)
    seq = k_ref.shape[0]
    nb = seq // ATT_BLOCK
    band = 3 * ATT_BLOCK
    sink = jnp.concatenate([jnp.full((ATT_BLOCK, HEAD_DIM), sink_ref[kv * ATT_GROUP + g] * LOG2E, F32)
                            for g in range(ATT_GROUP)], axis=0)
    k_ctx, v_ctx = kc_ref[...], vc_ref[...]
    ones = jnp.ones((band + k_ctx.shape[0], HEAD_DIM), BF16)
    for jj in range(ATT_QB):
        n = ng * ATT_QB + jj
        start = pl.multiple_of(jnp.clip((n - 1) * ATT_BLOCK, 0, seq - band), ATT_BLOCK)
        if jj == 0:
            bias = bias_ref[jnp.where(n == 0, 1, 0)]
        elif jj == ATT_QB - 1:
            bias = bias_ref[jnp.where(n == nb - 1, 2, 0)]
        else:
            bias = bias_ref[0]
        q = q_ref[jj * ATT_BLOCK:(jj + 1) * ATT_BLOCK, :]
        qs = jnp.concatenate([q[:, g * HEAD_DIM:(g + 1) * HEAD_DIM] for g in range(ATT_GROUP)], axis=0)
        keys = jnp.concatenate([k_ref[pl.ds(start, band), :], k_ctx], axis=0)
        vals = jnp.concatenate([v_ref[pl.ds(start, band), :], v_ctx], axis=0)
        s = _dot_nt(qs, keys)
        s = jnp.concatenate([s[:, :band] + bias, s[:, band:]], axis=1)
        m = jnp.maximum(sink, jnp.max(s, axis=-1, keepdims=True))
        p = jnp.exp2(s - jnp.concatenate([m] * (s.shape[1] // HEAD_DIM), axis=1)).astype(BF16)
        ov = _dot(p, jnp.concatenate([vals, ones], axis=1))
        o = ov[:, :HEAD_DIM] / (ov[:, HEAD_DIM:] + jnp.exp2(sink - m))
        o_ref[jj * ATT_BLOCK:(jj + 1) * ATT_BLOCK, :] = jnp.concatenate(
            [o[g * ATT_BLOCK:(g + 1) * ATT_BLOCK] for g in range(ATT_GROUP)], axis=1).astype(o_ref.dtype)


def _band_bias():
    t = (np.arange(ATT_GROUP * ATT_BLOCK) % ATT_BLOCK)[:, None]
    i = np.arange(3 * ATT_BLOCK)[None, :]
    shifts = (-ATT_BLOCK, 0, -2 * ATT_BLOCK)
    return jnp.asarray(np.stack([np.where(np.abs(i + sh - t) <= WINDOW, 0.0, -np.inf) for sh in shifts]), F32)


def _attention(qk, v, k_c, v_c, sink, batch, n_q_heads):
    m = qk.shape[0]
    seq, ctx_len = m // batch, k_c.shape[0] // batch
    kv_heads = n_q_heads // ATT_GROUP
    ng = seq // (ATT_BLOCK * ATT_QB)
    assert ng >= 2
    qw = ATT_GROUP * HEAD_DIM
    bias = _band_bias()
    return pl.pallas_call(
        _attn_kernel, grid=(batch, kv_heads, ng),
        in_specs=[pl.BlockSpec(memory_space=pltpu.SMEM),
                  pl.BlockSpec(bias.shape, lambda b, h, n: (0, 0, 0)),
                  pl.BlockSpec((ATT_QB * ATT_BLOCK, qw), lambda b, h, n: (b * ng + n, h)),
                  pl.BlockSpec((seq, HEAD_DIM), lambda b, h, n: (b, n_q_heads + h)),
                  pl.BlockSpec((seq, HEAD_DIM), lambda b, h, n: (b, h)),
                  pl.BlockSpec((ctx_len, HEAD_DIM), lambda b, h, n: (b, h)),
                  pl.BlockSpec((ctx_len, HEAD_DIM), lambda b, h, n: (b, h))],
        out_specs=pl.BlockSpec((ATT_QB * ATT_BLOCK, qw), lambda b, h, n: (b * ng + n, h)),
        out_shape=jax.ShapeDtypeStruct((m, n_q_heads * HEAD_DIM), BF16),
        compiler_params=_cparams("parallel", "parallel", "arbitrary"), name="attention",
    )(sink, bias, qk, qk, v, k_c, v_c)


def _rope_tables(seq):
    rows = seq // GRID_W
    row = jnp.repeat(jnp.arange(rows, dtype=F32), GRID_W)
    col = jnp.tile(jnp.arange(GRID_W, dtype=F32), rows)
    n_freq = HEAD_DIM // 4
    inv_freq = ROPE_THETA ** (-jnp.arange(n_freq, dtype=F32) / n_freq)
    ang = jnp.concatenate([row[:, None] * inv_freq, col[:, None] * inv_freq], axis=-1)
    cos, sin = jnp.cos(ang), jnp.sin(ang)
    return jnp.concatenate([cos, cos], axis=-1), jnp.concatenate([-sin, sin], axis=-1)


def kernel(x, c, ctx, c_ctx, ada_w, ada_b, ffn_w_gate, ffn_w_up, ffn_w_down, even_w_in, even_gate_w2_fwd, even_gate_b_fwd, even_gate_w2_bwd, even_gate_b_bwd, even_gla_norm_gain, even_sgu_norm_gain, even_sgu_w_s, even_sgu_b_s, even_w_out, odd_w_in, odd_q_norm_gain, odd_k_norm_gain, odd_sink, odd_w_out):
    batch, seq, d = x.shape
    ctx_len = ctx.shape[1]
    depth = ada_w.shape[0]
    assert depth == 2 and batch + 1 <= 8
    xl = x.reshape(batch * seq, d)
    xc = ctx.reshape(batch * ctx_len, d)

    cc = jnp.concatenate([c, c_ctx[None], jnp.zeros((8 - batch - 1, d), F32)], axis=0)
    mods = _ada(cc, ada_w, ada_b)

    def mod_l(layer, which):
        return mods[layer, :batch, which * d:(which + 1) * d].reshape(batch, 1, d)

    def mod_c(layer, which):
        return mods[layer, batch:batch + 1, which * d:(which + 1) * d].reshape(1, 1, d)

    all_ctx = batch * ctx_len

    dv = even_gla_norm_gain.shape[1]
    key_w = even_gate_w2_fwd.shape[2]
    dk = key_w // GLA_HEADS
    val_w = GLA_HEADS * dv
    sgu_w = even_sgu_norm_gain.shape[1]
    gate0 = 2 * key_w + 2 * val_w
    gate1 = gate0 + 2 * GLA_GATE_RANK
    w_in_t = jnp.swapaxes(even_w_in, 1, 2)
    w_qkvr = _cast_row_tiles(w_in_t, 0, 0, gate0, TN_WIDE)
    w_us = _cast_row_tiles(w_in_t, 0, gate1, 2 * sgu_w, TN_WIDE)
    w_gate = jnp.pad(w_in_t[0, gate0:gate1], ((0, LANE - 2 * GLA_GATE_RANK), (0, 0))).astype(BF16)[None]
    pad_rows = LANE - GLA_GATE_RANK
    w2f = jnp.pad(even_gate_w2_fwd[0], ((0, pad_rows), (0, 0))).astype(BF16)
    w2b = jnp.pad(even_gate_w2_bwd[0], ((GLA_GATE_RANK, pad_rows - GLA_GATE_RANK), (0, 0))).astype(BF16)
    b2f = even_gate_b_fwd[0].reshape(1, key_w)
    b2b = even_gate_b_bwd[0].reshape(1, key_w)

    h_l = _modulate(xl, mod_l(0, 0), mod_l(0, 1), seq)
    h_c = _modulate(xc, mod_c(0, 0), mod_c(0, 1), all_ctx)
    p_l, p_c = _proj(h_l, w_qkvr, True), _proj(h_c, w_qkvr, True)
    us_l, us_c = _proj(h_l, w_us, True), _proj(h_c, w_us, True)
    g_l, g_c = _proj(h_l, w_gate, True), _proj(h_c, w_gate, True)
    gla_c, gla_l = _gla(p_l, p_c, g_l, g_c, w2f, b2f, w2b, b2b, even_gla_norm_gain[0].reshape(1, dv),
                        batch, dk, dv)
    w_s = even_sgu_w_s[0].astype(BF16)
    b_s_t = even_sgu_b_s[0].T
    sgu_gain = even_sgu_norm_gain[0].reshape(1, sgu_w)
    sgu_l = _sgu(us_l, w_s, b_s_t, sgu_gain, sgu_w, 0, 1)
    sgu_c = _sgu(us_c, w_s, b_s_t, sgu_gain, sgu_w, 0, 1)
    assert val_w == sgu_w
    w_out = _cast_col_tiles(even_w_out, 0, 0, d, TN_WIDE)
    x_l = _out_res([gla_l, sgu_l], w_out, xl, mod_l(0, 2), seq)
    x_c = _out_res([gla_c, sgu_c], w_out, xc, mod_c(0, 2), all_ctx)

    hidden = ffn_w_gate.shape[2]
    assert hidden % FFN_TH == 0
    wgu = _cast_col_tile_pairs(ffn_w_gate, ffn_w_up, 0, FFN_TH)
    wd = _cast_rows(ffn_w_down, 0, FFN_TH)
    x_l = _ffn(x_l, mod_l(0, 3), mod_l(0, 4), mod_l(0, 5), wgu, wd, seq)
    x_c = _ffn(x_c, mod_c(0, 3), mod_c(0, 4), mod_c(0, 5), wgu, wd, all_ctx)

    h_l = _modulate(x_l, mod_l(1, 0), mod_l(1, 1), seq)
    h_c = _modulate(x_c, mod_c(1, 0), mod_c(1, 1), all_ctx)

    n_heads = odd_sink.shape[1]
    q_w = n_heads * HEAD_DIM
    kv_w = q_w // ATT_GROUP
    qk_tile = TN_WIDE
    assert kv_w == qk_tile
    w_qk = _cast_col_tiles(odd_w_in, 0, 0, q_w + kv_w, qk_tile)
    w_v = _cast_col_tiles(odd_w_in, 0, q_w + kv_w, kv_w, TN_WIDE)
    k_gain = jnp.tile(odd_k_norm_gain[0], kv_w // HEAD_DIM)
    qk_gain = jnp.concatenate([jnp.tile(odd_q_norm_gain[0] * (HEAD_DIM ** -0.5 * LOG2E), n_heads),
                               k_gain]).reshape(1, -1)
    cos2, sin2 = _rope_tables(seq)
    qk_l = _qk_proj(h_l, w_qk, 0, w_qk.shape[0], qk_gain, cos2, sin2, rope=True)
    v_l = _proj(h_l, w_v)
    k_c = _qk_proj(h_c, w_qk, q_w // qk_tile, 1, k_gain.reshape(1, -1), cos2, sin2, rope=False)
    v_c = _proj(h_c, w_v)
    att = _attention(qk_l, v_l, k_c, v_c, odd_sink[0], batch, n_heads)
    x_l = _out_res([att], _cast_col_tiles(odd_w_out, 0, 0, d, TN_WIDE), x_l, mod_l(1, 2), seq)

    wgu = _cast_col_tile_pairs(ffn_w_gate, ffn_w_up, 1, FFN_TH)
    wd = _cast_rows(ffn_w_down, 1, FFN_TH)
    return _ffn(x_l, mod_l(1, 3), mod_l(1, 4), mod_l(1, 5), wgu, wd, seq).reshape(batch, seq, d)
```

```python
import functools

import numpy as np
import jax
import jax.numpy as jnp
from jax import lax
from jax.experimental import pallas as pl
from jax.experimental.pallas import tpu as pltpu

F32 = jnp.float32
BF16 = jnp.bfloat16

EPS = 1e-6
LOG2E = float(np.log2(np.e))
N_MOD = 6
GRID_W = 64
ROPE_THETA = 10000.0

GLA_HEADS = 4
GLA_TAU = 16.0
GLA_GATE_RANK = 16
GLA_BLOCK = 64
GLA_TILE = 256
SGU_GROUPS = 8
SGU_CHUNK = 128
HEAD_DIM = 128
ATT_GROUP = 4
WINDOW = 128
ATT_BLOCK = 128
ATT_QB = 8

LANE = 128
MXU_W = 256
VMEM_LIMIT = 56 * 1024 * 1024
FFN_VMEM_LIMIT = 60 * 1024 * 1024

TM = 1024
TN = 512
TN_WIDE = 1024
FFN_TH = 256
X_ROWS = 128
X_GROUP = 16
ROWS_EW = 512


def _cparams(*sem):
    return pltpu.CompilerParams(dimension_semantics=sem, vmem_limit_bytes=VMEM_LIMIT)


def _dot(a, b):
    return jnp.dot(a, b, preferred_element_type=F32)


def _dot_nt(a, b):
    return lax.dot_general(a, b, (((1,), (1,)), ((), ())), preferred_element_type=F32)


def _dot_tn(a, b):
    return lax.dot_general(a, b, (((0,), (0,)), ((), ())), preferred_element_type=F32)


def _silu(x):
    return x * jax.nn.sigmoid(x)


def _gelu_tanh(x):
    c = np.float32(np.sqrt(2.0 / np.pi))
    return x * (0.5 * (1.0 + jnp.tanh(c * (x + 0.044715 * (x * x * x)))))


def _rms(x):
    return x * lax.rsqrt(jnp.mean(x * x, axis=-1, keepdims=True) + EPS)


def _ada_kernel(c_ref, w_ref, b_ref, o_ref):
    a = _silu(c_ref[...]).astype(BF16)
    o_ref[...] = _dot(a, w_ref[...].astype(BF16)) + b_ref[...]


def _ada(cc, ada_w, ada_b):
    depth, d, n = ada_w.shape
    rows = cc.shape[0]
    return pl.pallas_call(
        _ada_kernel,
        grid=(depth, n // TN),
        in_specs=[pl.BlockSpec((rows, d), lambda l, j: (0, 0)),
                  pl.BlockSpec((None, d, TN), lambda l, j: (l, 0, j)),
                  pl.BlockSpec((None, 1, TN), lambda l, j: (l, 0, j))],
        out_specs=pl.BlockSpec((None, rows, TN), lambda l, j: (l, 0, j)),
        out_shape=jax.ShapeDtypeStruct((depth, rows, n), F32),
        compiler_params=_cparams("parallel", "parallel"),
        name="ada",
    )(cc, ada_w, ada_b.reshape(depth, 1, n))


def _modulate_kernel(x_ref, sh_ref, sc_ref, h_ref):
    h_ref[...] = (_rms(x_ref[...]) * (1.0 + sc_ref[...]) + sh_ref[...]).astype(h_ref.dtype)


def _row_spec(d):
    return pl.BlockSpec((ROWS_EW, d), lambda i: (i, 0))


def _vec_spec(d, rows_per_vec):
    return pl.BlockSpec((None, 1, d), lambda i: ((i * ROWS_EW) // rows_per_vec, 0, 0))


def _modulate(x, shift, scale, rows_per_vec):
    m, d = x.shape
    return pl.pallas_call(
        _modulate_kernel, grid=(m // ROWS_EW,),
        in_specs=[_row_spec(d), _vec_spec(d, rows_per_vec), _vec_spec(d, rows_per_vec)],
        out_specs=_row_spec(d),
        out_shape=jax.ShapeDtypeStruct((m, d), BF16),
        compiler_params=_cparams("parallel"), name="modulate",
    )(x, shift, scale)


def _cast_kernel(w_ref, o_ref):
    o_ref[...] = w_ref[...].astype(o_ref.dtype)


def _cast_col_tiles(w, layer, col0, n_cols, tile):
    k = w.shape[1]
    assert col0 % tile == 0 and n_cols % tile == 0
    t0 = col0 // tile
    return pl.pallas_call(
        _cast_kernel, grid=(n_cols // tile,),
        in_specs=[pl.BlockSpec((None, k, tile), lambda j: (layer, 0, t0 + j))],
        out_specs=pl.BlockSpec((None, k, tile), lambda j: (j, 0, 0)),
        out_shape=jax.ShapeDtypeStruct((n_cols // tile, k, tile), BF16),
        compiler_params=_cparams("parallel"), name="cast_cols",
    )(w)


def _cast_pair_kernel(a_ref, b_ref, o_ref):
    tile = a_ref.shape[1]
    o_ref[:, :tile] = a_ref[...].astype(o_ref.dtype)
    o_ref[:, tile:] = b_ref[...].astype(o_ref.dtype)


def _cast_col_tile_pairs(wa, wb, layer, tile):
    k, n = wa.shape[1:]
    src = pl.BlockSpec((None, k, tile), lambda j: (layer, 0, j))
    return pl.pallas_call(
        _cast_pair_kernel, grid=(n // tile,),
        in_specs=[src, src],
        out_specs=pl.BlockSpec((None, k, 2 * tile), lambda j: (j, 0, 0)),
        out_shape=jax.ShapeDtypeStruct((n // tile, k, 2 * tile), BF16),
        compiler_params=_cparams("parallel"), name="cast_pairs",
    )(wa, wb)


def _cast_row_tiles(wt, layer, row0, n_rows, tile):
    k = wt.shape[2]
    assert row0 % 8 == 0 and n_rows % tile == 0
    return pl.pallas_call(
        _cast_kernel, grid=(n_rows // tile,),
        in_specs=[pl.BlockSpec((pl.Element(1), pl.Element(tile), pl.Element(k)),
                               lambda j: (layer, pl.multiple_of(row0 + j * tile, 8), 0))],
        out_specs=pl.BlockSpec((1, tile, k), lambda j: (j, 0, 0)),
        out_shape=jax.ShapeDtypeStruct((n_rows // tile, tile, k), BF16),
        compiler_params=_cparams("parallel"), name="cast_row_tiles",
    )(wt)


def _cast_rows(w, layer, tile):
    k, n = w.shape[1:]
    return pl.pallas_call(
        _cast_kernel, grid=(k // tile,),
        in_specs=[pl.BlockSpec((None, tile, n), lambda j: (layer, j, 0))],
        out_specs=pl.BlockSpec((tile, n), lambda j: (j, 0)),
        out_shape=jax.ShapeDtypeStruct((k, n), BF16),
        compiler_params=_cparams("parallel"), name="cast_rows",
    )(w)


def _proj_kernel(x_ref, w_ref, o_ref, *, transposed):
    dot = _dot_nt if transposed else _dot
    o_ref[...] = dot(x_ref[...], w_ref[...]).astype(o_ref.dtype)


def _proj(x, w3, transposed=False):
    m, k = x.shape
    nt = w3.shape[0]
    tn = w3.shape[1] if transposed else w3.shape[2]
    tm = min(TM, m)
    return pl.pallas_call(
        functools.partial(_proj_kernel, transposed=transposed), grid=(m // tm, nt),
        in_specs=[pl.BlockSpec((tm, k), lambda i, j: (i, 0)),
                  pl.BlockSpec((None,) + w3.shape[1:], lambda i, j: (j, 0, 0))],
        out_specs=pl.BlockSpec((tm, tn), lambda i, j: (i, j)),
        out_shape=jax.ShapeDtypeStruct((m, nt * tn), BF16),
        compiler_params=_cparams("parallel", "parallel"), name="proj",
    )(x, w3)


def _qk_proj_kernel(x_ref, w_ref, gain_ref, cos_ref, sin_ref, ones_ref, o_ref, *, rope):
    half = x_ref.shape[0] // 2
    for rr in range(2):
        rows = slice(rr * half, (rr + 1) * half)
        x = x_ref[rows, :]
        for cc in range(w_ref.shape[1] // TN):
            acc = _dot(x, w_ref[:, cc * TN:(cc + 1) * TN])
            for tt in range(TN // MXU_W):
                c0 = cc * TN + tt * MXU_W
                a = acc[:, tt * MXU_W:(tt + 1) * MXU_W]
                ssq = _dot((a * a).astype(BF16), ones_ref[...])
                y = a * lax.rsqrt(ssq * (1.0 / HEAD_DIM) + EPS) * gain_ref[:, c0:c0 + MXU_W]
                for hh in range(MXU_W // HEAD_DIM):
                    yh = y[:, hh * HEAD_DIM:(hh + 1) * HEAD_DIM]
                    if rope:
                        yh = (yh * cos_ref[rows, :]
                              + pltpu.roll(yh, HEAD_DIM // 2, 1) * sin_ref[rows, :])
                    o_ref[rows, c0 + hh * HEAD_DIM:c0 + (hh + 1) * HEAD_DIM] = yh.astype(o_ref.dtype)


def _qk_proj(x, w3, tile0, n_tiles, gain, cos2, sin2, rope):
    m, k = x.shape
    tn = w3.shape[2]
    tm = min(TM, m)
    seq_blocks = cos2.shape[0] // tm if rope else 1
    tab = pl.BlockSpec((tm, HEAD_DIM), lambda i, j: (i % seq_blocks, 0))
    head_of = np.arange(MXU_W) // HEAD_DIM
    ones = jnp.asarray(head_of[:, None] == head_of[None, :], BF16)
    return pl.pallas_call(
        functools.partial(_qk_proj_kernel, rope=rope), grid=(m // tm, n_tiles),
        in_specs=[pl.BlockSpec((tm, k), lambda i, j: (i, 0)),
                  pl.BlockSpec((None, k, tn), lambda i, j: (tile0 + j, 0, 0)),
                  pl.BlockSpec((1, tn), lambda i, j: (0, j)),
                  tab, tab,
                  pl.BlockSpec((MXU_W, MXU_W), lambda i, j: (0, 0))],
        out_specs=pl.BlockSpec((tm, tn), lambda i, j: (i, j)),
        out_shape=jax.ShapeDtypeStruct((m, n_tiles * tn), BF16),
        compiler_params=_cparams("parallel", "parallel"), name="qk_proj",
    )(x, w3, gain, cos2, sin2, ones)


def _out_res_kernel(*refs, n_in):
    xs, ws = refs[:n_in], refs[n_in:2 * n_in]
    res_ref, g_ref, o_ref = refs[2 * n_in:]
    acc = _dot(xs[0][...], ws[0][...])
    for x_ref, w_ref in zip(xs[1:], ws[1:]):
        acc = acc + _dot(x_ref[...], w_ref[...])
    o_ref[...] = res_ref[...] + g_ref[...] * acc


def _out_res(xs, w3, res, gate, rows_per_vec):
    m, n = res.shape
    nt, k, tn = w3.shape
    tm = min(TM, m)
    n_in = len(xs)
    kx = k // n_in
    assert all(x.shape[1] == kx for x in xs) and nt * tn == n

    def w_spec(part):
        return pl.BlockSpec((None, kx, tn), lambda i, j: (j, part, 0))

    in_specs = ([pl.BlockSpec((tm, kx), lambda i, j: (i, 0)) for _ in xs]
                + [w_spec(part) for part in range(n_in)]
                + [pl.BlockSpec((tm, tn), lambda i, j: (i, j)),
                   pl.BlockSpec((None, 1, tn), lambda i, j: ((i * tm) // rows_per_vec, 0, j))])
    return pl.pallas_call(
        functools.partial(_out_res_kernel, n_in=n_in), grid=(m // tm, nt),
        in_specs=in_specs,
        out_specs=pl.BlockSpec((tm, tn), lambda i, j: (i, j)),
        out_shape=jax.ShapeDtypeStruct((m, n), F32),
        compiler_params=_cparams("parallel", "parallel"), name="out_res",
    )(*xs, *([w3] * n_in), res, gate)


def _ffn_kernel(x_hbm, sh_ref, sc_ref, gate_ref, wgu_ref, wd_ref, *rest):
    if len(rest) == 3:
        o_ref, h_ref, sem = rest
        side = None
    else:
        wg_next, wu_next, wd_next, o_ref, wgu_next_out, wd_next_out, h_ref, sem = rest
        side = True
    i, j = pl.program_id(0), pl.program_id(1)
    tm = o_ref.shape[0]
    n_chunks = tm // X_ROWS

    def x_copy(ci):
        rows = pl.ds(ci * X_ROWS, X_ROWS)
        src = x_hbm.at[pl.ds(i * tm + ci * X_ROWS, X_ROWS), :]
        return pltpu.make_async_copy(src, o_ref.at[rows, :], sem.at[ci])

    @pl.when(j == 0)
    def _():
        for ci in range(n_chunks):
            x_copy(ci).start()
        for ci in range(n_chunks):
            x_copy(ci).wait()

            def group(gi, carry, ci=ci):
                rows = pl.ds(pl.multiple_of(ci * X_ROWS + gi * X_GROUP, X_GROUP), X_GROUP)
                h_ref[rows, :] = (_rms(o_ref[rows, :]) * (1.0 + sc_ref[...]) + sh_ref[...]).astype(BF16)
                return carry
            lax.fori_loop(0, X_ROWS // X_GROUP, group, 0, unroll=True)

    gu = _dot(h_ref[...], wgu_ref[...])
    a = (_silu(gu[:, :FFN_TH]) * gu[:, FFN_TH:]).astype(BF16)
    o_ref[...] += gate_ref[...] * _dot(a, wd_ref[...])

    if side:
        wgu_next_out[:, :FFN_TH] = wg_next[...].astype(BF16)
        wgu_next_out[:, FFN_TH:] = wu_next[...].astype(BF16)
        wd_next_out[...] = wd_next[...].astype(BF16)


def _ffn(x, shift, scale, gate, wgu3, wd, rows_per_vec, cast_next=None):
    m, d = x.shape
    n_tiles = wgu3.shape[0]
    tm = min(TM, m)
    n_rows = m // tm
    vec = pl.BlockSpec((None, 1, d), lambda i, j: ((i * tm) // rows_per_vec, 0, 0))
    in_specs = [pl.BlockSpec(memory_space=pl.ANY),
                vec, vec, vec,
                pl.BlockSpec((None, d, 2 * FFN_TH), lambda i, j: (j, 0, 0)),
                pl.BlockSpec((FFN_TH, d), lambda i, j: (j, 0))]
    out_specs = [pl.BlockSpec((tm, d), lambda i, j: (i, 0))]
    out_shape = [jax.ShapeDtypeStruct((m, d), F32)]
    args = [x, shift, scale, gate, wgu3, wd]
    if cast_next is not None:
        w_gate, w_up, w_down, layer = cast_next
        blk = d // n_rows
        assert d % n_rows == 0 and blk % LANE == 0
        col_tile = pl.BlockSpec((None, blk, FFN_TH), lambda i, j: (layer, i, j))
        in_specs += [col_tile, col_tile, pl.BlockSpec((None, FFN_TH, blk), lambda i, j: (layer, j, i))]
        out_specs += [pl.BlockSpec((None, blk, 2 * FFN_TH), lambda i, j: (j, i, 0)),
                      pl.BlockSpec((FFN_TH, blk), lambda i, j: (j, i))]
        out_shape += [jax.ShapeDtypeStruct(wgu3.shape, BF16), jax.ShapeDtypeStruct(wd.shape, BF16)]
        args += [w_gate, w_up, w_down]
    res = pl.pallas_call(
        _ffn_kernel, grid=(n_rows, n_tiles),
        in_specs=in_specs, out_specs=out_specs, out_shape=out_shape,
        scratch_shapes=[pltpu.VMEM((tm, d), BF16), pltpu.SemaphoreType.DMA((tm // X_ROWS,))],
        compiler_params=pltpu.CompilerParams(dimension_semantics=("parallel", "arbitrary"),
                                             vmem_limit_bytes=FFN_VMEM_LIMIT), name="ffn",
    )(*args)
    return res[0] if cast_next is None else res


def _gla_scan_tile(q_ref, k_ref, v_ref, g_ref, w2_ref, b2_ref, s_ref, emit, reverse):
    dk = q_ref.shape[1] // GLA_HEADS
    dv = v_ref.shape[1] // GLA_HEADS
    row = lax.broadcasted_iota(jnp.int32, (GLA_BLOCK, GLA_BLOCK), 0)
    col = lax.broadcasted_iota(jnp.int32, (GLA_BLOCK, GLA_BLOCK), 1)
    if reverse:
        tri = (col >= row).astype(BF16)
        mask = col > row
        last_row = 0
    else:
        tri = (col <= row).astype(BF16)
        mask = col <= row
        last_row = GLA_BLOCK - 1
    n_steps = q_ref.shape[0] // GLA_BLOCK
    order = [n_steps - 1 - st if reverse else st for st in range(n_steps)]

    parts = []
    for c in order:
        rows = slice(c * GLA_BLOCK, (c + 1) * GLA_BLOCK)
        q = q_ref[rows, :].astype(F32) * (dk ** -0.5)
        k = k_ref[rows, :].astype(F32)
        x = _dot(g_ref[rows, :], w2_ref[...]) + b2_ref[...]
        log_a = (jnp.minimum(x, 0.0) - jnp.log(1.0 + jnp.exp(-jnp.abs(x)))) * (1.0 / GLA_TAU)
        hi = log_a.astype(BF16)
        lo = (log_a - hi.astype(F32)).astype(BF16)
        b = _dot(tri, hi) + _dot(tri, lo)
        mid = b[GLA_BLOCK // 2:GLA_BLOCK // 2 + 1, :]
        b_last = b[last_row:last_row + 1, :]
        qe = q * jnp.exp(b - mid)
        ke = k * jnp.exp(mid - b)
        q_in = (qe * jnp.exp(mid)).astype(BF16)
        k_out = (ke * jnp.exp(b_last - mid)).astype(BF16)
        decay = jnp.exp(b_last)
        qe, ke = qe.astype(BF16), ke.astype(BF16)
        intra, kv = [], []
        for h in range(GLA_HEADS):
            ks = slice(h * dk, (h + 1) * dk)
            v = v_ref[rows, h * dv:(h + 1) * dv]
            att = jnp.where(mask, _dot_nt(qe[:, ks], ke[:, ks]), 0.0).astype(BF16)
            intra.append(_dot(att, v))
            kv.append(_dot_tn(v, k_out[:, ks]))
        parts.append((rows, q_in, decay, intra, kv))

    for h in range(GLA_HEADS):
        ks = slice(h * dk, (h + 1) * dk)
        s = s_ref[h]
        for rows, q_in, decay, intra, kv in parts:
            emit(rows, h, intra[h] + _dot_nt(q_in[:, ks], s.astype(BF16)))
            s = s * decay[:, ks] + kv[h]
        s_ref[h] = s


def _gla_rev_kernel(qc_ref, kc_ref, vc_ref, gc_ref, ql_ref, kl_ref, vl_ref, gl_ref, w2_ref, b2_ref,
                    oc_ref, ol_ref, s_ref):
    t = pl.program_id(1)
    dv = vc_ref.shape[1] // GLA_HEADS

    def run(q_ref, k_ref, v_ref, g_ref, o_ref):
        def emit(rows, h, o):
            o_ref[rows, h * dv:(h + 1) * dv] = o
        _gla_scan_tile(q_ref, k_ref, v_ref, g_ref, w2_ref, b2_ref, s_ref, emit, reverse=True)

    @pl.when(t == 0)
    def _():
        s_ref[...] = jnp.zeros_like(s_ref)
        run(qc_ref, kc_ref, vc_ref, gc_ref, oc_ref)

    @pl.when(t > 0)
    def _():
        run(ql_ref, kl_ref, vl_ref, gl_ref, ol_ref)


def _gla_fwd_kernel(qc_ref, kc_ref, vc_ref, rc_ref, gc_ref, pc_ref, ql_ref, kl_ref, vl_ref, rl_ref, gl_ref, pl_ref,
                    w2_ref, b2_ref, gain_ref, oc_ref, ol_ref, s_ref):
    t = pl.program_id(1)
    dv = vc_ref.shape[1] // GLA_HEADS

    def run(q_ref, k_ref, v_ref, r_ref, g_ref, p_ref, o_ref):
        def emit(rows, h, o):
            cols = slice(h * dv, (h + 1) * dv)
            total = o + p_ref[rows, cols]
            r = r_ref[rows, cols].astype(F32)
            o_ref[rows, cols] = (_rms(total) * gain_ref[...] * _silu(r)).astype(o_ref.dtype)
        _gla_scan_tile(q_ref, k_ref, v_ref, g_ref, w2_ref, b2_ref, s_ref, emit, reverse=False)

    @pl.when(t == 0)
    def _():
        s_ref[...] = jnp.zeros_like(s_ref)
        run(qc_ref, kc_ref, vc_ref, rc_ref, gc_ref, pc_ref, oc_ref)

    @pl.when(t > 0)
    def _():
        run(ql_ref, kl_ref, vl_ref, rl_ref, gl_ref, pl_ref, ol_ref)


def _gla(p_l, p_c, g_l, g_c, w2f, b2f, w2b, b2b, gain, batch, dk, dv):
    seq, ctx_len = p_l.shape[0] // batch, p_c.shape[0] // batch
    assert ctx_len == GLA_TILE and seq % GLA_TILE == 0
    n_lat = seq // GLA_TILE
    key_w, val_w = GLA_HEADS * dk, GLA_HEADS * dv
    assert val_w == 2 * key_w

    def lat_fwd(b, t):
        return b * n_lat + jnp.maximum(t - 1, 0)

    def lat_rev(b, t):
        return b * n_lat + n_lat - jnp.maximum(t, 1)

    def tok_specs(row_blk, with_r):
        specs = [pl.BlockSpec((GLA_TILE, key_w), lambda b, t: (row_blk(b, t), 0)),
                 pl.BlockSpec((GLA_TILE, key_w), lambda b, t: (row_blk(b, t), 1)),
                 pl.BlockSpec((GLA_TILE, val_w), lambda b, t: (row_blk(b, t), 1))]
        if with_r:
            specs.append(pl.BlockSpec((GLA_TILE, val_w), lambda b, t: (row_blk(b, t), 2)))
        return specs + [pl.BlockSpec((GLA_TILE, LANE), lambda b, t: (row_blk(b, t), 0))]

    def out_spec(row_blk):
        return pl.BlockSpec((GLA_TILE, val_w), lambda b, t: (row_blk(b, t), 0))

    def ctx_blk(b, t):
        return b

    w2_spec = pl.BlockSpec((LANE, key_w), lambda b, t: (0, 0))
    b2_spec = pl.BlockSpec((1, key_w), lambda b, t: (0, 0))
    state = pltpu.VMEM((GLA_HEADS, dv, dk), F32)
    grid = (batch, 1 + n_lat)
    rev_c, rev_l = pl.pallas_call(
        _gla_rev_kernel, grid=grid,
        in_specs=tok_specs(ctx_blk, False) + tok_specs(lat_rev, False) + [w2_spec, b2_spec],
        out_specs=[out_spec(ctx_blk), out_spec(lat_rev)],
        out_shape=[jax.ShapeDtypeStruct((batch * ctx_len, val_w), F32),
                   jax.ShapeDtypeStruct((batch * seq, val_w), F32)],
        scratch_shapes=[state],
        compiler_params=_cparams("parallel", "arbitrary"), name="gla_rev",
    )(p_c, p_c, p_c, g_c, p_l, p_l, p_l, g_l, w2b, b2b)
    return pl.pallas_call(
        _gla_fwd_kernel, grid=grid,
        in_specs=tok_specs(ctx_blk, True) + [out_spec(ctx_blk)] + tok_specs(lat_fwd, True) + [out_spec(lat_fwd)]
        + [w2_spec, b2_spec, pl.BlockSpec((1, dv), lambda b, t: (0, 0))],
        out_specs=[out_spec(ctx_blk), out_spec(lat_fwd)],
        out_shape=[jax.ShapeDtypeStruct((batch * ctx_len, val_w), BF16),
                   jax.ShapeDtypeStruct((batch * seq, val_w), BF16)],
        scratch_shapes=[state],
        compiler_params=_cparams("parallel", "arbitrary"), name="gla_fwd",
    )(p_c, p_c, p_c, p_c, g_c, rev_c, p_l, p_l, p_l, p_l, g_l, rev_l, w2f, b2f, gain)


def _sgu_kernel(u_ref, s_ref, ws_ref, bs_ref, gain_ref, o_ref):
    gw = u_ref.shape[1] // SGU_GROUPS
    for ci in range(u_ref.shape[0] // SGU_CHUNK):
        rows = slice(ci * SGU_CHUNK, (ci + 1) * SGU_CHUNK)
        for g in range(SGU_GROUPS):
            cols = slice(g * gw, (g + 1) * gw)
            vg = _rms(_gelu_tanh(s_ref[rows, cols].astype(F32))) * gain_ref[:, cols]
            mixed = _dot(ws_ref[g], vg.astype(BF16)) + bs_ref[:, g:g + 1]
            o_ref[rows, cols] = (_gelu_tanh(u_ref[rows, cols].astype(F32)) * mixed).astype(o_ref.dtype)


def _sgu(p, w_s, b_s_t, gain, width, u_blk, s_blk):
    m = p.shape[0]
    rows = 2 * SGU_CHUNK
    return pl.pallas_call(
        _sgu_kernel, grid=(m // rows,),
        in_specs=[pl.BlockSpec((rows, width), lambda i: (i, u_blk)),
                  pl.BlockSpec((rows, width), lambda i: (i, s_blk)),
                  pl.BlockSpec(w_s.shape, lambda i: (0, 0, 0)),
                  pl.BlockSpec(b_s_t.shape, lambda i: (0, 0)),
                  pl.BlockSpec((1, width), lambda i: (0, 0))],
        out_specs=pl.BlockSpec((rows, width), lambda i: (i, 0)),
        out_shape=jax.ShapeDtypeStruct((m, width), BF16),
        compiler_params=_cparams("parallel"), name="sgu",
    )(p, p, w_s, b_s_t, gain)


def _attn_kernel(sink_ref, bias_ref, q_ref, k_ref, v_ref, kc_ref, vc_ref, o_ref):
    kv, ng = pl.program_id(1), pl.program_id(2)
    seq = k_ref.shape[0]
    nb = seq // ATT_BLOCK
    band = 3 * ATT_BLOCK
    sink = jnp.concatenate([jnp.full((ATT_BLOCK, HEAD_DIM), sink_ref[kv * ATT_GROUP + g] * LOG2E, F32)
                            for g in range(ATT_GROUP)], axis=0)
    k_ctx, v_ctx = kc_ref[...], vc_ref[...]
    ones = jnp.ones((band + k_ctx.shape[0], HEAD_DIM), BF16)
    for jj in range(ATT_QB):
        n = ng * ATT_QB + jj
        start = pl.multiple_of(jnp.clip((n - 1) * ATT_BLOCK, 0, seq - band), ATT_BLOCK)
        if jj == 0:
            bias = bias_ref[jnp.where(n == 0, 1, 0)]
        elif jj == ATT_QB - 1:
            bias = bias_ref[jnp.where(n == nb - 1, 2, 0)]
        else:
            bias = bias_ref[0]
        q = q_ref[jj * ATT_BLOCK:(jj + 1) * ATT_BLOCK, :]
        qs = jnp.concatenate([q[:, g * HEAD_DIM:(g + 1) * HEAD_DIM] for g in range(ATT_GROUP)], axis=0)
        keys = jnp.concatenate([k_ref[pl.ds(start, band), :], k_ctx], axis=0)
        vals = jnp.concatenate([v_ref[pl.ds(start, band), :], v_ctx], axis=0)
        s = _dot_nt(qs, keys)
        s = jnp.concatenate([s[:, :band] + bias, s[:, band:]], axis=1)
        m = jnp.maximum(sink, jnp.max(s, axis=-1, keepdims=True))
        p = jnp.exp2(s - jnp.concatenate([m] * (s.shape[1] // HEAD_DIM), axis=1)).astype(BF16)
        ov = _dot(p, jnp.concatenate([vals, ones], axis=1))
        o = ov[:, :HEAD_DIM] / (ov[:, HEAD_DIM:] + jnp.exp2(sink - m))
        o_ref[jj * ATT_BLOCK:(jj + 1) * ATT_BLOCK, :] = jnp.concatenate(
            [o[g * ATT_BLOCK:(g + 1) * ATT_BLOCK] for g in range(ATT_GROUP)], axis=1).astype(o_ref.dtype)


def _band_bias():
    t = (np.arange(ATT_GROUP * ATT_BLOCK) % ATT_BLOCK)[:, None]
    i = np.arange(3 * ATT_BLOCK)[None, :]
    shifts = (-ATT_BLOCK, 0, -2 * ATT_BLOCK)
    return jnp.asarray(np.stack([np.where(np.abs(i + sh - t) <= WINDOW, 0.0, -np.inf) for sh in shifts]), F32)


def _attention(qk, v, k_c, v_c, sink, batch, n_q_heads):
    m = qk.shape[0]
    seq, ctx_len = m // batch, k_c.shape[0] // batch
    kv_heads = n_q_heads // ATT_GROUP
    ng = seq // (ATT_BLOCK * ATT_QB)
    assert ng >= 2
    qw = ATT_GROUP * HEAD_DIM
    bias = _band_bias()
    return pl.pallas_call(
        _attn_kernel, grid=(batch, kv_heads, ng),
        in_specs=[pl.BlockSpec(memory_space=pltpu.SMEM),
                  pl.BlockSpec(bias.shape, lambda b, h, n: (0, 0, 0)),
                  pl.BlockSpec((ATT_QB * ATT_BLOCK, qw), lambda b, h, n: (b * ng + n, h)),
                  pl.BlockSpec((seq, HEAD_DIM), lambda b, h, n: (b, n_q_heads + h)),
                  pl.BlockSpec((seq, HEAD_DIM), lambda b, h, n: (b, h)),
                  pl.BlockSpec((ctx_len, HEAD_DIM), lambda b, h, n: (b, h)),
                  pl.BlockSpec((ctx_len, HEAD_DIM), lambda b, h, n: (b, h))],
        out_specs=pl.BlockSpec((ATT_QB * ATT_BLOCK, qw), lambda b, h, n: (b * ng + n, h)),
        out_shape=jax.ShapeDtypeStruct((m, n_q_heads * HEAD_DIM), BF16),
        compiler_params=_cparams("parallel", "parallel", "arbitrary"), name="attention",
    )(sink, bias, qk, qk, v, k_c, v_c)


def _rope_tables(seq):
    rows = seq // GRID_W
    row = jnp.repeat(jnp.arange(rows, dtype=F32), GRID_W)
    col = jnp.tile(jnp.arange(GRID_W, dtype=F32), rows)
    n_freq = HEAD_DIM // 4
    inv_freq = ROPE_THETA ** (-jnp.arange(n_freq, dtype=F32) / n_freq)
    ang = jnp.concatenate([row[:, None] * inv_freq, col[:, None] * inv_freq], axis=-1)
    cos, sin = jnp.cos(ang), jnp.sin(ang)
    return jnp.concatenate([cos, cos], axis=-1), jnp.concatenate([-sin, sin], axis=-1)


def kernel(x, c, ctx, c_ctx, ada_w, ada_b, ffn_w_gate, ffn_w_up, ffn_w_down, even_w_in, even_gate_w2_fwd, even_gate_b_fwd, even_gate_w2_bwd, even_gate_b_bwd, even_gla_norm_gain, even_sgu_norm_gain, even_sgu_w_s, even_sgu_b_s, even_w_out, odd_w_in, odd_q_norm_gain, odd_k_norm_gain, odd_sink, odd_w_out):
    batch, seq, d = x.shape
    ctx_len = ctx.shape[1]
    depth = ada_w.shape[0]
    assert depth == 2 and batch + 1 <= 8
    xl = x.reshape(batch * seq, d)
    xc = ctx.reshape(batch * ctx_len, d)

    cc = jnp.concatenate([c, c_ctx[None], jnp.zeros((8 - batch - 1, d), F32)], axis=0)
    mods = _ada(cc, ada_w, ada_b)

    def mod_l(layer, which):
        return mods[layer, :batch, which * d:(which + 1) * d].reshape(batch, 1, d)

    def mod_c(layer, which):
        return mods[layer, batch:batch + 1, which * d:(which + 1) * d].reshape(1, 1, d)

    all_ctx = batch * ctx_len

    dv = even_gla_norm_gain.shape[1]
    key_w = even_gate_w2_fwd.shape[2]
    dk = key_w // GLA_HEADS
    val_w = GLA_HEADS * dv
    sgu_w = even_sgu_norm_gain.shape[1]
    gate0 = 2 * key_w + 2 * val_w
    gate1 = gate0 + 2 * GLA_GATE_RANK
    w_in_t = jnp.swapaxes(even_w_in, 1, 2)
    w_qkvr = _cast_row_tiles(w_in_t, 0, 0, gate0, TN_WIDE)
    w_us = _cast_row_tiles(w_in_t, 0, gate1, 2 * sgu_w, TN_WIDE)
    w_gate = jnp.pad(w_in_t[0, gate0:gate1], ((0, LANE - 2 * GLA_GATE_RANK), (0, 0))).astype(BF16)[None]
    pad_rows = LANE - GLA_GATE_RANK
    w2f = jnp.pad(even_gate_w2_fwd[0], ((0, pad_rows), (0, 0))).astype(BF16)
    w2b = jnp.pad(even_gate_w2_bwd[0], ((GLA_GATE_RANK, pad_rows - GLA_GATE_RANK), (0, 0))).astype(BF16)
    b2f = even_gate_b_fwd[0].reshape(1, key_w)
    b2b = even_gate_b_bwd[0].reshape(1, key_w)

    h_l = _modulate(xl, mod_l(0, 0), mod_l(0, 1), seq)
    h_c = _modulate(xc, mod_c(0, 0), mod_c(0, 1), all_ctx)
    p_l, p_c = _proj(h_l, w_qkvr, True), _proj(h_c, w_qkvr, True)
    us_l, us_c = _proj(h_l, w_us, True), _proj(h_c, w_us, True)
    g_l, g_c = _proj(h_l, w_gate, True), _proj(h_c, w_gate, True)
    gla_c, gla_l = _gla(p_l, p_c, g_l, g_c, w2f, b2f, w2b, b2b, even_gla_norm_gain[0].reshape(1, dv),
                        batch, dk, dv)
    w_s = even_sgu_w_s[0].astype(BF16)
    b_s_t = even_sgu_b_s[0].T
    sgu_gain = even_sgu_norm_gain[0].reshape(1, sgu_w)
    sgu_l = _sgu(us_l, w_s, b_s_t, sgu_gain, sgu_w, 0, 1)
    sgu_c = _sgu(us_c, w_s, b_s_t, sgu_gain, sgu_w, 0, 1)
    assert val_w == sgu_w
    w_out = _cast_col_tiles(even_w_out, 0, 0, d, TN_WIDE)
    x_l = _out_res([gla_l, sgu_l], w_out, xl, mod_l(0, 2), seq)
    x_c = _out_res([gla_c, sgu_c], w_out, xc, mod_c(0, 2), all_ctx)

    hidden = ffn_w_gate.shape[2]
    assert hidden % FFN_TH == 0
    wgu = _cast_col_tile_pairs(ffn_w_gate, ffn_w_up, 0, FFN_TH)
    wd = _cast_rows(ffn_w_down, 0, FFN_TH)
    x_l, wgu_next, wd_next = _ffn(x_l, mod_l(0, 3), mod_l(0, 4), mod_l(0, 5), wgu, wd, seq,
                                  cast_next=(ffn_w_gate, ffn_w_up, ffn_w_down, 1))
    x_c = _ffn(x_c, mod_c(0, 3), mod_c(0, 4), mod_c(0, 5), wgu, wd, all_ctx)

    h_l = _modulate(x_l, mod_l(1, 0), mod_l(1, 1), seq)
    h_c = _modulate(x_c, mod_c(1, 0), mod_c(1, 1), all_ctx)

    n_heads = odd_sink.shape[1]
    q_w = n_heads * HEAD_DIM
    kv_w = q_w // ATT_GROUP
    qk_tile = TN_WIDE
    assert kv_w == qk_tile
    w_qk = _cast_col_tiles(odd_w_in, 0, 0, q_w + kv_w, qk_tile)
    w_v = _cast_col_tiles(odd_w_in, 0, q_w + kv_w, kv_w, TN_WIDE)
    k_gain = jnp.tile(odd_k_norm_gain[0], kv_w // HEAD_DIM)
    qk_gain = jnp.concatenate([jnp.tile(odd_q_norm_gain[0] * (HEAD_DIM ** -0.5 * LOG2E), n_heads),
                               k_gain]).reshape(1, -1)
    cos2, sin2 = _rope_tables(seq)
    qk_l = _qk_proj(h_l, w_qk, 0, w_qk.shape[0], qk_gain, cos2, sin2, rope=True)
    v_l = _proj(h_l, w_v)
    k_c = _qk_proj(h_c, w_qk, q_w // qk_tile, 1, k_gain.reshape(1, -1), cos2, sin2, rope=False)
    v_c = _proj(h_c, w_v)
    att = _attention(qk_l, v_l, k_c, v_c, odd_sink[0], batch, n_heads)
    x_l = _out_res([att], _cast_col_tiles(odd_w_out, 0, 0, d, TN_WIDE), x_l, mod_l(1, 2), seq)

    return _ffn(x_l, mod_l(1, 3), mod_l(1, 4), mod_l(1, 5), wgu_next, wd_next, seq).reshape(batch, seq, d)
```

```python
import functools

import numpy as np
import jax
import jax.numpy as jnp
from jax import lax
from jax.experimental import pallas as pl
from jax.experimental.pallas import tpu as pltpu

F32 = jnp.float32
BF16 = jnp.bfloat16

EPS = 1e-6
LOG2E = float(np.log2(np.e))
N_MOD = 6
GRID_W = 64
ROPE_THETA = 10000.0

GLA_HEADS = 4
GLA_TAU = 16.0
GLA_GATE_RANK = 16
GLA_BLOCK = 64
GLA_TILE = 256
SGU_GROUPS = 8
SGU_CHUNK = 128
HEAD_DIM = 128
ATT_GROUP = 4
WINDOW = 128
ATT_BLOCK = 128
ATT_QB = 8

LANE = 128
MXU_W = 256
VMEM_LIMIT = 56 * 1024 * 1024
FFN_VMEM_LIMIT = 60 * 1024 * 1024

TM = 1024
TN = 512
TN_WIDE = 1024
FFN_TH = 256
X_ROWS = 128
X_GROUP = 16
ROWS_EW = 512


def _cparams(*sem):
    return pltpu.CompilerParams(dimension_semantics=sem, vmem_limit_bytes=VMEM_LIMIT)


def _dot(a, b):
    return jnp.dot(a, b, preferred_element_type=F32)


def _dot_nt(a, b):
    return lax.dot_general(a, b, (((1,), (1,)), ((), ())), preferred_element_type=F32)


def _dot_tn(a, b):
    return lax.dot_general(a, b, (((0,), (0,)), ((), ())), preferred_element_type=F32)


def _silu(x):
    return x * jax.nn.sigmoid(x)


def _gelu_tanh(x):
    c = np.float32(np.sqrt(2.0 / np.pi))
    return x * (0.5 * (1.0 + jnp.tanh(c * (x + 0.044715 * (x * x * x)))))


def _rms(x):
    return x * lax.rsqrt(jnp.mean(x * x, axis=-1, keepdims=True) + EPS)


def _ada_kernel(c_ref, w_ref, b_ref, o_ref):
    a = _silu(c_ref[...]).astype(BF16)
    o_ref[...] = _dot(a, w_ref[...].astype(BF16)) + b_ref[...]


def _ada(cc, ada_w, ada_b):
    depth, d, n = ada_w.shape
    rows = cc.shape[0]
    return pl.pallas_call(
        _ada_kernel,
        grid=(depth, n // TN),
        in_specs=[pl.BlockSpec((rows, d), lambda l, j: (0, 0)),
                  pl.BlockSpec((None, d, TN), lambda l, j: (l, 0, j)),
                  pl.BlockSpec((None, 1, TN), lambda l, j: (l, 0, j))],
        out_specs=pl.BlockSpec((None, rows, TN), lambda l, j: (l, 0, j)),
        out_shape=jax.ShapeDtypeStruct((depth, rows, n), F32),
        compiler_params=_cparams("parallel", "parallel"),
        name="ada",
    )(cc, ada_w, ada_b.reshape(depth, 1, n))


def _modulate_kernel(x_ref, sh_ref, sc_ref, h_ref):
    h_ref[...] = (_rms(x_ref[...]) * (1.0 + sc_ref[...]) + sh_ref[...]).astype(h_ref.dtype)


def _row_spec(d):
    return pl.BlockSpec((ROWS_EW, d), lambda i: (i, 0))


def _vec_spec(d, rows_per_vec):
    return pl.BlockSpec((None, 1, d), lambda i: ((i * ROWS_EW) // rows_per_vec, 0, 0))


def _modulate(x, shift, scale, rows_per_vec):
    m, d = x.shape
    return pl.pallas_call(
        _modulate_kernel, grid=(m // ROWS_EW,),
        in_specs=[_row_spec(d), _vec_spec(d, rows_per_vec), _vec_spec(d, rows_per_vec)],
        out_specs=_row_spec(d),
        out_shape=jax.ShapeDtypeStruct((m, d), BF16),
        compiler_params=_cparams("parallel"), name="modulate",
    )(x, shift, scale)


def _cast_kernel(w_ref, o_ref):
    o_ref[...] = w_ref[...].astype(o_ref.dtype)


def _cast_col_tiles(w, layer, col0, n_cols, tile):
    k = w.shape[1]
    assert col0 % tile == 0 and n_cols % tile == 0
    t0 = col0 // tile
    return pl.pallas_call(
        _cast_kernel, grid=(n_cols // tile,),
        in_specs=[pl.BlockSpec((None, k, tile), lambda j: (layer, 0, t0 + j))],
        out_specs=pl.BlockSpec((None, k, tile), lambda j: (j, 0, 0)),
        out_shape=jax.ShapeDtypeStruct((n_cols // tile, k, tile), BF16),
        compiler_params=_cparams("parallel"), name="cast_cols",
    )(w)


def _cast_row_tiles(wt, layer, row0, n_rows, tile):
    k = wt.shape[2]
    assert row0 % 8 == 0 and n_rows % tile == 0
    return pl.pallas_call(
        _cast_kernel, grid=(n_rows // tile,),
        in_specs=[pl.BlockSpec((pl.Element(1), pl.Element(tile), pl.Element(k)),
                               lambda j: (layer, pl.multiple_of(row0 + j * tile, 8), 0))],
        out_specs=pl.BlockSpec((1, tile, k), lambda j: (j, 0, 0)),
        out_shape=jax.ShapeDtypeStruct((n_rows // tile, tile, k), BF16),
        compiler_params=_cparams("parallel"), name="cast_row_tiles",
    )(wt)


def _proj_kernel(x_ref, w_ref, *rest, transposed, n_side_blocks):
    o_ref = rest[len(rest) // 2]
    dot = _dot_nt if transposed else _dot
    o_ref[...] = dot(x_ref[...], w_ref[...]).astype(o_ref.dtype)

    if n_side_blocks:
        side_in, side_out = rest[:len(rest) // 2], rest[len(rest) // 2 + 1]
        step = pl.program_id(0) * pl.num_programs(1) + pl.program_id(1)

        @pl.when(step < n_side_blocks)
        def _():
            width = side_in[0].shape[1]
            for idx, src in enumerate(side_in):
                side_out[:, idx * width:(idx + 1) * width] = src[...].astype(side_out.dtype)


def _proj(x, w3, transposed=False, side_cast=None):
    m, k = x.shape
    nt = w3.shape[0]
    tn = w3.shape[1] if transposed else w3.shape[2]
    tm = min(TM, m)
    in_specs = [pl.BlockSpec((tm, k), lambda i, j: (i, 0)),
                pl.BlockSpec((None,) + w3.shape[1:], lambda i, j: (j, 0, 0))]
    out_specs = [pl.BlockSpec((tm, tn), lambda i, j: (i, j))]
    out_shape = [jax.ShapeDtypeStruct((m, nt * tn), BF16)]
    args, n_blocks = [x, w3], 0
    if side_cast is not None:
        kind, *mats, layer = side_cast
        d, hidden = (mats[0].shape[1], mats[0].shape[2]) if kind == "pair" else (mats[0].shape[2], mats[0].shape[1])
        n_tiles = hidden // FFN_TH
        halves = 2 if kind == "pair" else 1
        n_blocks = n_tiles * halves
        assert n_blocks <= (m // tm) * nt

        def blk(i, j):
            return jnp.minimum(i * nt + j, n_blocks - 1)

        if kind == "pair":
            src = pl.BlockSpec((None, d // halves, FFN_TH), lambda i, j: (layer, blk(i, j) % halves, blk(i, j) // halves))
            in_specs += [src, src]
            out_specs.append(pl.BlockSpec((None, d // halves, 2 * FFN_TH),
                                          lambda i, j: (blk(i, j) // halves, blk(i, j) % halves, 0)))
            out_shape.append(jax.ShapeDtypeStruct((n_tiles, d, 2 * FFN_TH), BF16))
        else:
            in_specs.append(pl.BlockSpec((None, FFN_TH, d), lambda i, j: (layer, blk(i, j), 0)))
            out_specs.append(pl.BlockSpec((FFN_TH, d), lambda i, j: (blk(i, j), 0)))
            out_shape.append(jax.ShapeDtypeStruct((hidden, d), BF16))
        args += mats
    res = pl.pallas_call(
        functools.partial(_proj_kernel, transposed=transposed, n_side_blocks=n_blocks), grid=(m // tm, nt),
        in_specs=in_specs, out_specs=out_specs, out_shape=out_shape,
        compiler_params=_cparams("parallel", "arbitrary" if n_blocks else "parallel"), name="proj",
    )(*args)
    return res[0] if side_cast is None else res


def _qk_proj_kernel(x_ref, w_ref, gain_ref, cos_ref, sin_ref, ones_ref, o_ref, *, rope):
    half = x_ref.shape[0] // 2
    for rr in range(2):
        rows = slice(rr * half, (rr + 1) * half)
        x = x_ref[rows, :]
        for cc in range(w_ref.shape[1] // TN):
            acc = _dot(x, w_ref[:, cc * TN:(cc + 1) * TN])
            for tt in range(TN // MXU_W):
                c0 = cc * TN + tt * MXU_W
                a = acc[:, tt * MXU_W:(tt + 1) * MXU_W]
                ssq = _dot((a * a).astype(BF16), ones_ref[...])
                y = a * lax.rsqrt(ssq * (1.0 / HEAD_DIM) + EPS) * gain_ref[:, c0:c0 + MXU_W]
                for hh in range(MXU_W // HEAD_DIM):
                    yh = y[:, hh * HEAD_DIM:(hh + 1) * HEAD_DIM]
                    if rope:
                        yh = (yh * cos_ref[rows, :]
                              + pltpu.roll(yh, HEAD_DIM // 2, 1) * sin_ref[rows, :])
                    o_ref[rows, c0 + hh * HEAD_DIM:c0 + (hh + 1) * HEAD_DIM] = yh.astype(o_ref.dtype)


def _qk_proj(x, w3, tile0, n_tiles, gain, cos2, sin2, rope):
    m, k = x.shape
    tn = w3.shape[2]
    tm = min(TM, m)
    seq_blocks = cos2.shape[0] // tm if rope else 1
    tab = pl.BlockSpec((tm, HEAD_DIM), lambda i, j: (i % seq_blocks, 0))
    head_of = np.arange(MXU_W) // HEAD_DIM
    ones = jnp.asarray(head_of[:, None] == head_of[None, :], BF16)
    return pl.pallas_call(
        functools.partial(_qk_proj_kernel, rope=rope), grid=(m // tm, n_tiles),
        in_specs=[pl.BlockSpec((tm, k), lambda i, j: (i, 0)),
                  pl.BlockSpec((None, k, tn), lambda i, j: (tile0 + j, 0, 0)),
                  pl.BlockSpec((1, tn), lambda i, j: (0, j)),
                  tab, tab,
                  pl.BlockSpec((MXU_W, MXU_W), lambda i, j: (0, 0))],
        out_specs=pl.BlockSpec((tm, tn), lambda i, j: (i, j)),
        out_shape=jax.ShapeDtypeStruct((m, n_tiles * tn), BF16),
        compiler_params=_cparams("parallel", "parallel"), name="qk_proj",
    )(x, w3, gain, cos2, sin2, ones)


def _out_res_kernel(*refs, n_in):
    xs, ws = refs[:n_in], refs[n_in:2 * n_in]
    res_ref, g_ref, o_ref = refs[2 * n_in:]
    acc = _dot(xs[0][...], ws[0][...])
    for x_ref, w_ref in zip(xs[1:], ws[1:]):
        acc = acc + _dot(x_ref[...], w_ref[...])
    o_ref[...] = res_ref[...] + g_ref[...] * acc


def _out_res(xs, w3, res, gate, rows_per_vec):
    m, n = res.shape
    nt, k, tn = w3.shape
    tm = min(TM, m)
    n_in = len(xs)
    kx = k // n_in
    assert all(x.shape[1] == kx for x in xs) and nt * tn == n

    def w_spec(part):
        return pl.BlockSpec((None, kx, tn), lambda i, j: (j, part, 0))

    in_specs = ([pl.BlockSpec((tm, kx), lambda i, j: (i, 0)) for _ in xs]
                + [w_spec(part) for part in range(n_in)]
                + [pl.BlockSpec((tm, tn), lambda i, j: (i, j)),
                   pl.BlockSpec((None, 1, tn), lambda i, j: ((i * tm) // rows_per_vec, 0, j))])
    return pl.pallas_call(
        functools.partial(_out_res_kernel, n_in=n_in), grid=(m // tm, nt),
        in_specs=in_specs,
        out_specs=pl.BlockSpec((tm, tn), lambda i, j: (i, j)),
        out_shape=jax.ShapeDtypeStruct((m, n), F32),
        compiler_params=_cparams("parallel", "parallel"), name="out_res",
    )(*xs, *([w3] * n_in), res, gate)


def _ffn_kernel(x_hbm, sh_ref, sc_ref, gate_ref, wgu_ref, wd_ref, *rest):
    if len(rest) == 3:
        o_ref, h_ref, sem = rest
        side = None
    else:
        wg_next, wu_next, wd_next, o_ref, wgu_next_out, wd_next_out, h_ref, sem = rest
        side = True
    i, j = pl.program_id(0), pl.program_id(1)
    tm = o_ref.shape[0]
    n_chunks = tm // X_ROWS

    def x_copy(ci):
        rows = pl.ds(ci * X_ROWS, X_ROWS)
        src = x_hbm.at[pl.ds(i * tm + ci * X_ROWS, X_ROWS), :]
        return pltpu.make_async_copy(src, o_ref.at[rows, :], sem.at[ci])

    @pl.when(j == 0)
    def _():
        for ci in range(n_chunks):
            x_copy(ci).start()
        for ci in range(n_chunks):
            x_copy(ci).wait()

            def group(gi, carry, ci=ci):
                rows = pl.ds(pl.multiple_of(ci * X_ROWS + gi * X_GROUP, X_GROUP), X_GROUP)
                h_ref[rows, :] = (_rms(o_ref[rows, :]) * (1.0 + sc_ref[...]) + sh_ref[...]).astype(BF16)
                return carry
            lax.fori_loop(0, X_ROWS // X_GROUP, group, 0, unroll=True)

    gu = _dot(h_ref[...], wgu_ref[...])
    a = (_silu(gu[:, :FFN_TH]) * gu[:, FFN_TH:]).astype(BF16)
    o_ref[...] += gate_ref[...] * _dot(a, wd_ref[...])

    if side:
        wgu_next_out[:, :FFN_TH] = wg_next[...].astype(BF16)
        wgu_next_out[:, FFN_TH:] = wu_next[...].astype(BF16)
        wd_next_out[...] = wd_next[...].astype(BF16)


def _ffn(x, shift, scale, gate, wgu3, wd, rows_per_vec, cast_next=None):
    m, d = x.shape
    n_tiles = wgu3.shape[0]
    tm = min(TM, m)
    n_rows = m // tm
    vec = pl.BlockSpec((None, 1, d), lambda i, j: ((i * tm) // rows_per_vec, 0, 0))
    in_specs = [pl.BlockSpec(memory_space=pl.ANY),
                vec, vec, vec,
                pl.BlockSpec((None, d, 2 * FFN_TH), lambda i, j: (j, 0, 0)),
                pl.BlockSpec((FFN_TH, d), lambda i, j: (j, 0))]
    out_specs = [pl.BlockSpec((tm, d), lambda i, j: (i, 0))]
    out_shape = [jax.ShapeDtypeStruct((m, d), F32)]
    args = [x, shift, scale, gate, wgu3, wd]
    if cast_next is not None:
        w_gate, w_up, w_down, layer = cast_next
        blk = d // n_rows
        assert d % n_rows == 0 and blk % LANE == 0
        col_tile = pl.BlockSpec((None, blk, FFN_TH), lambda i, j: (layer, i, j))
        in_specs += [col_tile, col_tile, pl.BlockSpec((None, FFN_TH, blk), lambda i, j: (layer, j, i))]
        out_specs += [pl.BlockSpec((None, blk, 2 * FFN_TH), lambda i, j: (j, i, 0)),
                      pl.BlockSpec((FFN_TH, blk), lambda i, j: (j, i))]
        out_shape += [jax.ShapeDtypeStruct(wgu3.shape, BF16), jax.ShapeDtypeStruct(wd.shape, BF16)]
        args += [w_gate, w_up, w_down]
    res = pl.pallas_call(
        _ffn_kernel, grid=(n_rows, n_tiles),
        in_specs=in_specs, out_specs=out_specs, out_shape=out_shape,
        scratch_shapes=[pltpu.VMEM((tm, d), BF16), pltpu.SemaphoreType.DMA((tm // X_ROWS,))],
        compiler_params=pltpu.CompilerParams(dimension_semantics=("parallel", "arbitrary"),
                                             vmem_limit_bytes=FFN_VMEM_LIMIT), name="ffn",
    )(*args)
    return res[0] if cast_next is None else res


def _gla_scan_tile(q_ref, k_ref, v_ref, g_ref, w2_ref, b2_ref, s_ref, emit, reverse):
    dk = q_ref.shape[1] // GLA_HEADS
    dv = v_ref.shape[1] // GLA_HEADS
    row = lax.broadcasted_iota(jnp.int32, (GLA_BLOCK, GLA_BLOCK), 0)
    col = lax.broadcasted_iota(jnp.int32, (GLA_BLOCK, GLA_BLOCK), 1)
    if reverse:
        tri = (col >= row).astype(BF16)
        mask = col > row
        last_row = 0
    else:
        tri = (col <= row).astype(BF16)
        mask = col <= row
        last_row = GLA_BLOCK - 1
    n_steps = q_ref.shape[0] // GLA_BLOCK
    order = [n_steps - 1 - st if reverse else st for st in range(n_steps)]

    parts = []
    for c in order:
        rows = slice(c * GLA_BLOCK, (c + 1) * GLA_BLOCK)
        q = q_ref[rows, :].astype(F32) * (dk ** -0.5)
        k = k_ref[rows, :].astype(F32)
        x = _dot(g_ref[rows, :], w2_ref[...]) + b2_ref[...]
        log_a = (jnp.minimum(x, 0.0) - jnp.log(1.0 + jnp.exp(-jnp.abs(x)))) * (1.0 / GLA_TAU)
        hi = log_a.astype(BF16)
        lo = (log_a - hi.astype(F32)).astype(BF16)
        b = _dot(tri, hi) + _dot(tri, lo)
        mid = b[GLA_BLOCK // 2:GLA_BLOCK // 2 + 1, :]
        b_last = b[last_row:last_row + 1, :]
        qe = q * jnp.exp(b - mid)
        ke = k * jnp.exp(mid - b)
        q_in = (qe * jnp.exp(mid)).astype(BF16)
        k_out = (ke * jnp.exp(b_last - mid)).astype(BF16)
        decay = jnp.exp(b_last)
        qe, ke = qe.astype(BF16), ke.astype(BF16)
        intra, kv = [], []
        for h in range(GLA_HEADS):
            ks = slice(h * dk, (h + 1) * dk)
            v = v_ref[rows, h * dv:(h + 1) * dv]
            att = jnp.where(mask, _dot_nt(qe[:, ks], ke[:, ks]), 0.0).astype(BF16)
            intra.append(_dot(att, v))
            kv.append(_dot_tn(v, k_out[:, ks]))
        parts.append((rows, q_in, decay, intra, kv))

    for h in range(GLA_HEADS):
        ks = slice(h * dk, (h + 1) * dk)
        s = s_ref[h]
        for rows, q_in, decay, intra, kv in parts:
            emit(rows, h, intra[h] + _dot_nt(q_in[:, ks], s.astype(BF16)))
            s = s * decay[:, ks] + kv[h]
        s_ref[h] = s


def _gla_rev_kernel(qc_ref, kc_ref, vc_ref, gc_ref, ql_ref, kl_ref, vl_ref, gl_ref, w2_ref, b2_ref,
                    oc_ref, ol_ref, s_ref):
    t = pl.program_id(1)
    dv = vc_ref.shape[1] // GLA_HEADS

    def run(q_ref, k_ref, v_ref, g_ref, o_ref):
        def emit(rows, h, o):
            o_ref[rows, h * dv:(h + 1) * dv] = o
        _gla_scan_tile(q_ref, k_ref, v_ref, g_ref, w2_ref, b2_ref, s_ref, emit, reverse=True)

    @pl.when(t == 0)
    def _():
        s_ref[...] = jnp.zeros_like(s_ref)
        run(qc_ref, kc_ref, vc_ref, gc_ref, oc_ref)

    @pl.when(t > 0)
    def _():
        run(ql_ref, kl_ref, vl_ref, gl_ref, ol_ref)


def _gla_fwd_kernel(qc_ref, kc_ref, vc_ref, rc_ref, gc_ref, pc_ref, ql_ref, kl_ref, vl_ref, rl_ref, gl_ref, pl_ref,
                    w2_ref, b2_ref, gain_ref, oc_ref, ol_ref, s_ref):
    t = pl.program_id(1)
    dv = vc_ref.shape[1] // GLA_HEADS

    def run(q_ref, k_ref, v_ref, r_ref, g_ref, p_ref, o_ref):
        def emit(rows, h, o):
            cols = slice(h * dv, (h + 1) * dv)
            total = o + p_ref[rows, cols]
            r = r_ref[rows, cols].astype(F32)
            o_ref[rows, cols] = (_rms(total) * gain_ref[...] * _silu(r)).astype(o_ref.dtype)
        _gla_scan_tile(q_ref, k_ref, v_ref, g_ref, w2_ref, b2_ref, s_ref, emit, reverse=False)

    @pl.when(t == 0)
    def _():
        s_ref[...] = jnp.zeros_like(s_ref)
        run(qc_ref, kc_ref, vc_ref, rc_ref, gc_ref, pc_ref, oc_ref)

    @pl.when(t > 0)
    def _():
        run(ql_ref, kl_ref, vl_ref, rl_ref, gl_ref, pl_ref, ol_ref)


def _gla(p_l, p_c, g_l, g_c, w2f, b2f, w2b, b2b, gain, batch, dk, dv):
    seq, ctx_len = p_l.shape[0] // batch, p_c.shape[0] // batch
    assert ctx_len == GLA_TILE and seq % GLA_TILE == 0
    n_lat = seq // GLA_TILE
    key_w, val_w = GLA_HEADS * dk, GLA_HEADS * dv
    assert val_w == 2 * key_w

    def lat_fwd(b, t):
        return b * n_lat + jnp.maximum(t - 1, 0)

    def lat_rev(b, t):
        return b * n_lat + n_lat - jnp.maximum(t, 1)

    def tok_specs(row_blk, with_r):
        specs = [pl.BlockSpec((GLA_TILE, key_w), lambda b, t: (row_blk(b, t), 0)),
                 pl.BlockSpec((GLA_TILE, key_w), lambda b, t: (row_blk(b, t), 1)),
                 pl.BlockSpec((GLA_TILE, val_w), lambda b, t: (row_blk(b, t), 1))]
        if with_r:
            specs.append(pl.BlockSpec((GLA_TILE, val_w), lambda b, t: (row_blk(b, t), 2)))
        return specs + [pl.BlockSpec((GLA_TILE, LANE), lambda b, t: (row_blk(b, t), 0))]

    def out_spec(row_blk):
        return pl.BlockSpec((GLA_TILE, val_w), lambda b, t: (row_blk(b, t), 0))

    def ctx_blk(b, t):
        return b

    w2_spec = pl.BlockSpec((LANE, key_w), lambda b, t: (0, 0))
    b2_spec = pl.BlockSpec((1, key_w), lambda b, t: (0, 0))
    state = pltpu.VMEM((GLA_HEADS, dv, dk), F32)
    grid = (batch, 1 + n_lat)
    rev_c, rev_l = pl.pallas_call(
        _gla_rev_kernel, grid=grid,
        in_specs=tok_specs(ctx_blk, False) + tok_specs(lat_rev, False) + [w2_spec, b2_spec],
        out_specs=[out_spec(ctx_blk), out_spec(lat_rev)],
        out_shape=[jax.ShapeDtypeStruct((batch * ctx_len, val_w), F32),
                   jax.ShapeDtypeStruct((batch * seq, val_w), F32)],
        scratch_shapes=[state],
        compiler_params=_cparams("parallel", "arbitrary"), name="gla_rev",
    )(p_c, p_c, p_c, g_c, p_l, p_l, p_l, g_l, w2b, b2b)
    return pl.pallas_call(
        _gla_fwd_kernel, grid=grid,
        in_specs=tok_specs(ctx_blk, True) + [out_spec(ctx_blk)] + tok_specs(lat_fwd, True) + [out_spec(lat_fwd)]
        + [w2_spec, b2_spec, pl.BlockSpec((1, dv), lambda b, t: (0, 0))],
        out_specs=[out_spec(ctx_blk), out_spec(lat_fwd)],
        out_shape=[jax.ShapeDtypeStruct((batch * ctx_len, val_w), BF16),
                   jax.ShapeDtypeStruct((batch * seq, val_w), BF16)],
        scratch_shapes=[state],
        compiler_params=_cparams("parallel", "arbitrary"), name="gla_fwd",
    )(p_c, p_c, p_c, p_c, g_c, rev_c, p_l, p_l, p_l, p_l, g_l, rev_l, w2f, b2f, gain)


def _sgu_kernel(u_ref, s_ref, ws_ref, bs_ref, gain_ref, o_ref):
    gw = u_ref.shape[1] // SGU_GROUPS
    for ci in range(u_ref.shape[0] // SGU_CHUNK):
        rows = slice(ci * SGU_CHUNK, (ci + 1) * SGU_CHUNK)
        for g in range(SGU_GROUPS):
            cols = slice(g * gw, (g + 1) * gw)
            vg = _rms(_gelu_tanh(s_ref[rows, cols].astype(F32))) * gain_ref[:, cols]
            mixed = _dot(ws_ref[g], vg.astype(BF16)) + bs_ref[:, g:g + 1]
            o_ref[rows, cols] = (_gelu_tanh(u_ref[rows, cols].astype(F32)) * mixed).astype(o_ref.dtype)


def _sgu(p, w_s, b_s_t, gain, width, u_blk, s_blk):
    m = p.shape[0]
    rows = 2 * SGU_CHUNK
    return pl.pallas_call(
        _sgu_kernel, grid=(m // rows,),
        in_specs=[pl.BlockSpec((rows, width), lambda i: (i, u_blk)),
                  pl.BlockSpec((rows, width), lambda i: (i, s_blk)),
                  pl.BlockSpec(w_s.shape, lambda i: (0, 0, 0)),
                  pl.BlockSpec(b_s_t.shape, lambda i: (0, 0)),
                  pl.BlockSpec((1, width), lambda i: (0, 0))],
        out_specs=pl.BlockSpec((rows, width), lambda i: (i, 0)),
        out_shape=jax.ShapeDtypeStruct((m, width), BF16),
        compiler_params=_cparams("parallel"), name="sgu",
    )(p, p, w_s, b_s_t, gain)


def _attn_kernel(sink_ref, bias_ref, q_ref, k_ref, v_ref, kc_ref, vc_ref, o_ref):
    kv, ng = pl.program_id(1), pl.program_id(2)
    seq = k_ref.shape[0]
    nb = seq // ATT_BLOCK
    band = 3 * ATT_BLOCK
    sink = jnp.concatenate([jnp.full((ATT_BLOCK, HEAD_DIM), sink_ref[kv * ATT_GROUP + g] * LOG2E, F32)
                            for g in range(ATT_GROUP)], axis=0)
    k_ctx, v_ctx = kc_ref[...], vc_ref[...]
    ones = jnp.ones((band + k_ctx.shape[0], HEAD_DIM), BF16)
    for jj in range(ATT_QB):
        n = ng * ATT_QB + jj
        start = pl.multiple_of(jnp.clip((n - 1) * ATT_BLOCK, 0, seq - band), ATT_BLOCK)
        if jj == 0:
            bias = bias_ref[jnp.where(n == 0, 1, 0)]
        elif jj == ATT_QB - 1:
            bias = bias_ref[jnp.where(n == nb - 1, 2, 0)]
        else:
            bias = bias_ref[0]
        q = q_ref[jj * ATT_BLOCK:(jj + 1) * ATT_BLOCK, :]
        qs = jnp.concatenate([q[:, g * HEAD_DIM:(g + 1) * HEAD_DIM] for g in range(ATT_GROUP)], axis=0)
        keys = jnp.concatenate([k_ref[pl.ds(start, band), :], k_ctx], axis=0)
        vals = jnp.concatenate([v_ref[pl.ds(start, band), :], v_ctx], axis=0)
        s = _dot_nt(qs, keys)
        s = jnp.concatenate([s[:, :band] + bias, s[:, band:]], axis=1)
        m = jnp.maximum(sink, jnp.max(s, axis=-1, keepdims=True))
        p = jnp.exp2(s - jnp.concatenate([m] * (s.shape[1] // HEAD_DIM), axis=1)).astype(BF16)
        ov = _dot(p, jnp.concatenate([vals, ones], axis=1))
        o = ov[:, :HEAD_DIM] / (ov[:, HEAD_DIM:] + jnp.exp2(sink - m))
        o_ref[jj * ATT_BLOCK:(jj + 1) * ATT_BLOCK, :] = jnp.concatenate(
            [o[g * ATT_BLOCK:(g + 1) * ATT_BLOCK] for g in range(ATT_GROUP)], axis=1).astype(o_ref.dtype)


def _band_bias():
    t = (np.arange(ATT_GROUP * ATT_BLOCK) % ATT_BLOCK)[:, None]
    i = np.arange(3 * ATT_BLOCK)[None, :]
    shifts = (-ATT_BLOCK, 0, -2 * ATT_BLOCK)
    return jnp.asarray(np.stack([np.where(np.abs(i + sh - t) <= WINDOW, 0.0, -np.inf) for sh in shifts]), F32)


def _attention(qk, v, k_c, v_c, sink, batch, n_q_heads):
    m = qk.shape[0]
    seq, ctx_len = m // batch, k_c.shape[0] // batch
    kv_heads = n_q_heads // ATT_GROUP
    ng = seq // (ATT_BLOCK * ATT_QB)
    assert ng >= 2
    qw = ATT_GROUP * HEAD_DIM
    bias = _band_bias()
    return pl.pallas_call(
        _attn_kernel, grid=(batch, kv_heads, ng),
        in_specs=[pl.BlockSpec(memory_space=pltpu.SMEM),
                  pl.BlockSpec(bias.shape, lambda b, h, n: (0, 0, 0)),
                  pl.BlockSpec((ATT_QB * ATT_BLOCK, qw), lambda b, h, n: (b * ng + n, h)),
                  pl.BlockSpec((seq, HEAD_DIM), lambda b, h, n: (b, n_q_heads + h)),
                  pl.BlockSpec((seq, HEAD_DIM), lambda b, h, n: (b, h)),
                  pl.BlockSpec((ctx_len, HEAD_DIM), lambda b, h, n: (b, h)),
                  pl.BlockSpec((ctx_len, HEAD_DIM), lambda b, h, n: (b, h))],
        out_specs=pl.BlockSpec((ATT_QB * ATT_BLOCK, qw), lambda b, h, n: (b * ng + n, h)),
        out_shape=jax.ShapeDtypeStruct((m, n_q_heads * HEAD_DIM), BF16),
        compiler_params=_cparams("parallel", "parallel", "arbitrary"), name="attention",
    )(sink, bias, qk, qk, v, k_c, v_c)


def _rope_tables(seq):
    rows = seq // GRID_W
    row = jnp.repeat(jnp.arange(rows, dtype=F32), GRID_W)
    col = jnp.tile(jnp.arange(GRID_W, dtype=F32), rows)
    n_freq = HEAD_DIM // 4
    inv_freq = ROPE_THETA ** (-jnp.arange(n_freq, dtype=F32) / n_freq)
    ang = jnp.concatenate([row[:, None] * inv_freq, col[:, None] * inv_freq], axis=-1)
    cos, sin = jnp.cos(ang), jnp.sin(ang)
    return jnp.concatenate([cos, cos], axis=-1), jnp.concatenate([-sin, sin], axis=-1)


def kernel(x, c, ctx, c_ctx, ada_w, ada_b, ffn_w_gate, ffn_w_up, ffn_w_down, even_w_in, even_gate_w2_fwd, even_gate_b_fwd, even_gate_w2_bwd, even_gate_b_bwd, even_gla_norm_gain, even_sgu_norm_gain, even_sgu_w_s, even_sgu_b_s, even_w_out, odd_w_in, odd_q_norm_gain, odd_k_norm_gain, odd_sink, odd_w_out):
    batch, seq, d = x.shape
    ctx_len = ctx.shape[1]
    depth = ada_w.shape[0]
    assert depth == 2 and batch + 1 <= 8
    xl = x.reshape(batch * seq, d)
    xc = ctx.reshape(batch * ctx_len, d)

    cc = jnp.concatenate([c, c_ctx[None], jnp.zeros((8 - batch - 1, d), F32)], axis=0)
    mods = _ada(cc, ada_w, ada_b)

    def mod_l(layer, which):
        return mods[layer, :batch, which * d:(which + 1) * d].reshape(batch, 1, d)

    def mod_c(layer, which):
        return mods[layer, batch:batch + 1, which * d:(which + 1) * d].reshape(1, 1, d)

    all_ctx = batch * ctx_len

    dv = even_gla_norm_gain.shape[1]
    key_w = even_gate_w2_fwd.shape[2]
    dk = key_w // GLA_HEADS
    val_w = GLA_HEADS * dv
    sgu_w = even_sgu_norm_gain.shape[1]
    gate0 = 2 * key_w + 2 * val_w
    gate1 = gate0 + 2 * GLA_GATE_RANK
    w_in_t = jnp.swapaxes(even_w_in, 1, 2)
    w_qkvr = _cast_row_tiles(w_in_t, 0, 0, gate0, TN_WIDE)
    w_us = _cast_row_tiles(w_in_t, 0, gate1, 2 * sgu_w, TN_WIDE)
    w_gate = jnp.pad(w_in_t[0, gate0:gate1], ((0, LANE - 2 * GLA_GATE_RANK), (0, 0))).astype(BF16)[None]
    pad_rows = LANE - GLA_GATE_RANK
    w2f = jnp.pad(even_gate_w2_fwd[0], ((0, pad_rows), (0, 0))).astype(BF16)
    w2b = jnp.pad(even_gate_w2_bwd[0], ((GLA_GATE_RANK, pad_rows - GLA_GATE_RANK), (0, 0))).astype(BF16)
    b2f = even_gate_b_fwd[0].reshape(1, key_w)
    b2b = even_gate_b_bwd[0].reshape(1, key_w)

    h_l = _modulate(xl, mod_l(0, 0), mod_l(0, 1), seq)
    h_c = _modulate(xc, mod_c(0, 0), mod_c(0, 1), all_ctx)
    assert ffn_w_gate.shape[2] % FFN_TH == 0
    p_l, wgu = _proj(h_l, w_qkvr, True, side_cast=("pair", ffn_w_gate, ffn_w_up, 0))
    us_l, wd = _proj(h_l, w_us, True, side_cast=("rows", ffn_w_down, 0))
    p_c, us_c = _proj(h_c, w_qkvr, True), _proj(h_c, w_us, True)
    g_l, g_c = _proj(h_l, w_gate, True), _proj(h_c, w_gate, True)
    gla_c, gla_l = _gla(p_l, p_c, g_l, g_c, w2f, b2f, w2b, b2b, even_gla_norm_gain[0].reshape(1, dv),
                        batch, dk, dv)
    w_s = even_sgu_w_s[0].astype(BF16)
    b_s_t = even_sgu_b_s[0].T
    sgu_gain = even_sgu_norm_gain[0].reshape(1, sgu_w)
    sgu_l = _sgu(us_l, w_s, b_s_t, sgu_gain, sgu_w, 0, 1)
    sgu_c = _sgu(us_c, w_s, b_s_t, sgu_gain, sgu_w, 0, 1)
    assert val_w == sgu_w
    w_out = _cast_col_tiles(even_w_out, 0, 0, d, TN_WIDE)
    x_l = _out_res([gla_l, sgu_l], w_out, xl, mod_l(0, 2), seq)
    x_c = _out_res([gla_c, sgu_c], w_out, xc, mod_c(0, 2), all_ctx)

    x_l, wgu_next, wd_next = _ffn(x_l, mod_l(0, 3), mod_l(0, 4), mod_l(0, 5), wgu, wd, seq,
                                  cast_next=(ffn_w_gate, ffn_w_up, ffn_w_down, 1))
    x_c = _ffn(x_c, mod_c(0, 3), mod_c(0, 4), mod_c(0, 5), wgu, wd, all_ctx)

    h_l = _modulate(x_l, mod_l(1, 0), mod_l(1, 1), seq)
    h_c = _modulate(x_c, mod_c(1, 0), mod_c(1, 1), all_ctx)

    n_heads = odd_sink.shape[1]
    q_w = n_heads * HEAD_DIM
    kv_w = q_w // ATT_GROUP
    qk_tile = TN_WIDE
    assert kv_w == qk_tile
    w_qk = _cast_col_tiles(odd_w_in, 0, 0, q_w + kv_w, qk_tile)
    w_v = _cast_col_tiles(odd_w_in, 0, q_w + kv_w, kv_w, TN_WIDE)
    k_gain = jnp.tile(odd_k_norm_gain[0], kv_w // HEAD_DIM)
    qk_gain = jnp.concatenate([jnp.tile(odd_q_norm_gain[0] * (HEAD_DIM ** -0.5 * LOG2E), n_heads),
                               k_gain]).reshape(1, -1)
    cos2, sin2 = _rope_tables(seq)
    qk_l = _qk_proj(h_l, w_qk, 0, w_qk.shape[0], qk_gain, cos2, sin2, rope=True)
    v_l = _proj(h_l, w_v)
    k_c = _qk_proj(h_c, w_qk, q_w // qk_tile, 1, k_gain.reshape(1, -1), cos2, sin2, rope=False)
    v_c = _proj(h_c, w_v)
    att = _attention(qk_l, v_l, k_c, v_c, odd_sink[0], batch, n_heads)
    x_l = _out_res([att], _cast_col_tiles(odd_w_out, 0, 0, d, TN_WIDE), x_l, mod_l(1, 2), seq)

    return _ffn(x_l, mod_l(1, 3), mod_l(1, 4), mod_l(1, 5), wgu_next, wd_next, seq).reshape(batch, seq, d)
```
